```python
import jax, jax.numpy as jnp
from jax import lax
import numpy as np

D_MODEL = 1024
BATCH = 8
SEQ = 2048
DEPTH = 1

CHUNK = 64
Q_BLOCK = 128
RMS_EPS = 1e-6
FOX_HEADS = 8
FOX_HEAD_DIM = 64
FOX_WIDTH = FOX_HEADS * FOX_HEAD_DIM
MLA_HEADS = 8
MLA_Q_LORA = 384
MLA_KV_LORA = 256
MLA_NOPE_DIM = 64
MLA_ROPE_DIM = 32
MLA_V_DIM = 64
MLA_WIDTH = MLA_HEADS * MLA_V_DIM
ROPE_BASE = 10000.0
MIX_WIDTH = FOX_WIDTH + MLA_WIDTH
IN_SIZES = (FOX_WIDTH, FOX_WIDTH, FOX_WIDTH, FOX_HEADS, MLA_Q_LORA, MLA_KV_LORA, MLA_ROPE_DIM)
IN_COLS = sum(IN_SIZES)
MEM_TOKENS = 256
MEM_HEADS = 4
MEM_HEAD_DIM = 128
MEM_WIDTH = MEM_HEADS * MEM_HEAD_DIM
N_EXPERTS = 32
TOP_K = 4
D_EXPERT = D_MODEL
SWIGLU_LIMIT = 7.0
SWIGLU_ALPHA = 1.702

kernel_name = 'hybrid_fox_mla_moe_stream_encoder'


def rmsnorm(x, g):
    xf = x.astype(jnp.float32)
    y = xf * lax.rsqrt(jnp.mean(xf * xf, axis=-1, keepdims=True) + RMS_EPS)
    return (y * g.astype(jnp.float32)).astype(x.dtype)


def rope_tables(seq_len):
    inv = 1.0 / (ROPE_BASE ** (jnp.arange(0, MLA_ROPE_DIM, 2, dtype=jnp.float32) / MLA_ROPE_DIM))
    ang = jnp.arange(seq_len, dtype=jnp.float32)[:, None] * inv[None, :]
    return jnp.cos(ang), jnp.sin(ang)


def apply_rope(x, cos, sin):
    half = x.shape[-1] // 2
    x1, x2 = x[..., :half], x[..., half:]
    cos = cos.astype(x.dtype)
    sin = sin.astype(x.dtype)
    return jnp.concatenate([x1 * cos - x2 * sin, x2 * cos + x1 * sin], axis=-1)


def to_heads(y, n_heads):
    b, s, _ = y.shape
    return y.reshape(b, s, n_heads, -1).transpose(0, 2, 1, 3)


def from_heads(y):
    b, h, s, d = y.shape
    return y.transpose(0, 2, 1, 3).reshape(b, s, h * d)


def frame_causal(t, s):
    return s <= t


def chunk_causal(t, s):
    return (s // CHUNK) <= (t // CHUNK)


def blocked_attention(q, k, v, mask_fn, decay=None):
    b, h, s_len, dk = q.shape
    dv = v.shape[-1]
    nb = s_len // Q_BLOCK
    scale = dk ** -0.5
    kpos = jnp.arange(s_len, dtype=jnp.int32)
    qb = q.reshape(b, h, nb, Q_BLOCK, dk).transpose(2, 0, 1, 3, 4)
    starts = jnp.arange(nb, dtype=jnp.int32) * Q_BLOCK
    if decay is None:
        xs = (qb, starts)
    else:
        xs = (qb, starts, decay.reshape(b, h, nb, Q_BLOCK).transpose(2, 0, 1, 3))

    def block(args):
        qi, start = args[0], args[1]
        tpos = start + jnp.arange(Q_BLOCK, dtype=jnp.int32)
        logits = jnp.einsum('bhqd,bhkd->bhqk', qi, k).astype(jnp.float32) * scale
        if decay is not None:
            logits = logits + (args[2][..., :, None] - decay[..., None, :])
        allowed = mask_fn(tpos[:, None], kpos[None, :])
        logits = jnp.where(allowed, logits, -jnp.inf)
        p = jax.nn.softmax(logits, axis=-1).astype(v.dtype)
        return jnp.einsum('bhqk,bhkd->bhqd', p, v)

    out = lax.map(block, xs)
    return out.transpose(1, 2, 0, 3, 4).reshape(b, h, s_len, dv)


def parallel_mixer(a, w_in, b_f, g_q_lat, w_uq, g_kv_lat, w_ukv, g_fox_out, g_mla_out, w_o, cos, sin):
    b, s_len, _ = a.shape
    points = np.cumsum(IN_SIZES)[:-1].tolist()
    proj = a @ w_in
    fq, fk, fv, f_logit, q_lat, kv_lat, k_rope = jnp.split(proj, points, axis=-1)

    log_f = jax.nn.log_sigmoid(f_logit.astype(jnp.float32) + b_f.astype(jnp.float32))
    decay = jnp.cumsum(log_f, axis=1).transpose(0, 2, 1)
    o_fox = blocked_attention(to_heads(fq, FOX_HEADS), to_heads(fk, FOX_HEADS),
                              to_heads(fv, FOX_HEADS), frame_causal, decay)
    o_fox = from_heads(o_fox)

    q = (rmsnorm(q_lat, g_q_lat) @ w_uq).reshape(b, s_len, MLA_HEADS, MLA_NOPE_DIM + MLA_ROPE_DIM)
    q_nope, q_pe = q[..., :MLA_NOPE_DIM], q[..., MLA_NOPE_DIM:]
    q_pe = apply_rope(q_pe, cos[:, None, :], sin[:, None, :])
    kv = (rmsnorm(kv_lat, g_kv_lat) @ w_ukv).reshape(b, s_len, MLA_HEADS, MLA_NOPE_DIM + MLA_V_DIM)
    k_nope, v = kv[..., :MLA_NOPE_DIM], kv[..., MLA_NOPE_DIM:]
    k_pe = apply_rope(k_rope, cos, sin)[:, :, None, :]
    k_pe = jnp.broadcast_to(k_pe, (b, s_len, MLA_HEADS, MLA_ROPE_DIM))
    q_full = jnp.concatenate([q_nope, q_pe], axis=-1).transpose(0, 2, 1, 3)
    k_full = jnp.concatenate([k_nope, k_pe], axis=-1).transpose(0, 2, 1, 3)
    o_mla = from_heads(blocked_attention(q_full, k_full, v.transpose(0, 2, 1, 3), chunk_causal))

    mixed = jnp.concatenate([rmsnorm(o_fox, g_fox_out), rmsnorm(o_mla, g_mla_out)], axis=-1)
    return mixed @ w_o


def memory_cross_attention(a, mem_n, w_q, w_kv, w_o):
    q = to_heads(a @ w_q, MEM_HEADS)
    kv = mem_n @ w_kv
    k = to_heads(kv[..., :MEM_WIDTH], MEM_HEADS)
    v = to_heads(kv[..., MEM_WIDTH:], MEM_HEADS)
    logits = jnp.einsum('bhqd,bhmd->bhqm', q, k).astype(jnp.float32) * (MEM_HEAD_DIM ** -0.5)
    p = jax.nn.softmax(logits, axis=-1).astype(v.dtype)
    o = jnp.einsum('bhqm,bhmd->bhqd', p, v)
    return from_heads(o) @ w_o


def moe_ffn(a, w_router, b_router, w_gate_up, b_gate_up, w_down, b_down):
    b, s_len, d = a.shape
    t = a.reshape(-1, d)
    logits = (t @ w_router + b_router).astype(jnp.float32)
    top_v, top_i = lax.top_k(logits, TOP_K)
    top_w = jax.nn.softmax(top_v, axis=-1)
    gates = jnp.sum(jax.nn.one_hot(top_i, N_EXPERTS, dtype=jnp.float32) * top_w[..., None], axis=1)
    gates = gates.astype(t.dtype)
    y = jnp.zeros_like(t)
    for e in range(N_EXPERTS):
        gu = t @ w_gate_up[e] + b_gate_up[e]
        gate = jnp.minimum(gu[:, :D_EXPERT], SWIGLU_LIMIT)
        up = jnp.clip(gu[:, D_EXPERT:], -SWIGLU_LIMIT, SWIGLU_LIMIT)
        act = gate * jax.nn.sigmoid(SWIGLU_ALPHA * gate) * (up + 1.0)
        y = y + gates[:, e:e + 1] * (act @ w_down[e] + b_down[e])
    return y.reshape(b, s_len, d)


def setup_inputs(seed: int = 0) -> dict:
    key = jax.random.key(seed)
    ks = jax.random.split(key, 26)
    L = DEPTH

    def normal(k, shape, scale):
        return jax.random.normal(k, shape, jnp.float32) * scale

    def gain(k, shape):
        return 1.0 + 0.02 * jax.random.normal(k, shape, jnp.float32)

    return {
        'x': normal(ks[0], (BATCH, SEQ, D_MODEL), 1.0),
        'mem': normal(ks[1], (BATCH, MEM_TOKENS, D_MODEL), 1.0),
        'g_mix': gain(ks[2], (L, D_MODEL)),
        'w_in': normal(ks[3], (L, D_MODEL, IN_COLS), D_MODEL ** -0.5),
        'b_f': jax.random.uniform(ks[4], (L, FOX_HEADS), jnp.float32, 1.0, 4.0),
        'g_q_lat': gain(ks[5], (L, MLA_Q_LORA)),
        'w_uq': normal(ks[6], (L, MLA_Q_LORA, MLA_HEADS * (MLA_NOPE_DIM + MLA_ROPE_DIM)), MLA_Q_LORA ** -0.5),
        'g_kv_lat': gain(ks[7], (L, MLA_KV_LORA)),
        'w_ukv': normal(ks[8], (L, MLA_KV_LORA, MLA_HEADS * (MLA_NOPE_DIM + MLA_V_DIM)), MLA_KV_LORA ** -0.5),
        'g_fox_out': gain(ks[9], (L, FOX_WIDTH)),
        'g_mla_out': gain(ks[10], (L, MLA_WIDTH)),
        'w_o': normal(ks[11], (L, MIX_WIDTH, D_MODEL), MIX_WIDTH ** -0.5),
        'g_mem_q': gain(ks[12], (L, D_MODEL)),
        'w_mem_q': normal(ks[13], (L, D_MODEL, MEM_WIDTH), D_MODEL ** -0.5),
        'g_mem_kv': gain(ks[14], (L, D_MODEL)),
        'w_mem_kv': normal(ks[15], (L, D_MODEL, 2 * MEM_WIDTH), D_MODEL ** -0.5),
        'w_mem_o': normal(ks[16], (L, MEM_WIDTH, D_MODEL), MEM_WIDTH ** -0.5),
        'g_ffn': gain(ks[17], (L, D_MODEL)),
        'w_router': normal(ks[18], (L, D_MODEL, N_EXPERTS), D_MODEL ** -0.5),
        'b_router': normal(ks[19], (L, N_EXPERTS), 0.01),
        'w_gate_up': normal(ks[20], (L, N_EXPERTS, D_MODEL, 2 * D_EXPERT), D_MODEL ** -0.5),
        'b_gate_up': normal(ks[21], (L, N_EXPERTS, 2 * D_EXPERT), 0.02),
        'w_down': normal(ks[22], (L, N_EXPERTS, D_EXPERT, D_MODEL), D_EXPERT ** -0.5),
        'b_down': normal(ks[23], (L, N_EXPERTS, D_MODEL), 0.02),
        'g_final': gain(ks[24], (D_MODEL,)),
    }


def reference(x, mem, g_mix, w_in, b_f, g_q_lat, w_uq, g_kv_lat, w_ukv, g_fox_out, g_mla_out, w_o,
              g_mem_q, w_mem_q, g_mem_kv, w_mem_kv, w_mem_o, g_ffn, w_router, b_router,
              w_gate_up, b_gate_up, w_down, b_down, g_final):
    cos, sin = rope_tables(x.shape[1])
    h = x
    for l in range(DEPTH):
        h = h + parallel_mixer(rmsnorm(h, g_mix[l]), w_in[l], b_f[l], g_q_lat[l], w_uq[l],
                               g_kv_lat[l], w_ukv[l], g_fox_out[l], g_mla_out[l], w_o[l], cos, sin)
        h = h + memory_cross_attention(rmsnorm(h, g_mem_q[l]), rmsnorm(mem, g_mem_kv[l]),
                                       w_mem_q[l], w_mem_kv[l], w_mem_o[l])
        h = h + moe_ffn(rmsnorm(h, g_ffn[l]), w_router[l], b_router[l], w_gate_up[l],
                        b_gate_up[l], w_down[l], b_down[l])
    return rmsnorm(h, g_final)
```

```python
import functools

import numpy as np
import jax
import jax.numpy as jnp
from jax import lax
from jax.experimental import pallas as pl
from jax.experimental.pallas import tpu as pltpu

F32 = jnp.float32
BF16 = jnp.bfloat16
U32 = jnp.uint32
I32 = jnp.int32

LANES = 128
RMS_EPS = 1e-6
CHUNK = 64
FOX_HEADS = 8
FOX_HEAD_DIM = 64
FOX_WIDTH = FOX_HEADS * FOX_HEAD_DIM
MLA_HEADS = 8
MLA_Q_LORA = 384
MLA_KV_LORA = 256
MLA_NOPE_DIM = 64
MLA_ROPE_DIM = 32
MLA_V_DIM = 64
MLA_WIDTH = MLA_HEADS * MLA_V_DIM
ROPE_BASE = 10000.0
MEM_HEADS = 4
MEM_HEAD_DIM = 128
MEM_WIDTH = MEM_HEADS * MEM_HEAD_DIM
N_EXPERTS = 32
TOP_K = 4
SWIGLU_LIMIT = 7.0
SWIGLU_ALPHA = 1.702

LOG2E = 1.4426950408889634
NEG = -1e30
FOX_QSCALE = FOX_HEAD_DIM ** -0.5 * LOG2E
MLA_QSCALE = (MLA_NOPE_DIM + MLA_ROPE_DIM) ** -0.5 * LOG2E
MEM_QSCALE = MEM_HEAD_DIM ** -0.5 * LOG2E

VMEM_LIMIT = 56 * 1024 * 1024

TS_PROJ = 512
TS_MIX = 512
TS_MEM = 512
T_ATT = 256
DECAY_BLK = 256
TM_MOE = 256
TD_ROWS = 256
MOE_CH = 512


def _cparams(n_axes):
    return pltpu.CompilerParams(dimension_semantics=("arbitrary",) * n_axes, vmem_limit_bytes=VMEM_LIMIT)


def _dot(a, b):
    return jnp.dot(a, b, preferred_element_type=F32)


def _dot_nt(a, b):
    return lax.dot_general(a, b, (((1,), (1,)), ((), ())), preferred_element_type=F32)


def _rms(x, g):
    return x * lax.rsqrt(jnp.mean(x * x, axis=-1, keepdims=True) + RMS_EPS) * g


def _split3(x):
    hi = x.astype(BF16)
    r1 = x - hi.astype(F32)
    mid = r1.astype(BF16)
    lo = (r1 - mid.astype(F32)).astype(BF16)
    return hi, mid, lo


def _inproj_kernel(x_ref, g_ref, w1_ref, gq_ref, wqa_ref, wqb_ref, gkv_ref, wk_ref, wv_ref, cos_ref, sin_ref,
                   fq_ref, fk_ref, fv_ref, fl_ref, qm_ref, km_ref, vm_ref):
    a = _rms(x_ref[0], g_ref[...]).astype(BF16)

    def proj(lo, hi):
        return _dot(a, w1_ref[:, lo:hi])

    c0 = 3 * FOX_WIDTH
    c1 = c0 + MLA_Q_LORA
    c2 = c1 + MLA_KV_LORA
    fq_ref[0] = (proj(0, FOX_WIDTH) * FOX_QSCALE).astype(BF16)
    fk_ref[0] = proj(FOX_WIDTH, 2 * FOX_WIDTH).astype(BF16)
    fv_ref[0] = proj(2 * FOX_WIDTH, c0).astype(BF16)
    qn = _rms(proj(c0, c1), gq_ref[...]).astype(BF16)
    kvn = _rms(proj(c1, c2), gkv_ref[...]).astype(BF16)
    fl_ref[0] = proj(c2, c2 + LANES)
    cos = cos_ref[...]
    sin = sin_ref[...]
    kpe = proj(c2 + LANES, c2 + 2 * LANES) * cos + proj(c2 + 2 * LANES, c2 + 3 * LANES) * sin
    cos8 = jnp.concatenate([cos] * MLA_HEADS, axis=1)
    sin8 = jnp.concatenate([sin] * MLA_HEADS, axis=1)
    qm = (_dot(qn, wqa_ref[...]) * cos8 + _dot(qn, wqb_ref[...]) * sin8) * MLA_QSCALE
    qm_ref[0] = qm.astype(BF16)
    km_ref[0] = (_dot(kvn, wk_ref[...]) + jnp.concatenate([kpe] * MLA_HEADS, axis=1)).astype(BF16)
    vm_ref[0] = _dot(kvn, wv_ref[...]).astype(BF16)


def _inproj(x, g_mix, w1, g_q, wqa, wqb, g_kv, wk, wv, cos_t, sin_t):
    b, s, d = x.shape
    ts = min(TS_PROJ, s)
    n1 = w1.shape[1]

    def full(arr):
        return pl.BlockSpec(arr.shape, lambda bi, si: (0,) * arr.ndim)

    def tok(width):
        return pl.BlockSpec((1, ts, width), lambda bi, si: (bi, si, 0))

    tab = pl.BlockSpec((ts, LANES), lambda bi, si: (si, 0))
    out_shapes = (
        jax.ShapeDtypeStruct((b, s, FOX_WIDTH), BF16),
        jax.ShapeDtypeStruct((b, s, FOX_WIDTH), BF16),
        jax.ShapeDtypeStruct((b, s, FOX_WIDTH), BF16),
        jax.ShapeDtypeStruct((b, s, LANES), F32),
        jax.ShapeDtypeStruct((b, s, MLA_HEADS * LANES), BF16),
        jax.ShapeDtypeStruct((b, s, MLA_HEADS * LANES), BF16),
        jax.ShapeDtypeStruct((b, s, MLA_WIDTH), BF16),
    )
    return pl.pallas_call(
        _inproj_kernel,
        grid=(b, s // ts),
        in_specs=[tok(d), full(g_mix), full(w1), full(g_q), full(wqa), full(wqb), full(g_kv), full(wk), full(wv),
                  tab, tab],
        out_specs=(tok(FOX_WIDTH), tok(FOX_WIDTH), tok(FOX_WIDTH), tok(LANES), tok(MLA_HEADS * LANES),
                   tok(MLA_HEADS * LANES), tok(MLA_WIDTH)),
        out_shape=out_shapes,
        compiler_params=_cparams(2),
        name="inproj",
    )(x, g_mix, w1, g_q, wqa, wqb, g_kv, wk, wv, cos_t, sin_t)


def _decay_kernel(fl_ref, bf_ref, pq_ref, pk_ref, oq_ref, ok_ref, cq_ref, ck_ref, *, blk):
    s = fl_ref.shape[1]
    row = lax.broadcasted_iota(I32, (blk, blk), 0)
    col = lax.broadcasted_iota(I32, (blk, blk), 1)
    tri = jnp.where(row >= col, 1.0, 0.0).astype(BF16)
    carry = jnp.zeros((1, LANES), F32)
    for i in range(s // blk):
        sl = slice(i * blk, (i + 1) * blk)
        z = fl_ref[0, sl, :] + bf_ref[...]
        lf = (jnp.minimum(z, 0.0) - jnp.log1p(jnp.exp(-jnp.abs(z)))) * LOG2E
        h, m, l = _split3(lf)
        cs = _dot(tri, h) + _dot(tri, m) + _dot(tri, l) + carry
        carry = cs[blk - 1:blk, :]
        h, m, l = _split3(cs)
        cq = _dot(h, pq_ref[0]) + _dot(m, pq_ref[1]) + _dot(l, pq_ref[2]) + oq_ref[...]
        ck = _dot(h, pk_ref[0]) + _dot(m, pk_ref[1]) + _dot(l, pk_ref[2]) + ok_ref[...]
        cq_ref[0, sl, :] = cq.astype(BF16)
        ck_ref[0, sl, :] = ck.astype(BF16)


def _decay_tables():
    pq = np.zeros((3, LANES, LANES), np.float32)
    pk = np.zeros((3, LANES, LANES), np.float32)
    oq = np.zeros((1, LANES), np.float32)
    ok = np.zeros((1, LANES), np.float32)
    for h in range(FOX_HEADS):
        for part in range(3):
            pq[part, h, 16 * h + part] = 1.0
            pk[part, h, 16 * h + 3 + part] = -1.0
            oq[0, 16 * h + 3 + part] = 1.0
            ok[0, 16 * h + part] = 1.0
    return jnp.asarray(pq, BF16), jnp.asarray(pk, BF16), jnp.asarray(oq), jnp.asarray(ok)


def _decay(flog, b_f):
    b, s, _ = flog.shape
    blk = min(DECAY_BLK, s)
    pq, pk, oq, ok = _decay_tables()
    bf = jnp.zeros((1, LANES), F32).at[0, :FOX_HEADS].set(b_f)

    def full(arr):
        return pl.BlockSpec(arr.shape, lambda bi: (0,) * arr.ndim)

    seq = pl.BlockSpec((1, s, LANES), lambda bi: (bi, 0, 0))
    return pl.pallas_call(
        functools.partial(_decay_kernel, blk=blk),
        grid=(b,),
        in_specs=[seq, full(bf), full(pq), full(pk), full(oq), full(ok)],
        out_specs=(seq, seq),
        out_shape=(jax.ShapeDtypeStruct((b, s, LANES), BF16), jax.ShapeDtypeStruct((b, s, LANES), BF16)),
        compiler_params=_cparams(1),
        name="decay",
    )(flog, bf, pq, pk, oq, ok)


def _attn_pair_kernel(*refs, fox, t):
    if fox:
        q_ref, cq_ref, k_ref, ck_ref, v_ref, o_ref = refs
    else:
        q_ref, k_ref, v_ref, o_ref = refs
    p = pl.program_id(1)
    i = pl.program_id(2)
    lane = lax.broadcasted_iota(I32, (1, LANES), 1)
    row = lax.broadcasted_iota(I32, (t, t), 0)
    col = lax.broadcasted_iota(I32, (t, t), 1)
    if fox:
        allowed = col <= row
    else:
        allowed = (col // CHUNK) <= (row // CHUNK)
    outs = []
    for hh in range(2):
        if fox:
            q2 = q_ref[0]
            c2 = cq_ref[0]
            zero = jnp.zeros_like(q2)
            qa = jnp.concatenate([jnp.where((lane // FOX_HEAD_DIM) == hh, q2, zero),
                                  jnp.where((lane // 16) == 2 * p + hh, c2, zero)], axis=1)
        else:
            qa = q_ref[0, :, hh * LANES:(hh + 1) * LANES]

        def kv_block(j, hh=hh):
            st = pl.multiple_of(j * t, t)
            if fox:
                ka = jnp.concatenate([k_ref[0, pl.ds(st, t), :], ck_ref[0, pl.ds(st, t), :]], axis=1)
            else:
                ka = k_ref[0, pl.ds(st, t), hh * LANES:(hh + 1) * LANES]
            return ka, v_ref[0, pl.ds(st, t), :]

        def update(sc, v, carry):
            m, l, acc = carry
            m_new = jnp.maximum(m, jnp.max(sc, axis=-1, keepdims=True))
            alpha = jnp.exp2(m - m_new)
            pm = jnp.exp2(sc - m_new)
            l = alpha * l + jnp.sum(pm, axis=-1, keepdims=True)
            acc = alpha * acc + _dot(pm.astype(BF16), v)
            return m_new, l, acc

        def body(j, carry, qa=qa, kv_block=kv_block, update=update):
            ka, v = kv_block(j)
            return update(_dot_nt(qa, ka), v, carry)

        carry = (jnp.full((t, 1), NEG, F32), jnp.zeros((t, 1), F32), jnp.zeros((t, LANES), F32))
        carry = lax.fori_loop(0, i, body, carry)
        ka, v = kv_block(i)
        sc = jnp.where(allowed, _dot_nt(qa, ka), NEG)
        _, l, acc = update(sc, v, carry)
        outs.append(acc / l)
    o_ref[0] = jnp.where(lane < MLA_V_DIM, outs[0], outs[1]).astype(o_ref.dtype)


def _attention(fox, q, k, v, cq=None, ck=None):
    b, s, _ = v.shape
    t = min(T_ATT, s)
    pairs = v.shape[2] // LANES
    qw = q.shape[2] // pairs
    kw = k.shape[2] // pairs
    qspec = pl.BlockSpec((1, t, qw), lambda bi, pi, qi: (bi, qi, pi))
    kspec = pl.BlockSpec((1, s, kw), lambda bi, pi, qi: (bi, 0, pi))
    vspec = pl.BlockSpec((1, s, LANES), lambda bi, pi, qi: (bi, 0, pi))
    ospec = pl.BlockSpec((1, t, LANES), lambda bi, pi, qi: (bi, qi, pi))
    if fox:
        cqspec = pl.BlockSpec((1, t, LANES), lambda bi, pi, qi: (bi, qi, 0))
        ckspec = pl.BlockSpec((1, s, LANES), lambda bi, pi, qi: (bi, 0, 0))
        in_specs = [qspec, cqspec, kspec, ckspec, vspec]
        args = (q, cq, k, ck, v)
    else:
        in_specs = [qspec, kspec, vspec]
        args = (q, k, v)
    return pl.pallas_call(
        functools.partial(_attn_pair_kernel, fox=fox, t=t),
        grid=(b, pairs, s // t),
        in_specs=in_specs,
        out_specs=ospec,
        out_shape=jax.ShapeDtypeStruct((b, s, pairs * LANES), BF16),
        compiler_params=_cparams(3),
        name="fox_attn" if fox else "mla_attn",
    )(*args)


def _mixout_kernel(of_ref, om_ref, x_ref, gf_ref, gm_ref, wo_ref, gq_ref, wq_ref, h1_ref, qmem_ref):
    nf = _rms(of_ref[...].astype(F32), gf_ref[...]).astype(BF16)
    nm = _rms(om_ref[...].astype(F32), gm_ref[...]).astype(BF16)
    h1 = x_ref[...] + _dot(nf, wo_ref[0:FOX_WIDTH, :]) + _dot(nm, wo_ref[FOX_WIDTH:FOX_WIDTH + MLA_WIDTH, :])
    h1_ref[...] = h1
    a = _rms(h1, gq_ref[...]).astype(BF16)
    qmem_ref[...] = (_dot(a, wq_ref[...]) * MEM_QSCALE).astype(BF16)


def _mixout(o_fox, o_mla, x2, g_fox, g_mla, w_o, g_mq, w_mq):
    n, d = x2.shape
    ts = min(TS_MIX, n)

    def full(arr):
        return pl.BlockSpec(arr.shape, lambda i: (0,) * arr.ndim)

    def tok(width):
        return pl.BlockSpec((ts, width), lambda i: (i, 0))

    return pl.pallas_call(
        _mixout_kernel,
        grid=(n // ts,),
        in_specs=[tok(FOX_WIDTH), tok(MLA_WIDTH), tok(d), full(g_fox), full(g_mla), full(w_o), full(g_mq),
                  full(w_mq)],
        out_specs=(tok(d), tok(MEM_WIDTH)),
        out_shape=(jax.ShapeDtypeStruct((n, d), F32), jax.ShapeDtypeStruct((n, MEM_WIDTH), BF16)),
        compiler_params=_cparams(1),
        name="mixout",
    )(o_fox, o_mla, x2, g_fox, g_mla, w_o, g_mq, w_mq)


def _memkv_kernel(mem_ref, g_ref, w_ref, k_ref, v_ref):
    a = _rms(mem_ref[0], g_ref[...]).astype(BF16)
    k_ref[0] = _dot(a, w_ref[:, 0:MEM_WIDTH]).astype(BF16)
    v_ref[0] = _dot(a, w_ref[:, MEM_WIDTH:2 * MEM_WIDTH]).astype(BF16)


def _memkv(mem, g, w):
    b, m, d = mem.shape
    kv = pl.BlockSpec((1, m, MEM_WIDTH), lambda bi: (bi, 0, 0))
    return pl.pallas_call(
        _memkv_kernel,
        grid=(b,),
        in_specs=[pl.BlockSpec((1, m, d), lambda bi: (bi, 0, 0)), pl.BlockSpec(g.shape, lambda bi: (0, 0)),
                  pl.BlockSpec(w.shape, lambda bi: (0, 0))],
        out_specs=(kv, kv),
        out_shape=(jax.ShapeDtypeStruct((b, m, MEM_WIDTH), BF16), jax.ShapeDtypeStruct((b, m, MEM_WIDTH), BF16)),
        compiler_params=_cparams(1),
        name="memkv",
    )(mem, g, w)


def _memrouter_kernel(q_ref, k_ref, v_ref, h1_ref, wo_ref, gffn_ref, wrh_ref, wrl_ref, br_ref,
                      h2_ref, ap_ref, meta_ref, cnt_ref, carry_ref, *, ts):
    first = jnp.logical_and(pl.program_id(0) == 0, pl.program_id(1) == 0)

    @pl.when(first)
    def _():
        carry_ref[...] = jnp.zeros_like(carry_ref)

    q = q_ref[0]
    heads = []
    for h in range(MEM_HEADS):
        sl = slice(h * MEM_HEAD_DIM, (h + 1) * MEM_HEAD_DIM)
        sc = _dot_nt(q[:, sl], k_ref[0, :, sl])
        m = jnp.max(sc, axis=-1, keepdims=True)
        pm = jnp.exp2(sc - m)
        l = jnp.sum(pm, axis=-1, keepdims=True)
        heads.append((_dot(pm.astype(BF16), v_ref[0, :, sl]) / l).astype(BF16))
    h2 = h1_ref[0] + _dot(jnp.concatenate(heads, axis=1), wo_ref[...])
    h2_ref[0] = h2

    a = _rms(h2, gffn_ref[...])
    a_hi = a.astype(BF16)
    a_r = a_hi.astype(F32)
    half = a.shape[1] // 2
    bits = lax.bitcast_convert_type(a_r, U32)
    ap_ref[0] = (bits[:, :half] >> 16) | bits[:, half:]

    a_lo = (a - a_r).astype(BF16)
    logits = _dot(a_hi, wrh_ref[...]) + _dot(a_lo, wrh_ref[...]) + _dot(a_hi, wrl_ref[...]) + br_ref[...]
    lane = lax.broadcasted_iota(I32, (ts, LANES), 1)
    work = jnp.where(lane < N_EXPERTS, logits, NEG)
    vals, idxs, sels = [], [], []
    for _ in range(TOP_K):
        mk = jnp.max(work, axis=-1, keepdims=True)
        ik = jnp.min(jnp.where(work == mk, lane, LANES), axis=-1, keepdims=True)
        sel = lane == ik
        work = jnp.where(sel, NEG, work)
        vals.append(mk)
        idxs.append(ik)
        sels.append(sel)
    exps = [jnp.exp(vk - vals[0]) for vk in vals]
    denom = exps[0] + exps[1] + exps[2] + exps[3]
    chosen = jnp.logical_or(jnp.logical_or(sels[0], sels[1]), jnp.logical_or(sels[2], sels[3]))
    onehot = jnp.where(chosen, 1.0, 0.0)
    row = lax.broadcasted_iota(I32, (ts, ts), 0)
    col = lax.broadcasted_iota(I32, (ts, ts), 1)
    tri = jnp.where(row > col, 1.0, 0.0).astype(BF16)
    carry = carry_ref[...]
    rank = _dot(tri, onehot.astype(BF16)) + carry
    carry_new = carry + jnp.sum(onehot, axis=0, keepdims=True)
    carry_ref[...] = carry_new
    cnt_ref[...] = carry_new
    meta = jnp.zeros((ts, LANES), F32)
    for kk in range(TOP_K):
        rk = jnp.sum(jnp.where(sels[kk], rank, 0.0), axis=-1, keepdims=True)
        meta = jnp.where(lane == kk, idxs[kk].astype(F32), meta)
        meta = jnp.where(lane == TOP_K + kk, rk, meta)
        meta = jnp.where(lane == 2 * TOP_K + kk, exps[kk] / denom, meta)
    meta_ref[0] = meta


def _memrouter(qmem, kmem, vmem, h1, w_mo, g_ffn, wr_hi, wr_lo, b_r):
    b, s, d = h1.shape
    m = kmem.shape[1]
    ts = min(TS_MEM, s)

    def full(arr):
        return pl.BlockSpec(arr.shape, lambda bi, si: (0,) * arr.ndim)

    def tok(width):
        return pl.BlockSpec((1, ts, width), lambda bi, si: (bi, si, 0))

    kv = pl.BlockSpec((1, m, MEM_WIDTH), lambda bi, si: (bi, 0, 0))
    cnt = pl.BlockSpec((1, LANES), lambda bi, si: (0, 0))
    return pl.pallas_call(
        functools.partial(_memrouter_kernel, ts=ts),
        grid=(b, s // ts),
        in_specs=[tok(MEM_WIDTH), kv, kv, tok(d), full(w_mo), full(g_ffn), full(wr_hi), full(wr_lo), full(b_r)],
        out_specs=(tok(d), tok(d // 2), tok(LANES), cnt),
        out_shape=(jax.ShapeDtypeStruct((b, s, d), F32), jax.ShapeDtypeStruct((b, s, d // 2), U32),
                   jax.ShapeDtypeStruct((b, s, LANES), F32), jax.ShapeDtypeStruct((1, LANES), F32)),
        scratch_shapes=[pltpu.VMEM((1, LANES), F32)],
        compiler_params=_cparams(2),
        name="memrouter",
    )(qmem, kmem, vmem, h1, w_mo, g_ffn, wr_hi, wr_lo, b_r)


def _row_copy(src_ref, src_row, dst_ref, dst_row, sem):
    return pltpu.make_async_copy(src_ref.at[pl.ds(src_row, 1), :], dst_ref.at[pl.ds(dst_row, 1), :], sem)


def _dispatch_kernel(off_ref, ek_ref, rk_ref, ap_ref, xs_in_ref, xs_ref, sem, *, td):
    del xs_in_ref

    def issue(r, c):
        for kk in range(TOP_K):
            pos = off_ref[ek_ref[r * TOP_K + kk]] + rk_ref[r * TOP_K + kk]
            _row_copy(ap_ref, r, xs_ref, pos, sem).start()
        return c

    lax.fori_loop(0, td, issue, 0)

    def drain(r, c):
        _row_copy(ap_ref, 0, xs_ref, 0, sem).wait()
        return c

    lax.fori_loop(0, td * TOP_K, drain, 0)


def _dispatch(off, ek, rk, ap, xs_init):
    n, w = ap.shape
    td = min(TD_ROWS, n)
    smem = pl.BlockSpec((td * TOP_K,), lambda i, off_ref: (i,), memory_space=pltpu.SMEM)
    grid_spec = pltpu.PrefetchScalarGridSpec(
        num_scalar_prefetch=1,
        grid=(n // td,),
        in_specs=[smem, smem, pl.BlockSpec((td, w), lambda i, off_ref: (i, 0)),
                  pl.BlockSpec(memory_space=pl.ANY)],
        out_specs=pl.BlockSpec(memory_space=pl.ANY),
        scratch_shapes=[pltpu.SemaphoreType.DMA],
    )
    return pl.pallas_call(
        functools.partial(_dispatch_kernel, td=td),
        grid_spec=grid_spec,
        out_shape=jax.ShapeDtypeStruct(xs_init.shape, xs_init.dtype),
        input_output_aliases={4: 0},
        compiler_params=_cparams(1),
        name="dispatch",
    )(off, ek, rk, ap, xs_init)


def _moe_kernel(te_ref, nv_ref, xs_ref, wgu_ref, bgu_ref, wd_ref, bd_ref, y_ref, *, ch):
    del te_ref
    i = pl.program_id(0)
    d_exp = wd_ref.shape[1]

    @pl.when(i < nv_ref[0])
    def _():
        xp = xs_ref[...]
        half = xp.shape[1]
        x_lo = lax.bitcast_convert_type(xp << 16, F32).astype(BF16)
        x_hi = lax.bitcast_convert_type(xp & jnp.uint32(0xFFFF0000), F32).astype(BF16)
        acc = jnp.zeros(y_ref.shape, F32)
        for c in range(d_exp // ch):
            def gu(lo):
                return (_dot(x_lo, wgu_ref[0, 0:half, lo:lo + ch]) + _dot(x_hi, wgu_ref[0, half:2 * half, lo:lo + ch])
                        + bgu_ref[0, :, lo:lo + ch])

            gate = jnp.minimum(gu(c * ch), SWIGLU_LIMIT)
            up = jnp.clip(gu(d_exp + c * ch), -SWIGLU_LIMIT, SWIGLU_LIMIT)
            act = gate * (1.0 / (1.0 + jnp.exp(-SWIGLU_ALPHA * gate))) * (up + 1.0)
            acc = acc + _dot(act.astype(BF16), wd_ref[0, c * ch:(c + 1) * ch, :])
        y_ref[...] = acc + bd_ref[0]

    @pl.when(i >= nv_ref[0])
    def _():
        y_ref[...] = jnp.zeros_like(y_ref)


def _moe(te, nv, xs, wgu, bgu, wd, bd):
    npad, half = xs.shape
    tm = TM_MOE
    d = wd.shape[2]
    d_exp = wd.shape[1]
    grid_spec = pltpu.PrefetchScalarGridSpec(
        num_scalar_prefetch=2,
        grid=(npad // tm,),
        in_specs=[
            pl.BlockSpec((tm, half), lambda i, te, nv: (i, 0)),
            pl.BlockSpec((1, 2 * half, 2 * d_exp), lambda i, te, nv: (te[i], 0, 0)),
            pl.BlockSpec((1, 1, 2 * d_exp), lambda i, te, nv: (te[i], 0, 0)),
            pl.BlockSpec((1, d_exp, d), lambda i, te, nv: (te[i], 0, 0)),
            pl.BlockSpec((1, 1, d), lambda i, te, nv: (te[i], 0, 0)),
        ],
        out_specs=pl.BlockSpec((tm, d), lambda i, te, nv: (i, 0)),
    )
    return pl.pallas_call(
        functools.partial(_moe_kernel, ch=min(MOE_CH, d_exp)),
        grid_spec=grid_spec,
        out_shape=jax.ShapeDtypeStruct((npad, d), F32),
        compiler_params=_cparams(1),
        name="moe",
    )(te, nv, xs, wgu, bgu, wd, bd)


def _combine_kernel(off_ref, ek_ref, rk_ref, h2_ref, meta_ref, y_ref, gf_ref, o_ref, ybuf, sem, *, td, final_norm):
    def issue(r, c):
        for kk in range(TOP_K):
            pos = off_ref[ek_ref[r * TOP_K + kk]] + rk_ref[r * TOP_K + kk]
            _row_copy(y_ref, pos, ybuf.at[kk], r, sem).start()
        return c

    lax.fori_loop(0, td, issue, 0)

    def drain(r, c):
        _row_copy(y_ref, 0, ybuf.at[0], 0, sem).wait()
        return c

    lax.fori_loop(0, td * TOP_K, drain, 0)

    meta = meta_ref[...]
    acc = h2_ref[...]
    for kk in range(TOP_K):
        acc = acc + meta[:, 2 * TOP_K + kk:2 * TOP_K + kk + 1] * ybuf[kk]
    o_ref[...] = _rms(acc, gf_ref[...]) if final_norm else acc


def _combine(off, ek, rk, h2, meta, y, g_final, final_norm):
    n, d = h2.shape
    td = min(TD_ROWS, n)
    smem = pl.BlockSpec((td * TOP_K,), lambda i, off_ref: (i,), memory_space=pltpu.SMEM)
    grid_spec = pltpu.PrefetchScalarGridSpec(
        num_scalar_prefetch=1,
        grid=(n // td,),
        in_specs=[smem, smem, pl.BlockSpec((td, d), lambda i, off_ref: (i, 0)),
                  pl.BlockSpec((td, LANES), lambda i, off_ref: (i, 0)),
                  pl.BlockSpec(memory_space=pl.ANY),
                  pl.BlockSpec(g_final.shape, lambda i, off_ref: (0, 0))],
        out_specs=pl.BlockSpec((td, d), lambda i, off_ref: (i, 0)),
        scratch_shapes=[pltpu.VMEM((TOP_K, td, d), F32), pltpu.SemaphoreType.DMA],
    )
    return pl.pallas_call(
        functools.partial(_combine_kernel, td=td, final_norm=final_norm),
        grid_spec=grid_spec,
        out_shape=jax.ShapeDtypeStruct((n, d), F32),
        compiler_params=_cparams(1),
        name="combine",
    )(off, ek, rk, h2, meta, y, g_final)


def _rope_tables(seq_len):
    inv = 1.0 / (ROPE_BASE ** (jnp.arange(0, MLA_ROPE_DIM, 2, dtype=F32) / MLA_ROPE_DIM))
    ang = jnp.arange(seq_len, dtype=F32)[:, None] * inv[None, :]
    cos, sin = jnp.cos(ang), jnp.sin(ang)
    ones = jnp.ones((seq_len, MLA_NOPE_DIM), F32)
    zeros_n = jnp.zeros((seq_len, MLA_NOPE_DIM), F32)
    pad = jnp.zeros((seq_len, LANES - MLA_NOPE_DIM - MLA_ROPE_DIM), F32)
    cos_t = jnp.concatenate([ones, cos, cos, pad], axis=1)
    sin_t = jnp.concatenate([zeros_n, sin, sin, pad], axis=1)
    return cos_t, sin_t


def _rot_cols(w):
    half = MLA_ROPE_DIM // 2
    return jnp.concatenate([-w[:, half:], w[:, :half]], axis=1)


def _head_block(nope, rope):
    pad = jnp.zeros((nope.shape[0], LANES - MLA_NOPE_DIM - MLA_ROPE_DIM), F32)
    return jnp.concatenate([nope, rope, pad], axis=1)


def _prep_inproj_weights(w_in, w_uq, w_ukv):
    d = w_in.shape[0]
    pts = np.cumsum((FOX_WIDTH, FOX_WIDTH, FOX_WIDTH, FOX_HEADS, MLA_Q_LORA, MLA_KV_LORA, MLA_ROPE_DIM))
    w_fq, w_fk, w_fv = w_in[:, :pts[0]], w_in[:, pts[0]:pts[1]], w_in[:, pts[1]:pts[2]]
    w_fl, w_ql = w_in[:, pts[2]:pts[3]], w_in[:, pts[3]:pts[4]]
    w_kvl, w_kr = w_in[:, pts[4]:pts[5]], w_in[:, pts[5]:pts[6]]
    zeros_n = jnp.zeros((d, MLA_NOPE_DIM), F32)
    misc = jnp.concatenate([w_fl, jnp.zeros((d, LANES - FOX_HEADS), F32)], axis=1)
    w1 = jnp.concatenate([w_fq, w_fk, w_fv, w_ql, w_kvl, misc, _head_block(zeros_n, w_kr),
                          _head_block(zeros_n, _rot_cols(w_kr))], axis=1).astype(BF16)
    qd = MLA_NOPE_DIM + MLA_ROPE_DIM
    zq = jnp.zeros((w_uq.shape[0], MLA_NOPE_DIM), F32)
    wqa, wqb, wk, wv = [], [], [], []
    for h in range(MLA_HEADS):
        nope = w_uq[:, h * qd:h * qd + MLA_NOPE_DIM]
        rope = w_uq[:, h * qd + MLA_NOPE_DIM:(h + 1) * qd]
        wqa.append(_head_block(nope, rope))
        wqb.append(_head_block(zq, _rot_cols(rope)))
        kvd = MLA_NOPE_DIM + MLA_V_DIM
        wk.append(jnp.concatenate([w_ukv[:, h * kvd:h * kvd + MLA_NOPE_DIM],
                                   jnp.zeros((w_ukv.shape[0], LANES - MLA_NOPE_DIM), F32)], axis=1))
        wv.append(w_ukv[:, h * kvd + MLA_NOPE_DIM:(h + 1) * kvd])
    cat = lambda xs: jnp.concatenate(xs, axis=1).astype(BF16)
    return w1, cat(wqa), cat(wqb), cat(wk), cat(wv)


def _routing_tables(counts, n_tiles, tm):
    tiles_e = (counts + tm - 1) // tm
    tile_end = jnp.cumsum(tiles_e)
    off = ((tile_end - tiles_e) * tm).astype(I32)
    total = tile_end[-1]
    ti = jnp.arange(n_tiles, dtype=I32)
    te = jnp.minimum(jnp.sum(ti[:, None] >= tile_end[None, :], axis=1), N_EXPERTS - 1).astype(I32)
    valid = ti < total
    last_e = jnp.sum(jnp.where(ti == total - 1, te, 0))
    te = jnp.where(valid, te, last_e).astype(I32)
    return off, te, total.astype(I32).reshape(1)


def kernel(x, mem, g_mix, w_in, b_f, g_q_lat, w_uq, g_kv_lat, w_ukv, g_fox_out, g_mla_out, w_o, g_mem_q, w_mem_q,
           g_mem_kv, w_mem_kv, w_mem_o, g_ffn, w_router, b_router, w_gate_up, b_gate_up, w_down, b_down, g_final):
    b, s, d = x.shape
    n = b * s
    depth = g_mix.shape[0]
    cos_t, sin_t = _rope_tables(s)
    row = lambda v: v.reshape(1, -1)
    h = x
    for l in range(depth):
        w1, wqa, wqb, wk, wv = _prep_inproj_weights(w_in[l], w_uq[l], w_ukv[l])
        fq, fk, fv, flog, qm, km, vm = _inproj(h, row(g_mix[l]), w1, row(g_q_lat[l]), wqa, wqb, row(g_kv_lat[l]),
                                                wk, wv, cos_t, sin_t)
        cq, ck = _decay(flog, b_f[l])
        o_fox = _attention(True, fq, fk, fv, cq, ck)
        o_mla = _attention(False, qm, km, vm)
        h1, qmem = _mixout(o_fox.reshape(n, -1), o_mla.reshape(n, -1), h.reshape(n, d), row(g_fox_out[l]),
                           row(g_mla_out[l]), w_o[l].astype(BF16), row(g_mem_q[l]), w_mem_q[l].astype(BF16))
        kmem, vmem = _memkv(mem, row(g_mem_kv[l]), w_mem_kv[l].astype(BF16))
        wr = jnp.zeros((d, LANES), F32).at[:, :N_EXPERTS].set(w_router[l])
        wr_hi = wr.astype(BF16)
        wr_lo = (wr - wr_hi.astype(F32)).astype(BF16)
        br = jnp.zeros((1, LANES), F32).at[0, :N_EXPERTS].set(b_router[l])
        h2, ap, meta, cnt = _memrouter(qmem.reshape(b, s, -1), kmem, vmem, h1.reshape(b, s, d),
                                       w_mem_o[l].astype(BF16), row(g_ffn[l]), wr_hi, wr_lo, br)
        meta2 = meta.reshape(n, LANES)
        ek = meta2[:, 0:TOP_K].astype(I32).reshape(-1)
        rk = meta2[:, TOP_K:2 * TOP_K].astype(I32).reshape(-1)
        n_tiles = n * TOP_K // TM_MOE + N_EXPERTS
        off, te, nv = _routing_tables(cnt[0, :N_EXPERTS].astype(I32), n_tiles, TM_MOE)
        xs_init = jnp.zeros((n_tiles * TM_MOE, d // 2), U32)
        xs = _dispatch(off, ek, rk, ap.reshape(n, d // 2), xs_init)
        y = _moe(te, nv, xs, w_gate_up[l].astype(BF16), b_gate_up[l].reshape(N_EXPERTS, 1, -1),
                 w_down[l].astype(BF16), b_down[l].reshape(N_EXPERTS, 1, -1))
        h = _combine(off, ek, rk, h2.reshape(n, d), meta2, y, row(g_final), l == depth - 1).reshape(b, s, d)
    return h
```

```python
import functools

import numpy as np
import jax
import jax.numpy as jnp
from jax import lax
from jax.experimental import pallas as pl
from jax.experimental.pallas import tpu as pltpu

F32 = jnp.float32
BF16 = jnp.bfloat16
I32 = jnp.int32

LANES = 128
SUBLANES = 8
RMS_EPS = 1e-6
CHUNK = 64
FOX_HEADS = 8
FOX_HEAD_DIM = 64
FOX_WIDTH = FOX_HEADS * FOX_HEAD_DIM
MLA_HEADS = 8
MLA_Q_LORA = 384
MLA_KV_LORA = 256
MLA_NOPE_DIM = 64
MLA_ROPE_DIM = 32
MLA_V_DIM = 64
MLA_WIDTH = MLA_HEADS * MLA_V_DIM
ROPE_BASE = 10000.0
MEM_HEADS = 4
MEM_HEAD_DIM = 128
MEM_WIDTH = MEM_HEADS * MEM_HEAD_DIM
N_EXPERTS = 32
TOP_K = 4
SWIGLU_LIMIT = 7.0
SWIGLU_ALPHA = 1.702

LOG2E = 1.4426950408889634
NEG = -1e30
FOX_QSCALE = FOX_HEAD_DIM ** -0.5 * LOG2E
MLA_QSCALE = (MLA_NOPE_DIM + MLA_ROPE_DIM) ** -0.5 * LOG2E
MEM_QSCALE = MEM_HEAD_DIM ** -0.5 * LOG2E

VMEM_LIMIT = 56 * 1024 * 1024

TS_PROJ = 512
TS_MIX = 512
TS_MEM = 512
T_ATT = 256
DECAY_BLK = 256
TM_MOE = 256
TD_ROWS = 256
MOE_CH = 512


def _cparams(n_axes):
    return pltpu.CompilerParams(dimension_semantics=("arbitrary",) * n_axes, vmem_limit_bytes=VMEM_LIMIT)


def _dot(a, b):
    return jnp.dot(a, b, preferred_element_type=F32)


def _dot_nt(a, b):
    return lax.dot_general(a, b, (((1,), (1,)), ((), ())), preferred_element_type=F32)


def _rms(x, g):
    return x * lax.rsqrt(jnp.mean(x * x, axis=-1, keepdims=True) + RMS_EPS) * g


def _split3(x):
    hi = x.astype(BF16)
    r1 = x - hi.astype(F32)
    mid = r1.astype(BF16)
    lo = (r1 - mid.astype(F32)).astype(BF16)
    return hi, mid, lo


def _inproj_kernel(x_ref, g_ref, w1_ref, gq_ref, wqa_ref, wqb_ref, gkv_ref, wk_ref, wv_ref, cos_ref, sin_ref,
                   fq_ref, fk_ref, fv_ref, fl_ref, qm_ref, km_ref, vm_ref):
    a = _rms(x_ref[0], g_ref[...]).astype(BF16)

    def proj(lo, hi):
        return _dot(a, w1_ref[:, lo:hi])

    c0 = 3 * FOX_WIDTH
    c1 = c0 + MLA_Q_LORA
    c2 = c1 + MLA_KV_LORA
    fq_ref[0] = (proj(0, FOX_WIDTH) * FOX_QSCALE).astype(BF16)
    fk_ref[0] = proj(FOX_WIDTH, 2 * FOX_WIDTH).astype(BF16)
    fv_ref[0] = proj(2 * FOX_WIDTH, c0).astype(BF16)
    qn = _rms(proj(c0, c1), gq_ref[...]).astype(BF16)
    kvn = _rms(proj(c1, c2), gkv_ref[...]).astype(BF16)
    fl_ref[0] = proj(c2, c2 + LANES)
    cos = cos_ref[...]
    sin = sin_ref[...]
    kpe = proj(c2 + LANES, c2 + 2 * LANES) * cos + proj(c2 + 2 * LANES, c2 + 3 * LANES) * sin
    cos8 = jnp.concatenate([cos] * MLA_HEADS, axis=1)
    sin8 = jnp.concatenate([sin] * MLA_HEADS, axis=1)
    qm = (_dot(qn, wqa_ref[...]) * cos8 + _dot(qn, wqb_ref[...]) * sin8) * MLA_QSCALE
    qm_ref[0] = qm.astype(BF16)
    km_ref[0] = (_dot(kvn, wk_ref[...]) + jnp.concatenate([kpe] * MLA_HEADS, axis=1)).astype(BF16)
    vm_ref[0] = _dot(kvn, wv_ref[...]).astype(BF16)


def _inproj(x, g_mix, w1, g_q, wqa, wqb, g_kv, wk, wv, cos_t, sin_t):
    b, s, d = x.shape
    ts = min(TS_PROJ, s)
    n1 = w1.shape[1]

    def full(arr):
        return pl.BlockSpec(arr.shape, lambda bi, si: (0,) * arr.ndim)

    def tok(width):
        return pl.BlockSpec((1, ts, width), lambda bi, si: (bi, si, 0))

    tab = pl.BlockSpec((ts, LANES), lambda bi, si: (si, 0))
    out_shapes = (
        jax.ShapeDtypeStruct((b, s, FOX_WIDTH), BF16),
        jax.ShapeDtypeStruct((b, s, FOX_WIDTH), BF16),
        jax.ShapeDtypeStruct((b, s, FOX_WIDTH), BF16),
        jax.ShapeDtypeStruct((b, s, LANES), F32),
        jax.ShapeDtypeStruct((b, s, MLA_HEADS * LANES), BF16),
        jax.ShapeDtypeStruct((b, s, MLA_HEADS * LANES), BF16),
        jax.ShapeDtypeStruct((b, s, MLA_WIDTH), BF16),
    )
    return pl.pallas_call(
        _inproj_kernel,
        grid=(b, s // ts),
        in_specs=[tok(d), full(g_mix), full(w1), full(g_q), full(wqa), full(wqb), full(g_kv), full(wk), full(wv),
                  tab, tab],
        out_specs=(tok(FOX_WIDTH), tok(FOX_WIDTH), tok(FOX_WIDTH), tok(LANES), tok(MLA_HEADS * LANES),
                   tok(MLA_HEADS * LANES), tok(MLA_WIDTH)),
        out_shape=out_shapes,
        compiler_params=_cparams(2),
        name="inproj",
    )(x, g_mix, w1, g_q, wqa, wqb, g_kv, wk, wv, cos_t, sin_t)


def _decay_kernel(fl_ref, bf_ref, pq_ref, pk_ref, oq_ref, ok_ref, cq_ref, ck_ref, *, blk):
    s = fl_ref.shape[1]
    row = lax.broadcasted_iota(I32, (blk, blk), 0)
    col = lax.broadcasted_iota(I32, (blk, blk), 1)
    tri = jnp.where(row >= col, 1.0, 0.0).astype(BF16)
    carry = jnp.zeros((1, LANES), F32)
    for i in range(s // blk):
        sl = slice(i * blk, (i + 1) * blk)
        z = fl_ref[0, sl, :] + bf_ref[...]
        lf = (jnp.minimum(z, 0.0) - jnp.log1p(jnp.exp(-jnp.abs(z)))) * LOG2E
        h, m, l = _split3(lf)
        cs = _dot(tri, h) + _dot(tri, m) + _dot(tri, l) + carry
        carry = cs[blk - 1:blk, :]
        h, m, l = _split3(cs)
        cq = _dot(h, pq_ref[0]) + _dot(m, pq_ref[1]) + _dot(l, pq_ref[2]) + oq_ref[...]
        ck = _dot(h, pk_ref[0]) + _dot(m, pk_ref[1]) + _dot(l, pk_ref[2]) + ok_ref[...]
        cq_ref[0, sl, :] = cq.astype(BF16)
        ck_ref[0, sl, :] = ck.astype(BF16)


def _decay_tables():
    pq = np.zeros((3, LANES, LANES), np.float32)
    pk = np.zeros((3, LANES, LANES), np.float32)
    oq = np.zeros((1, LANES), np.float32)
    ok = np.zeros((1, LANES), np.float32)
    for h in range(FOX_HEADS):
        for part in range(3):
            pq[part, h, 16 * h + part] = 1.0
            pk[part, h, 16 * h + 3 + part] = -1.0
            oq[0, 16 * h + 3 + part] = 1.0
            ok[0, 16 * h + part] = 1.0
    return jnp.asarray(pq, BF16), jnp.asarray(pk, BF16), jnp.asarray(oq), jnp.asarray(ok)


def _decay(flog, b_f):
    b, s, _ = flog.shape
    blk = min(DECAY_BLK, s)
    pq, pk, oq, ok = _decay_tables()
    bf = jnp.zeros((1, LANES), F32).at[0, :FOX_HEADS].set(b_f)

    def full(arr):
        return pl.BlockSpec(arr.shape, lambda bi: (0,) * arr.ndim)

    seq = pl.BlockSpec((1, s, LANES), lambda bi: (bi, 0, 0))
    return pl.pallas_call(
        functools.partial(_decay_kernel, blk=blk),
        grid=(b,),
        in_specs=[seq, full(bf), full(pq), full(pk), full(oq), full(ok)],
        out_specs=(seq, seq),
        out_shape=(jax.ShapeDtypeStruct((b, s, LANES), BF16), jax.ShapeDtypeStruct((b, s, LANES), BF16)),
        compiler_params=_cparams(1),
        name="decay",
    )(flog, bf, pq, pk, oq, ok)


def _attn_pair_kernel(*refs, fox, t):
    if fox:
        q_ref, cq_ref, k_ref, ck_ref, v_ref, o_ref = refs
    else:
        q_ref, k_ref, v_ref, o_ref = refs
    p = pl.program_id(1)
    i = pl.program_id(2)
    lane = lax.broadcasted_iota(I32, (1, LANES), 1)

    def variant(iv):
        row = lax.broadcasted_iota(I32, (t, t), 0)
        col = lax.broadcasted_iota(I32, (t, t), 1)
        allowed = (col <= row) if fox else ((col // CHUNK) <= (row // CHUNK))
        past = iv * t
        outs = []
        for hh in range(2):
            if fox:
                q2 = q_ref[0]
                c2 = cq_ref[0]
                zero = jnp.zeros_like(q2)
                qa = jnp.concatenate([jnp.where((lane // FOX_HEAD_DIM) == hh, q2, zero),
                                      jnp.where((lane // 16) == 2 * p + hh, c2, zero)], axis=1)
            else:
                qa = q_ref[0, :, hh * LANES:(hh + 1) * LANES]

            def keys(lo, hi, hh=hh):
                if fox:
                    return jnp.concatenate([k_ref[0, lo:hi, :], ck_ref[0, lo:hi, :]], axis=1)
                return k_ref[0, lo:hi, hh * LANES:(hh + 1) * LANES]

            s_diag = jnp.where(allowed, _dot_nt(qa, keys(past, past + t)), NEG)
            m = jnp.max(s_diag, axis=-1, keepdims=True)
            if iv > 0:
                s_past = _dot_nt(qa, keys(0, past))
                m = jnp.maximum(m, jnp.max(s_past, axis=-1, keepdims=True))
                p_past = jnp.exp2(s_past - m)
                l = jnp.sum(p_past, axis=-1, keepdims=True)
                acc = _dot(p_past.astype(BF16), v_ref[0, 0:past, :])
            p_diag = jnp.exp2(s_diag - m)
            l_diag = jnp.sum(p_diag, axis=-1, keepdims=True)
            acc_diag = _dot(p_diag.astype(BF16), v_ref[0, past:past + t, :])
            if iv > 0:
                l = l + l_diag
                acc = acc + acc_diag
            else:
                l, acc = l_diag, acc_diag
            outs.append(acc * (1.0 / l))
        o_ref[0] = jnp.where(lane < MLA_V_DIM, outs[0], outs[1]).astype(o_ref.dtype)

    for iv in range(k_ref.shape[1] // t):
        pl.when(i == iv)(functools.partial(variant, iv))


def _attention(fox, q, k, v, cq=None, ck=None):
    b, s, _ = v.shape
    t = min(T_ATT, s)
    pairs = v.shape[2] // LANES
    qw = q.shape[2] // pairs
    kw = k.shape[2] // pairs
    qspec = pl.BlockSpec((1, t, qw), lambda bi, pi, qi: (bi, qi, pi))
    kspec = pl.BlockSpec((1, s, kw), lambda bi, pi, qi: (bi, 0, pi))
    vspec = pl.BlockSpec((1, s, LANES), lambda bi, pi, qi: (bi, 0, pi))
    ospec = pl.BlockSpec((1, t, LANES), lambda bi, pi, qi: (bi, qi, pi))
    if fox:
        cqspec = pl.BlockSpec((1, t, LANES), lambda bi, pi, qi: (bi, qi, 0))
        ckspec = pl.BlockSpec((1, s, LANES), lambda bi, pi, qi: (bi, 0, 0))
        in_specs = [qspec, cqspec, kspec, ckspec, vspec]
        args = (q, cq, k, ck, v)
    else:
        in_specs = [qspec, kspec, vspec]
        args = (q, k, v)
    return pl.pallas_call(
        functools.partial(_attn_pair_kernel, fox=fox, t=t),
        grid=(b, pairs, s // t),
        in_specs=in_specs,
        out_specs=ospec,
        out_shape=jax.ShapeDtypeStruct((b, s, pairs * LANES), BF16),
        compiler_params=_cparams(3),
        name="fox_attn" if fox else "mla_attn",
    )(*args)


def _mixout_kernel(of_ref, om_ref, x_ref, gf_ref, gm_ref, wo_ref, gq_ref, wq_ref, h1_ref, qmem_ref):
    nf = _rms(of_ref[...].astype(F32), gf_ref[...]).astype(BF16)
    nm = _rms(om_ref[...].astype(F32), gm_ref[...]).astype(BF16)
    h1 = x_ref[...] + _dot(nf, wo_ref[0:FOX_WIDTH, :]) + _dot(nm, wo_ref[FOX_WIDTH:FOX_WIDTH + MLA_WIDTH, :])
    h1_ref[...] = h1
    a = _rms(h1, gq_ref[...]).astype(BF16)
    qmem_ref[...] = (_dot(a, wq_ref[...]) * MEM_QSCALE).astype(BF16)


def _mixout(o_fox, o_mla, x2, g_fox, g_mla, w_o, g_mq, w_mq):
    n, d = x2.shape
    ts = min(TS_MIX, n)

    def full(arr):
        return pl.BlockSpec(arr.shape, lambda i: (0,) * arr.ndim)

    def tok(width):
        return pl.BlockSpec((ts, width), lambda i: (i, 0))

    return pl.pallas_call(
        _mixout_kernel,
        grid=(n // ts,),
        in_specs=[tok(FOX_WIDTH), tok(MLA_WIDTH), tok(d), full(g_fox), full(g_mla), full(w_o), full(g_mq),
                  full(w_mq)],
        out_specs=(tok(d), tok(MEM_WIDTH)),
        out_shape=(jax.ShapeDtypeStruct((n, d), F32), jax.ShapeDtypeStruct((n, MEM_WIDTH), BF16)),
        compiler_params=_cparams(1),
        name="mixout",
    )(o_fox, o_mla, x2, g_fox, g_mla, w_o, g_mq, w_mq)


def _memkv_kernel(mem_ref, g_ref, w_ref, k_ref, v_ref):
    a = _rms(mem_ref[0], g_ref[...]).astype(BF16)
    k_ref[0] = _dot(a, w_ref[:, 0:MEM_WIDTH]).astype(BF16)
    v_ref[0] = _dot(a, w_ref[:, MEM_WIDTH:2 * MEM_WIDTH]).astype(BF16)


def _memkv(mem, g, w):
    b, m, d = mem.shape
    kv = pl.BlockSpec((1, m, MEM_WIDTH), lambda bi: (bi, 0, 0))
    return pl.pallas_call(
        _memkv_kernel,
        grid=(b,),
        in_specs=[pl.BlockSpec((1, m, d), lambda bi: (bi, 0, 0)), pl.BlockSpec(g.shape, lambda bi: (0, 0)),
                  pl.BlockSpec(w.shape, lambda bi: (0, 0))],
        out_specs=(kv, kv),
        out_shape=(jax.ShapeDtypeStruct((b, m, MEM_WIDTH), BF16), jax.ShapeDtypeStruct((b, m, MEM_WIDTH), BF16)),
        compiler_params=_cparams(1),
        name="memkv",
    )(mem, g, w)


def _memrouter_kernel(q_ref, k_ref, v_ref, h1_ref, wo_ref, gffn_ref, wrh_ref, wrl_ref, br_ref,
                      h2_ref, at_ref, meta_ref, cnt_ref, carry_ref, *, ts):
    first = jnp.logical_and(pl.program_id(0) == 0, pl.program_id(1) == 0)

    @pl.when(first)
    def _():
        carry_ref[...] = jnp.zeros_like(carry_ref)

    q = q_ref[0]
    heads = []
    for h in range(MEM_HEADS):
        sl = slice(h * MEM_HEAD_DIM, (h + 1) * MEM_HEAD_DIM)
        sc = _dot_nt(q[:, sl], k_ref[0, :, sl])
        m = jnp.max(sc, axis=-1, keepdims=True)
        pm = jnp.exp2(sc - m)
        l = jnp.sum(pm, axis=-1, keepdims=True)
        heads.append((_dot(pm.astype(BF16), v_ref[0, :, sl]) / l).astype(BF16))
    h2 = h1_ref[0] + _dot(jnp.concatenate(heads, axis=1), wo_ref[...])
    h2_ref[0] = h2

    a = _rms(h2, gffn_ref[...])
    _store_token_tiles(at_ref, a)
    a_hi = a.astype(BF16)
    a_lo = (a - a_hi.astype(F32)).astype(BF16)
    logits = _dot(a_hi, wrh_ref[...]) + _dot(a_lo, wrh_ref[...]) + _dot(a_hi, wrl_ref[...]) + br_ref[...]
    lane = lax.broadcasted_iota(I32, (ts, LANES), 1)
    work = jnp.where(lane < N_EXPERTS, logits, NEG)
    vals, idxs, sels = [], [], []
    for _ in range(TOP_K):
        mk = jnp.max(work, axis=-1, keepdims=True)
        ik = jnp.min(jnp.where(work == mk, lane, LANES), axis=-1, keepdims=True)
        sel = lane == ik
        work = jnp.where(sel, NEG, work)
        vals.append(mk)
        idxs.append(ik)
        sels.append(sel)
    exps = [jnp.exp(vk - vals[0]) for vk in vals]
    denom = exps[0] + exps[1] + exps[2] + exps[3]
    chosen = jnp.logical_or(jnp.logical_or(sels[0], sels[1]), jnp.logical_or(sels[2], sels[3]))
    onehot = jnp.where(chosen, 1.0, 0.0)
    row = lax.broadcasted_iota(I32, (ts, ts), 0)
    col = lax.broadcasted_iota(I32, (ts, ts), 1)
    tri = jnp.where(row > col, 1.0, 0.0).astype(BF16)
    carry = carry_ref[...]
    rank = _dot(tri, onehot.astype(BF16)) + carry
    carry_new = carry + jnp.sum(onehot, axis=0, keepdims=True)
    carry_ref[...] = carry_new
    cnt_ref[...] = carry_new
    meta = jnp.zeros((ts, LANES), F32)
    for kk in range(TOP_K):
        rk = jnp.sum(jnp.where(sels[kk], rank, 0.0), axis=-1, keepdims=True)
        meta = jnp.where(lane == kk, idxs[kk].astype(F32), meta)
        meta = jnp.where(lane == TOP_K + kk, rk, meta)
        meta = jnp.where(lane == 2 * TOP_K + kk, exps[kk] / denom, meta)
    meta_ref[0] = meta


def _memrouter(qmem, kmem, vmem, h1, w_mo, g_ffn, wr_hi, wr_lo, b_r):
    b, s, d = h1.shape
    m = kmem.shape[1]
    ts = min(TS_MEM, s)

    def full(arr):
        return pl.BlockSpec(arr.shape, lambda bi, si: (0,) * arr.ndim)

    def tok(width):
        return pl.BlockSpec((1, ts, width), lambda bi, si: (bi, si, 0))

    kv = pl.BlockSpec((1, m, MEM_WIDTH), lambda bi, si: (bi, 0, 0))
    cnt = pl.BlockSpec((1, LANES), lambda bi, si: (0, 0))
    return pl.pallas_call(
        functools.partial(_memrouter_kernel, ts=ts),
        grid=(b, s // ts),
        in_specs=[tok(MEM_WIDTH), kv, kv, tok(d), full(w_mo), full(g_ffn), full(wr_hi), full(wr_lo), full(b_r)],
        out_specs=(tok(d), pl.BlockSpec((ts * SUBLANES, LANES), lambda bi, si: (bi * (s // ts) + si, 0)),
                   tok(LANES), cnt),
        out_shape=(jax.ShapeDtypeStruct((b, s, d), F32), jax.ShapeDtypeStruct((b * s * SUBLANES, LANES), F32),
                   jax.ShapeDtypeStruct((b, s, LANES), F32), jax.ShapeDtypeStruct((1, LANES), F32)),
        scratch_shapes=[pltpu.VMEM((1, LANES), F32)],
        compiler_params=_cparams(2),
        name="memrouter",
    )(qmem, kmem, vmem, h1, w_mo, g_ffn, wr_hi, wr_lo, b_r)


def _store_token_tiles(ref, val):
    rows = val.shape[0]
    for j in range(SUBLANES):
        ref[pl.ds(j, rows, stride=SUBLANES), :] = val[:, j * LANES:(j + 1) * LANES]


def _load_token_tiles(ref, rows):
    return jnp.concatenate([ref[pl.ds(j, rows, stride=SUBLANES), :] for j in range(SUBLANES)], axis=1)


def _tile_copy(src_ref, src_row8, dst_ref, dst_row8, sem):
    return pltpu.make_async_copy(src_ref.at[pl.ds(pl.multiple_of(src_row8, SUBLANES), SUBLANES), :],
                                 dst_ref.at[pl.ds(pl.multiple_of(dst_row8, SUBLANES), SUBLANES), :], sem)


def _dispatch_kernel(last_ref, cnt_ref, nv_ref, pos_ref, at_ref, xs_ref, zbuf, zsem, sem, *, td, tm, min_used,
                     n_tiles):
    @pl.when(pl.program_id(0) == 0)
    def _():
        zbuf[...] = jnp.zeros_like(zbuf)

        def fill(row):
            return pltpu.make_async_copy(
                zbuf, xs_ref.at[pl.ds(pl.multiple_of(row * SUBLANES, SUBLANES), tm * SUBLANES), :], zsem)

        fills = [(cnt_ref[e] > 0, last_ref[e]) for e in range(N_EXPERTS)]
        fills += [(t >= nv_ref[0], t * tm) for t in range(min_used, n_tiles)]
        for pred, row in fills:
            pl.when(pred)(lambda row=row: fill(row).start())
        for pred, row in fills:
            pl.when(pred)(lambda row=row: fill(row).wait())

    def issue(r, c):
        for kk in range(TOP_K):
            _tile_copy(at_ref, r * SUBLANES, xs_ref, pos_ref[r * TOP_K + kk], sem).start()
        return c

    lax.fori_loop(0, td, issue, 0, unroll=4)
    for _ in range(TOP_K):
        pltpu.make_async_copy(at_ref, xs_ref.at[pl.ds(0, td * SUBLANES), :], sem).wait()


def _dispatch(last_tile_row, cnt, nv, pos8, at, n_tiles, tm):
    n = at.shape[0] // SUBLANES
    td = min(TD_ROWS, n)
    grid_spec = pltpu.PrefetchScalarGridSpec(
        num_scalar_prefetch=3,
        grid=(n // td,),
        in_specs=[pl.BlockSpec((td * TOP_K,), lambda i, *_: (i,), memory_space=pltpu.SMEM),
                  pl.BlockSpec((td * SUBLANES, LANES), lambda i, *_: (i, 0))],
        out_specs=pl.BlockSpec(memory_space=pl.ANY),
        scratch_shapes=[pltpu.VMEM((tm * SUBLANES, LANES), F32), pltpu.SemaphoreType.DMA, pltpu.SemaphoreType.DMA],
    )
    min_used = n * TOP_K // tm
    return pl.pallas_call(
        functools.partial(_dispatch_kernel, td=td, tm=tm, min_used=min_used, n_tiles=n_tiles),
        grid_spec=grid_spec,
        out_shape=jax.ShapeDtypeStruct((n_tiles * tm * SUBLANES, LANES), F32),
        compiler_params=_cparams(1),
        name="dispatch",
    )(last_tile_row, cnt, nv, pos8, at)


def _moe_kernel(te_ref, nv_ref, xs_ref, wgu_ref, bgu_ref, wd_ref, bd_ref, y_ref, wgu_bf, wd_bf, *, ch, tm):
    i = pl.program_id(0)
    d_exp = wd_ref.shape[1]
    new_expert = jnp.logical_or(i == 0, te_ref[i] != te_ref[jnp.maximum(i - 1, 0)])

    @pl.when(jnp.logical_and(new_expert, i < nv_ref[0]))
    def _():
        wgu_bf[...] = wgu_ref[0].astype(BF16)
        wd_bf[...] = wd_ref[0].astype(BF16)

    @pl.when(i < nv_ref[0])
    def _():
        x = _load_token_tiles(xs_ref, tm).astype(BF16)
        acc = jnp.zeros((tm, wd_ref.shape[2]), F32)
        for c in range(d_exp // ch):
            def gu(lo):
                return _dot(x, wgu_bf[:, lo:lo + ch]) + bgu_ref[0, :, lo:lo + ch]

            gate = jnp.minimum(gu(c * ch), SWIGLU_LIMIT)
            up = jnp.clip(gu(d_exp + c * ch), -SWIGLU_LIMIT, SWIGLU_LIMIT)
            act = gate * (1.0 / (1.0 + jnp.exp(-SWIGLU_ALPHA * gate))) * (up + 1.0)
            acc = acc + _dot(act.astype(BF16), wd_bf[c * ch:(c + 1) * ch, :])
        _store_token_tiles(y_ref, acc + bd_ref[0])

    @pl.when(i >= nv_ref[0])
    def _():
        y_ref[...] = jnp.zeros_like(y_ref)


def _moe(te, nv, xs, wgu, bgu, wd, bd, tm):
    n_tiles = xs.shape[0] // (tm * SUBLANES)
    _, d, d_exp2 = wgu.shape
    d_exp = wd.shape[1]
    assert d == SUBLANES * LANES and wd.shape[2] == d
    tile = pl.BlockSpec((tm * SUBLANES, LANES), lambda i, te, nv: (i, 0))
    grid_spec = pltpu.PrefetchScalarGridSpec(
        num_scalar_prefetch=2,
        grid=(n_tiles,),
        in_specs=[
            tile,
            pl.BlockSpec((1, d, d_exp2), lambda i, te, nv: (te[i], 0, 0)),
            pl.BlockSpec((1, 1, d_exp2), lambda i, te, nv: (te[i], 0, 0)),
            pl.BlockSpec((1, d_exp, d), lambda i, te, nv: (te[i], 0, 0)),
            pl.BlockSpec((1, 1, d), lambda i, te, nv: (te[i], 0, 0)),
        ],
        out_specs=tile,
        scratch_shapes=[pltpu.VMEM((d, d_exp2), BF16), pltpu.VMEM((d_exp, d), BF16)],
    )
    return pl.pallas_call(
        functools.partial(_moe_kernel, ch=min(MOE_CH, d_exp), tm=tm),
        grid_spec=grid_spec,
        out_shape=jax.ShapeDtypeStruct(xs.shape, F32),
        compiler_params=_cparams(1),
        name="moe",
    )(te, nv, xs, wgu, bgu, wd, bd)


def _combine_kernel(pos_ref, posn_ref, h2_ref, meta_ref, y_ref, gf_ref, o_ref, ybuf, sem, *, td, final_norm):
    i = pl.program_id(0)
    slot = i % 2

    def issue_block(p_ref, sl):
        def issue(r, c):
            for kk in range(TOP_K):
                _tile_copy(y_ref, p_ref[r * TOP_K + kk], ybuf.at[sl, kk], r * SUBLANES, sem.at[sl]).start()
            return c

        lax.fori_loop(0, td, issue, 0, unroll=4)

    @pl.when(i == 0)
    def _():
        issue_block(pos_ref, 0)

    @pl.when(i + 1 < pl.num_programs(0))
    def _():
        issue_block(posn_ref, 1 - slot)

    for kk in range(TOP_K):
        pltpu.make_async_copy(y_ref.at[pl.ds(0, td * SUBLANES), :], ybuf.at[slot, kk], sem.at[slot]).wait()

    meta = meta_ref[...]
    acc = h2_ref[...]
    for kk in range(TOP_K):
        gate = meta[:, 2 * TOP_K + kk:2 * TOP_K + kk + 1]
        acc = acc + gate * _load_token_tiles(ybuf.at[slot, kk], td)
    o_ref[...] = _rms(acc, gf_ref[...]) if final_norm else acc


def _combine(pos8, h2, meta, y, g_final, final_norm):
    n, d = h2.shape
    td = min(TD_ROWS, n)
    n_blocks = n // td
    grid_spec = pltpu.PrefetchScalarGridSpec(
        num_scalar_prefetch=0,
        grid=(n_blocks,),
        in_specs=[pl.BlockSpec((td * TOP_K,), lambda i: (i,), memory_space=pltpu.SMEM),
                  pl.BlockSpec((td * TOP_K,), lambda i: (jnp.minimum(i + 1, n_blocks - 1),), memory_space=pltpu.SMEM),
                  pl.BlockSpec((td, d), lambda i: (i, 0)),
                  pl.BlockSpec((td, LANES), lambda i: (i, 0)),
                  pl.BlockSpec(memory_space=pl.ANY),
                  pl.BlockSpec(g_final.shape, lambda i: (0, 0))],
        out_specs=pl.BlockSpec((td, d), lambda i: (i, 0)),
        scratch_shapes=[pltpu.VMEM((2, TOP_K, td * SUBLANES, LANES), F32), pltpu.SemaphoreType.DMA((2,))],
    )
    return pl.pallas_call(
        functools.partial(_combine_kernel, td=td, final_norm=final_norm),
        grid_spec=grid_spec,
        out_shape=jax.ShapeDtypeStruct((n, d), F32),
        compiler_params=_cparams(1),
        name="combine",
    )(pos8, pos8, h2, meta, y, g_final)


def _rope_tables(seq_len):
    inv = 1.0 / (ROPE_BASE ** (jnp.arange(0, MLA_ROPE_DIM, 2, dtype=F32) / MLA_ROPE_DIM))
    ang = jnp.arange(seq_len, dtype=F32)[:, None] * inv[None, :]
    cos, sin = jnp.cos(ang), jnp.sin(ang)
    ones = jnp.ones((seq_len, MLA_NOPE_DIM), F32)
    zeros_n = jnp.zeros((seq_len, MLA_NOPE_DIM), F32)
    pad = jnp.zeros((seq_len, LANES - MLA_NOPE_DIM - MLA_ROPE_DIM), F32)
    cos_t = jnp.concatenate([ones, cos, cos, pad], axis=1)
    sin_t = jnp.concatenate([zeros_n, sin, sin, pad], axis=1)
    return cos_t, sin_t


def _rot_cols(w):
    half = MLA_ROPE_DIM // 2
    return jnp.concatenate([-w[:, half:], w[:, :half]], axis=1)


def _head_block(nope, rope):
    pad = jnp.zeros((nope.shape[0], LANES - MLA_NOPE_DIM - MLA_ROPE_DIM), F32)
    return jnp.concatenate([nope, rope, pad], axis=1)


def _prep_inproj_weights(w_in, w_uq, w_ukv):
    d = w_in.shape[0]
    pts = np.cumsum((FOX_WIDTH, FOX_WIDTH, FOX_WIDTH, FOX_HEADS, MLA_Q_LORA, MLA_KV_LORA, MLA_ROPE_DIM))
    w_fq, w_fk, w_fv = w_in[:, :pts[0]], w_in[:, pts[0]:pts[1]], w_in[:, pts[1]:pts[2]]
    w_fl, w_ql = w_in[:, pts[2]:pts[3]], w_in[:, pts[3]:pts[4]]
    w_kvl, w_kr = w_in[:, pts[4]:pts[5]], w_in[:, pts[5]:pts[6]]
    zeros_n = jnp.zeros((d, MLA_NOPE_DIM), F32)
    misc = jnp.concatenate([w_fl, jnp.zeros((d, LANES - FOX_HEADS), F32)], axis=1)
    w1 = jnp.concatenate([w_fq, w_fk, w_fv, w_ql, w_kvl, misc, _head_block(zeros_n, w_kr),
                          _head_block(zeros_n, _rot_cols(w_kr))], axis=1).astype(BF16)
    qd = MLA_NOPE_DIM + MLA_ROPE_DIM
    zq = jnp.zeros((w_uq.shape[0], MLA_NOPE_DIM), F32)
    wqa, wqb, wk, wv = [], [], [], []
    for h in range(MLA_HEADS):
        nope = w_uq[:, h * qd:h * qd + MLA_NOPE_DIM]
        rope = w_uq[:, h * qd + MLA_NOPE_DIM:(h + 1) * qd]
        wqa.append(_head_block(nope, rope))
        wqb.append(_head_block(zq, _rot_cols(rope)))
        kvd = MLA_NOPE_DIM + MLA_V_DIM
        wk.append(jnp.concatenate([w_ukv[:, h * kvd:h * kvd + MLA_NOPE_DIM],
                                   jnp.zeros((w_ukv.shape[0], LANES - MLA_NOPE_DIM), F32)], axis=1))
        wv.append(w_ukv[:, h * kvd + MLA_NOPE_DIM:(h + 1) * kvd])
    cat = lambda xs: jnp.concatenate(xs, axis=1).astype(BF16)
    return w1, cat(wqa), cat(wqb), cat(wk), cat(wv)


def _routing_tables(counts, n_tiles, tm):
    tiles_e = (counts + tm - 1) // tm
    tile_end = jnp.cumsum(tiles_e)
    off = ((tile_end - tiles_e) * tm).astype(I32)
    total = tile_end[-1]
    ti = jnp.arange(n_tiles, dtype=I32)
    te = jnp.minimum(jnp.sum(ti[:, None] >= tile_end[None, :], axis=1), N_EXPERTS - 1).astype(I32)
    valid = ti < total
    last_e = jnp.sum(jnp.where(ti == total - 1, te, 0))
    te = jnp.where(valid, te, last_e).astype(I32)
    last_tile_row = (off + (tiles_e - 1) * tm).astype(I32)
    return off, last_tile_row, te, total.astype(I32).reshape(1)


def kernel(x, mem, g_mix, w_in, b_f, g_q_lat, w_uq, g_kv_lat, w_ukv, g_fox_out, g_mla_out, w_o, g_mem_q, w_mem_q,
           g_mem_kv, w_mem_kv, w_mem_o, g_ffn, w_router, b_router, w_gate_up, b_gate_up, w_down, b_down, g_final):
    b, s, d = x.shape
    n = b * s
    depth = g_mix.shape[0]
    cos_t, sin_t = _rope_tables(s)
    row = lambda v: v.reshape(1, -1)
    h = x
    for l in range(depth):
        w1, wqa, wqb, wk, wv = _prep_inproj_weights(w_in[l], w_uq[l], w_ukv[l])
        fq, fk, fv, flog, qm, km, vm = _inproj(h, row(g_mix[l]), w1, row(g_q_lat[l]), wqa, wqb, row(g_kv_lat[l]),
                                                wk, wv, cos_t, sin_t)
        cq, ck = _decay(flog, b_f[l])
        o_fox = _attention(True, fq, fk, fv, cq, ck)
        o_mla = _attention(False, qm, km, vm)
        h1, qmem = _mixout(o_fox.reshape(n, -1), o_mla.reshape(n, -1), h.reshape(n, d), row(g_fox_out[l]),
                           row(g_mla_out[l]), w_o[l].astype(BF16), row(g_mem_q[l]), w_mem_q[l].astype(BF16))
        kmem, vmem = _memkv(mem, row(g_mem_kv[l]), w_mem_kv[l].astype(BF16))
        wr = jnp.zeros((d, LANES), F32).at[:, :N_EXPERTS].set(w_router[l])
        wr_hi = wr.astype(BF16)
        wr_lo = (wr - wr_hi.astype(F32)).astype(BF16)
        br = jnp.zeros((1, LANES), F32).at[0, :N_EXPERTS].set(b_router[l])
        h2, at, meta, cnt = _memrouter(qmem.reshape(b, s, -1), kmem, vmem, h1.reshape(b, s, d),
                                       w_mem_o[l].astype(BF16), row(g_ffn[l]), wr_hi, wr_lo, br)
        meta2 = meta.reshape(n, LANES)
        ek = meta2[:, 0:TOP_K].astype(I32)
        rk = meta2[:, TOP_K:2 * TOP_K].astype(I32)
        n_tiles = n * TOP_K // TM_MOE + N_EXPERTS
        counts = cnt[0, :N_EXPERTS].astype(I32)
        off, last_tile_row, te, nv = _routing_tables(counts, n_tiles, TM_MOE)
        off_of = jnp.sum(jnp.where(ek[..., None] == jnp.arange(N_EXPERTS, dtype=I32), off, 0), axis=-1)
        pos8 = ((off_of + rk) * SUBLANES).astype(I32).reshape(-1)
        xs = _dispatch(last_tile_row, counts, nv, pos8, at, n_tiles, TM_MOE)
        y = _moe(te, nv, xs, w_gate_up[l], b_gate_up[l].reshape(N_EXPERTS, 1, -1), w_down[l],
                 b_down[l].reshape(N_EXPERTS, 1, -1), TM_MOE)
        h = _combine(pos8, h2.reshape(n, d), meta2, y, row(g_final), l == depth - 1).reshape(b, s, d)
    return h
```

```python
import functools

import numpy as np
import jax
import jax.numpy as jnp
from jax import lax
from jax.experimental import pallas as pl
from jax.experimental.pallas import tpu as pltpu

F32 = jnp.float32
BF16 = jnp.bfloat16
I32 = jnp.int32

LANES = 128
SUBLANES = 8
RMS_EPS = 1e-6
CHUNK = 64
FOX_HEADS = 8
FOX_HEAD_DIM = 64
FOX_WIDTH = FOX_HEADS * FOX_HEAD_DIM
MLA_HEADS = 8
MLA_Q_LORA = 384
MLA_KV_LORA = 256
MLA_NOPE_DIM = 64
MLA_ROPE_DIM = 32
MLA_V_DIM = 64
MLA_WIDTH = MLA_HEADS * MLA_V_DIM
ROPE_BASE = 10000.0
MEM_HEADS = 4
MEM_HEAD_DIM = 128
MEM_WIDTH = MEM_HEADS * MEM_HEAD_DIM
N_EXPERTS = 32
TOP_K = 4
SWIGLU_LIMIT = 7.0
SWIGLU_ALPHA = 1.702

LOG2E = 1.4426950408889634
NEG = -1e30
FOX_QSCALE = FOX_HEAD_DIM ** -0.5 * LOG2E
MLA_QSCALE = (MLA_NOPE_DIM + MLA_ROPE_DIM) ** -0.5 * LOG2E
MEM_QSCALE = MEM_HEAD_DIM ** -0.5 * LOG2E
FOX_DECAY_LANES = 16

VMEM_LIMIT = 56 * 1024 * 1024

TS_PROJ = 512
TS_MIX = 512
TS_MEM = 512
T_ATT = 256
ROW_SPLIT = 1
DECAY_BLK = 256
TM_MOE = 256
TD_ROWS = 256
MOE_CH = 512


def _cparams(n_axes):
    return pltpu.CompilerParams(dimension_semantics=("arbitrary",) * n_axes, vmem_limit_bytes=VMEM_LIMIT)


def _dot(a, b):
    return jnp.dot(a, b, preferred_element_type=F32)


def _dot_nt(a, b):
    return lax.dot_general(a, b, (((1,), (1,)), ((), ())), preferred_element_type=F32)


def _rms(x, g):
    return x * lax.rsqrt(jnp.mean(x * x, axis=-1, keepdims=True) + RMS_EPS) * g


def _split3(x):
    hi = x.astype(BF16)
    r1 = x - hi.astype(F32)
    mid = r1.astype(BF16)
    lo = (r1 - mid.astype(F32)).astype(BF16)
    return hi, mid, lo


def _interleave_blocks(a, b):
    parts = []
    for p in range(a.shape[1] // LANES):
        parts += [a[:, p * LANES:(p + 1) * LANES], b[:, p * LANES:(p + 1) * LANES]]
    return jnp.concatenate(parts, axis=1)


def _inproj_kernel(x_ref, g_ref, w1_ref, gq_ref, wqn_ref, wqa_ref, wqb_ref, gkv_ref, wk_ref, wv_ref, cos_ref, sin_ref,
                   fq_ref, fk_ref, fv_ref, fl_ref, qm_ref, km_ref, vm_ref):
    a = _rms(x_ref[0], g_ref[...]).astype(BF16)

    def proj(lo, hi):
        return _dot(a, w1_ref[:, lo:hi])

    c0 = 3 * FOX_WIDTH
    c1 = c0 + MLA_Q_LORA
    c2 = c1 + MLA_KV_LORA
    fq_ref[0] = (proj(0, FOX_WIDTH) * FOX_QSCALE).astype(BF16)
    fk_ref[0] = proj(FOX_WIDTH, 2 * FOX_WIDTH).astype(BF16)
    fv_ref[0] = proj(2 * FOX_WIDTH, c0).astype(BF16)
    qn = _rms(proj(c0, c1), gq_ref[...]).astype(BF16)
    kvn = _rms(proj(c1, c2), gkv_ref[...]).astype(BF16)
    fl_ref[0] = proj(c2, c2 + LANES)
    cos = cos_ref[...]
    sin = sin_ref[...]
    pairs = MLA_HEADS // 2
    kpe = (proj(c2 + LANES, c2 + 2 * LANES) * cos + proj(c2 + 2 * LANES, c2 + 3 * LANES) * sin).astype(BF16)
    cos4 = jnp.concatenate([cos] * pairs, axis=1)
    sin4 = jnp.concatenate([sin] * pairs, axis=1)
    q_nope = (_dot(qn, wqn_ref[...]) * MLA_QSCALE).astype(BF16)
    q_rope = ((_dot(qn, wqa_ref[...]) * cos4 + _dot(qn, wqb_ref[...]) * sin4) * MLA_QSCALE).astype(BF16)
    qm_ref[0] = _interleave_blocks(q_nope, q_rope)
    k_nope = _dot(kvn, wk_ref[...]).astype(BF16)
    km_ref[0] = _interleave_blocks(k_nope, jnp.concatenate([kpe] * pairs, axis=1))
    vm_ref[0] = _dot(kvn, wv_ref[...]).astype(BF16)


def _inproj(x, g_mix, w1, g_q, wqn, wqa, wqb, g_kv, wk, wv, cos_t, sin_t):
    b, s, d = x.shape
    ts = min(TS_PROJ, s)

    def full(arr):
        return pl.BlockSpec(arr.shape, lambda bi, si: (0,) * arr.ndim)

    def tok(width):
        return pl.BlockSpec((1, ts, width), lambda bi, si: (bi, si, 0))

    tab = pl.BlockSpec((ts, LANES), lambda bi, si: (si, 0))
    pair_w = 2 * LANES * (MLA_HEADS // 2)
    out_shapes = (
        jax.ShapeDtypeStruct((b, s, FOX_WIDTH), BF16),
        jax.ShapeDtypeStruct((b, s, FOX_WIDTH), BF16),
        jax.ShapeDtypeStruct((b, s, FOX_WIDTH), BF16),
        jax.ShapeDtypeStruct((b, s, LANES), F32),
        jax.ShapeDtypeStruct((b, s, pair_w), BF16),
        jax.ShapeDtypeStruct((b, s, pair_w), BF16),
        jax.ShapeDtypeStruct((b, s, MLA_WIDTH), BF16),
    )
    return pl.pallas_call(
        _inproj_kernel,
        grid=(b, s // ts),
        in_specs=[tok(d), full(g_mix), full(w1), full(g_q), full(wqn), full(wqa), full(wqb), full(g_kv), full(wk),
                  full(wv), tab, tab],
        out_specs=(tok(FOX_WIDTH), tok(FOX_WIDTH), tok(FOX_WIDTH), tok(LANES), tok(pair_w), tok(pair_w),
                   tok(MLA_WIDTH)),
        out_shape=out_shapes,
        compiler_params=_cparams(2),
        name="inproj",
    )(x, g_mix, w1, g_q, wqn, wqa, wqb, g_kv, wk, wv, cos_t, sin_t)


def _decay_kernel(fl_ref, bf_ref, pq_ref, pk_ref, oq_ref, ok_ref, cq_ref, ck_ref, *, blk):
    s = fl_ref.shape[1]
    row = lax.broadcasted_iota(I32, (blk, blk), 0)
    col = lax.broadcasted_iota(I32, (blk, blk), 1)
    tri = jnp.where(row >= col, 1.0, 0.0).astype(BF16)
    carry = jnp.zeros((1, LANES), F32)
    for i in range(s // blk):
        sl = slice(i * blk, (i + 1) * blk)
        z = fl_ref[0, sl, :] + bf_ref[...]
        lf = (jnp.minimum(z, 0.0) - jnp.log1p(jnp.exp(-jnp.abs(z)))) * LOG2E
        h, m, l = _split3(lf)
        cs = _dot(tri, h) + _dot(tri, m) + _dot(tri, l) + carry
        carry = cs[blk - 1:blk, :]
        h, m, l = _split3(cs)
        cq = _dot(h, pq_ref[0]) + _dot(m, pq_ref[1]) + _dot(l, pq_ref[2]) + oq_ref[...]
        ck = _dot(h, pk_ref[0]) + _dot(m, pk_ref[1]) + _dot(l, pk_ref[2]) + ok_ref[...]
        cq_ref[0, sl, :] = cq.astype(BF16)
        ck_ref[0, sl, :] = ck.astype(BF16)


def _decay_tables():
    pq = np.zeros((3, LANES, LANES), np.float32)
    pk = np.zeros((3, LANES, LANES), np.float32)
    oq = np.zeros((1, LANES), np.float32)
    ok = np.zeros((1, LANES), np.float32)
    for h in range(FOX_HEADS):
        for part in range(3):
            pq[part, h, FOX_DECAY_LANES * h + part] = 1.0
            pk[part, h, FOX_DECAY_LANES * h + 3 + part] = -1.0
            oq[0, FOX_DECAY_LANES * h + 3 + part] = 1.0
            ok[0, FOX_DECAY_LANES * h + part] = 1.0
    return jnp.asarray(pq, BF16), jnp.asarray(pk, BF16), jnp.asarray(oq), jnp.asarray(ok)


def _decay(flog, b_f):
    b, s, _ = flog.shape
    blk = min(DECAY_BLK, s)
    pq, pk, oq, ok = _decay_tables()
    bf = jnp.zeros((1, LANES), F32).at[0, :FOX_HEADS].set(b_f)

    def full(arr):
        return pl.BlockSpec(arr.shape, lambda bi: (0,) * arr.ndim)

    seq = pl.BlockSpec((1, s, LANES), lambda bi: (bi, 0, 0))
    return pl.pallas_call(
        functools.partial(_decay_kernel, blk=blk),
        grid=(b,),
        in_specs=[seq, full(bf), full(pq), full(pk), full(oq), full(ok)],
        out_specs=(seq, seq),
        out_shape=(jax.ShapeDtypeStruct((b, s, LANES), BF16), jax.ShapeDtypeStruct((b, s, LANES), BF16)),
        compiler_params=_cparams(1),
        name="decay",
    )(flog, bf, pq, pk, oq, ok)


def _attn_pair_kernel(*refs, fox, t):
    if fox:
        q_ref, cq_ref, k_ref, ck_ref, v_ref, o_ref = refs
        group, base = FOX_DECAY_LANES, 2 * pl.program_id(1)
    else:
        q_ref, k_ref, v_ref, o_ref = refs
        group, base = MLA_ROPE_DIM, 0
    i = pl.program_id(2)
    lane = lax.broadcasted_iota(I32, (1, LANES), 1)

    def variant(iv):
        row = lax.broadcasted_iota(I32, (t, t), 0)
        col = lax.broadcasted_iota(I32, (t, t), 1)
        allowed = (col <= row) if fox else ((col // CHUNK) <= (row // CHUNK))
        past = iv * t
        if fox:
            q_main, q_extra = q_ref[0], cq_ref[0]
        else:
            q_main, q_extra = q_ref[0, :, :LANES], q_ref[0, :, LANES:]
        zero = jnp.zeros_like(q_main)

        def head_rows(hh):
            return jnp.concatenate([jnp.where((lane // (LANES // 2)) == hh, q_main, zero),
                                    jnp.where((lane // group) == base + hh, q_extra, zero)], axis=1)

        def keys(lo, hi):
            if fox:
                return jnp.concatenate([k_ref[0, lo:hi, :], ck_ref[0, lo:hi, :]], axis=1)
            return k_ref[0, lo:hi, :]

        rg = t // ROW_SPLIT
        outs = []
        for hh in range(2):
            qh = head_rows(hh)
            chunks = []
            for r in range(ROW_SPLIT):
                qa = qh[r * rg:(r + 1) * rg]
                dcols = (r + 1) * rg
                s_diag = jnp.where(allowed[r * rg:(r + 1) * rg, :dcols], _dot_nt(qa, keys(past, past + dcols)), NEG)
                m = jnp.max(s_diag, axis=-1, keepdims=True)
                if iv > 0:
                    s_past = _dot_nt(qa, keys(0, past))
                    m = jnp.maximum(m, jnp.max(s_past, axis=-1, keepdims=True))
                    p_past = jnp.exp2(s_past - m)
                    l = jnp.sum(p_past, axis=-1, keepdims=True)
                    acc = _dot(p_past.astype(BF16), v_ref[0, 0:past, :])
                p_diag = jnp.exp2(s_diag - m)
                l_diag = jnp.sum(p_diag, axis=-1, keepdims=True)
                acc_diag = _dot(p_diag.astype(BF16), v_ref[0, past:past + dcols, :])
                if iv > 0:
                    l = l + l_diag
                    acc = acc + acc_diag
                else:
                    l, acc = l_diag, acc_diag
                chunks.append(acc * (1.0 / l))
            outs.append(jnp.concatenate(chunks, axis=0) if ROW_SPLIT > 1 else chunks[0])
        o_ref[0] = jnp.where(lane < LANES // 2, outs[0], outs[1]).astype(o_ref.dtype)

    for iv in range(k_ref.shape[1] // t):
        pl.when(i == iv)(functools.partial(variant, iv))


def _attention(fox, q, k, v, cq=None, ck=None):
    b, s, _ = v.shape
    t = min(T_ATT, s)
    pairs = v.shape[2] // LANES
    qw = q.shape[2] // pairs
    kw = k.shape[2] // pairs
    qspec = pl.BlockSpec((1, t, qw), lambda bi, pi, qi: (bi, qi, pi))
    kspec = pl.BlockSpec((1, s, kw), lambda bi, pi, qi: (bi, 0, pi))
    vspec = pl.BlockSpec((1, s, LANES), lambda bi, pi, qi: (bi, 0, pi))
    ospec = pl.BlockSpec((1, t, LANES), lambda bi, pi, qi: (bi, qi, pi))
    if fox:
        cqspec = pl.BlockSpec((1, t, LANES), lambda bi, pi, qi: (bi, qi, 0))
        ckspec = pl.BlockSpec((1, s, LANES), lambda bi, pi, qi: (bi, 0, 0))
        in_specs = [qspec, cqspec, kspec, ckspec, vspec]
        args = (q, cq, k, ck, v)
    else:
        in_specs = [qspec, kspec, vspec]
        args = (q, k, v)
    return pl.pallas_call(
        functools.partial(_attn_pair_kernel, fox=fox, t=t),
        grid=(b, pairs, s // t),
        in_specs=in_specs,
        out_specs=ospec,
        out_shape=jax.ShapeDtypeStruct((b, s, pairs * LANES), BF16),
        compiler_params=_cparams(3),
        name="fox_attn" if fox else "mla_attn",
    )(*args)


def _mixout_kernel(of_ref, om_ref, x_ref, gf_ref, gm_ref, wo_ref, gq_ref, wq_ref, h1_ref, qmem_ref):
    nf = _rms(of_ref[...].astype(F32), gf_ref[...]).astype(BF16)
    nm = _rms(om_ref[...].astype(F32), gm_ref[...]).astype(BF16)
    h1 = x_ref[...] + _dot(nf, wo_ref[0:FOX_WIDTH, :]) + _dot(nm, wo_ref[FOX_WIDTH:FOX_WIDTH + MLA_WIDTH, :])
    h1_ref[...] = h1
    a = _rms(h1, gq_ref[...]).astype(BF16)
    qmem_ref[...] = (_dot(a, wq_ref[...]) * MEM_QSCALE).astype(BF16)


def _mixout(o_fox, o_mla, x2, g_fox, g_mla, w_o, g_mq, w_mq):
    n, d = x2.shape
    ts = min(TS_MIX, n)

    def full(arr):
        return pl.BlockSpec(arr.shape, lambda i: (0,) * arr.ndim)

    def tok(width):
        return pl.BlockSpec((ts, width), lambda i: (i, 0))

    return pl.pallas_call(
        _mixout_kernel,
        grid=(n // ts,),
        in_specs=[tok(FOX_WIDTH), tok(MLA_WIDTH), tok(d), full(g_fox), full(g_mla), full(w_o), full(g_mq),
                  full(w_mq)],
        out_specs=(tok(d), tok(MEM_WIDTH)),
        out_shape=(jax.ShapeDtypeStruct((n, d), F32), jax.ShapeDtypeStruct((n, MEM_WIDTH), BF16)),
        compiler_params=_cparams(1),
        name="mixout",
    )(o_fox, o_mla, x2, g_fox, g_mla, w_o, g_mq, w_mq)


def _memkv_kernel(mem_ref, g_ref, w_ref, k_ref, v_ref):
    a = _rms(mem_ref[0], g_ref[...]).astype(BF16)
    k_ref[0] = _dot(a, w_ref[:, 0:MEM_WIDTH]).astype(BF16)
    v_ref[0] = _dot(a, w_ref[:, MEM_WIDTH:2 * MEM_WIDTH]).astype(BF16)


def _memkv(mem, g, w):
    b, m, d = mem.shape
    kv = pl.BlockSpec((1, m, MEM_WIDTH), lambda bi: (bi, 0, 0))
    return pl.pallas_call(
        _memkv_kernel,
        grid=(b,),
        in_specs=[pl.BlockSpec((1, m, d), lambda bi: (bi, 0, 0)), pl.BlockSpec(g.shape, lambda bi: (0, 0)),
                  pl.BlockSpec(w.shape, lambda bi: (0, 0))],
        out_specs=(kv, kv),
        out_shape=(jax.ShapeDtypeStruct((b, m, MEM_WIDTH), BF16), jax.ShapeDtypeStruct((b, m, MEM_WIDTH), BF16)),
        compiler_params=_cparams(1),
        name="memkv",
    )(mem, g, w)


def _store_token_tiles(ref, val):
    rows = val.shape[0]
    for j in range(SUBLANES):
        ref[pl.ds(j, rows, stride=SUBLANES), :] = val[:, j * LANES:(j + 1) * LANES]


def _load_token_tiles(ref, rows):
    return jnp.concatenate([ref[pl.ds(j, rows, stride=SUBLANES), :] for j in range(SUBLANES)], axis=1)


def _memrouter_kernel(q_ref, k_ref, v_ref, h1_ref, wo_ref, gffn_ref, wrh_ref, wrl_ref, br_ref,
                      h2_ref, at_ref, meta_ref, cnt_ref, carry_ref, *, ts):
    first = jnp.logical_and(pl.program_id(0) == 0, pl.program_id(1) == 0)

    @pl.when(first)
    def _():
        carry_ref[...] = jnp.zeros_like(carry_ref)

    q = q_ref[0]
    heads = []
    for h in range(MEM_HEADS):
        sl = slice(h * MEM_HEAD_DIM, (h + 1) * MEM_HEAD_DIM)
        sc = _dot_nt(q[:, sl], k_ref[0, :, sl])
        m = jnp.max(sc, axis=-1, keepdims=True)
        pm = jnp.exp2(sc - m)
        l = jnp.sum(pm, axis=-1, keepdims=True)
        heads.append((_dot(pm.astype(BF16), v_ref[0, :, sl]) / l).astype(BF16))
    h2 = h1_ref[0] + _dot(jnp.concatenate(heads, axis=1), wo_ref[...])
    h2_ref[0] = h2

    a = _rms(h2, gffn_ref[...])
    _store_token_tiles(at_ref, a)
    a_hi = a.astype(BF16)
    a_lo = (a - a_hi.astype(F32)).astype(BF16)
    logits = _dot(a_hi, wrh_ref[...]) + _dot(a_lo, wrh_ref[...]) + _dot(a_hi, wrl_ref[...]) + br_ref[...]
    lane = lax.broadcasted_iota(I32, (ts, LANES), 1)
    work = jnp.where(lane < N_EXPERTS, logits, NEG)
    vals, idxs, sels = [], [], []
    for _ in range(TOP_K):
        mk = jnp.max(work, axis=-1, keepdims=True)
        ik = jnp.min(jnp.where(work == mk, lane, LANES), axis=-1, keepdims=True)
        sel = lane == ik
        work = jnp.where(sel, NEG, work)
        vals.append(mk)
        idxs.append(ik)
        sels.append(sel)
    exps = [jnp.exp(vk - vals[0]) for vk in vals]
    denom = exps[0] + exps[1] + exps[2] + exps[3]
    chosen = jnp.logical_or(jnp.logical_or(sels[0], sels[1]), jnp.logical_or(sels[2], sels[3]))
    onehot = jnp.where(chosen, 1.0, 0.0)
    row = lax.broadcasted_iota(I32, (ts, ts), 0)
    col = lax.broadcasted_iota(I32, (ts, ts), 1)
    tri = jnp.where(row > col, 1.0, 0.0).astype(BF16)
    carry = carry_ref[...]
    rank = _dot(tri, onehot.astype(BF16)) + carry
    carry_new = carry + jnp.sum(onehot, axis=0, keepdims=True)
    carry_ref[...] = carry_new
    cnt_ref[...] = carry_new
    meta = jnp.zeros((ts, LANES), F32)
    for kk in range(TOP_K):
        rk = jnp.sum(jnp.where(sels[kk], rank, 0.0), axis=-1, keepdims=True)
        meta = jnp.where(lane == kk, idxs[kk].astype(F32), meta)
        meta = jnp.where(lane == TOP_K + kk, rk, meta)
        meta = jnp.where(lane == 2 * TOP_K + kk, exps[kk] / denom, meta)
    meta_ref[0] = meta


def _memrouter(qmem, kmem, vmem, h1, w_mo, g_ffn, wr_hi, wr_lo, b_r):
    b, s, d = h1.shape
    m = kmem.shape[1]
    ts = min(TS_MEM, s)

    def full(arr):
        return pl.BlockSpec(arr.shape, lambda bi, si: (0,) * arr.ndim)

    def tok(width):
        return pl.BlockSpec((1, ts, width), lambda bi, si: (bi, si, 0))

    kv = pl.BlockSpec((1, m, MEM_WIDTH), lambda bi, si: (bi, 0, 0))
    cnt = pl.BlockSpec((1, LANES), lambda bi, si: (0, 0))
    return pl.pallas_call(
        functools.partial(_memrouter_kernel, ts=ts),
        grid=(b, s // ts),
        in_specs=[tok(MEM_WIDTH), kv, kv, tok(d), full(w_mo), full(g_ffn), full(wr_hi), full(wr_lo), full(b_r)],
        out_specs=(tok(d), pl.BlockSpec((ts * SUBLANES, LANES), lambda bi, si: (bi * (s // ts) + si, 0)),
                   tok(LANES), cnt),
        out_shape=(jax.ShapeDtypeStruct((b, s, d), F32), jax.ShapeDtypeStruct((b * s * SUBLANES, LANES), F32),
                   jax.ShapeDtypeStruct((b, s, LANES), F32), jax.ShapeDtypeStruct((1, LANES), F32)),
        scratch_shapes=[pltpu.VMEM((1, LANES), F32)],
        compiler_params=_cparams(2),
        name="memrouter",
    )(qmem, kmem, vmem, h1, w_mo, g_ffn, wr_hi, wr_lo, b_r)


def _tile_copy(src_ref, src_row8, dst_ref, dst_row8, sem):
    return pltpu.make_async_copy(src_ref.at[pl.ds(pl.multiple_of(src_row8, SUBLANES), SUBLANES), :],
                                 dst_ref.at[pl.ds(pl.multiple_of(dst_row8, SUBLANES), SUBLANES), :], sem)


def _dispatch_kernel(last_ref, cnt_ref, nv_ref, pos_ref, at_ref, xs_ref, zbuf, zsem, sem, *, td, tm, min_used,
                     n_tiles):
    @pl.when(pl.program_id(0) == 0)
    def _():
        zbuf[...] = jnp.zeros_like(zbuf)

        def fill(row):
            return pltpu.make_async_copy(
                zbuf, xs_ref.at[pl.ds(pl.multiple_of(row * SUBLANES, SUBLANES), tm * SUBLANES), :], zsem)

        fills = [(cnt_ref[e] > 0, last_ref[e]) for e in range(N_EXPERTS)]
        fills += [(t >= nv_ref[0], t * tm) for t in range(min_used, n_tiles)]
        for pred, row in fills:
            pl.when(pred)(lambda row=row: fill(row).start())
        for pred, row in fills:
            pl.when(pred)(lambda row=row: fill(row).wait())

    def issue(r, c):
        for kk in range(TOP_K):
            _tile_copy(at_ref, r * SUBLANES, xs_ref, pos_ref[r * TOP_K + kk], sem).start(priority=kk % 2)
        return c

    lax.fori_loop(0, td, issue, 0, unroll=4)
    for _ in range(TOP_K):
        pltpu.make_async_copy(at_ref, xs_ref.at[pl.ds(0, td * SUBLANES), :], sem).wait()


def _dispatch(last_tile_row, cnt, nv, pos8, at, n_tiles, tm):
    n = at.shape[0] // SUBLANES
    td = min(TD_ROWS, n)
    grid_spec = pltpu.PrefetchScalarGridSpec(
        num_scalar_prefetch=3,
        grid=(n // td,),
        in_specs=[pl.BlockSpec((td * TOP_K,), lambda i, *_: (i,), memory_space=pltpu.SMEM),
                  pl.BlockSpec((td * SUBLANES, LANES), lambda i, *_: (i, 0))],
        out_specs=pl.BlockSpec(memory_space=pl.ANY),
        scratch_shapes=[pltpu.VMEM((tm * SUBLANES, LANES), F32), pltpu.SemaphoreType.DMA, pltpu.SemaphoreType.DMA],
    )
    min_used = n * TOP_K // tm
    return pl.pallas_call(
        functools.partial(_dispatch_kernel, td=td, tm=tm, min_used=min_used, n_tiles=n_tiles),
        grid_spec=grid_spec,
        out_shape=jax.ShapeDtypeStruct((n_tiles * tm * SUBLANES, LANES), F32),
        compiler_params=_cparams(1),
        name="dispatch",
    )(last_tile_row, cnt, nv, pos8, at)


def _moe_kernel(te_ref, nv_ref, slot_ref, nxt_ref, xs_ref, wgu_hbm, bgu_ref, wd_hbm, bd_ref, y_ref,
                wgu_f32, wd_f32, wgu_bf, wd_bf, sem, *, ch, tm):
    i = pl.program_id(0)
    d_exp = wd_hbm.shape[1]
    valid = i < nv_ref[0]
    new_expert = jnp.logical_or(i == 0, te_ref[i] != te_ref[jnp.maximum(i - 1, 0)])

    def weight_copies(e, sl):
        return (pltpu.make_async_copy(wgu_hbm.at[e], wgu_f32.at[sl], sem.at[sl, 0]),
                pltpu.make_async_copy(wd_hbm.at[e], wd_f32.at[sl], sem.at[sl, 1]))

    @pl.when(i == 0)
    def _():
        for c in weight_copies(te_ref[0], slot_ref[0]):
            c.start()

    @pl.when(jnp.logical_and(new_expert, valid))
    def _():
        sl = slot_ref[i]
        for c in weight_copies(te_ref[i], sl):
            c.wait()
        wgu_bf[...] = wgu_f32[sl].astype(BF16)
        wd_bf[...] = wd_f32[sl].astype(BF16)

        @pl.when(nxt_ref[i] >= 0)
        def _():
            for c in weight_copies(nxt_ref[i], 1 - sl):
                c.start()

    @pl.when(valid)
    def _():
        x = _load_token_tiles(xs_ref, tm).astype(BF16)
        acc = jnp.zeros((tm, wd_hbm.shape[2]), F32)
        for c in range(d_exp // ch):
            def gu(lo):
                return _dot(x, wgu_bf[:, lo:lo + ch]) + bgu_ref[0, :, lo:lo + ch]

            gate = jnp.minimum(gu(c * ch), SWIGLU_LIMIT)
            up = jnp.clip(gu(d_exp + c * ch), -SWIGLU_LIMIT, SWIGLU_LIMIT)
            act = gate * (1.0 / (1.0 + jnp.exp(-SWIGLU_ALPHA * gate))) * (up + 1.0)
            acc = acc + _dot(act.astype(BF16), wd_bf[c * ch:(c + 1) * ch, :])
        _store_token_tiles(y_ref, acc + bd_ref[0])

    @pl.when(jnp.logical_not(valid))
    def _():
        y_ref[...] = jnp.zeros_like(y_ref)


def _moe(te, nv, slot, nxt, xs, wgu, bgu, wd, bd, tm):
    n_tiles = xs.shape[0] // (tm * SUBLANES)
    _, d, d_exp2 = wgu.shape
    d_exp = wd.shape[1]
    assert d == SUBLANES * LANES and wd.shape[2] == d
    tile = pl.BlockSpec((tm * SUBLANES, LANES), lambda i, te, *_: (i, 0))
    grid_spec = pltpu.PrefetchScalarGridSpec(
        num_scalar_prefetch=4,
        grid=(n_tiles,),
        in_specs=[
            tile,
            pl.BlockSpec(memory_space=pl.ANY),
            pl.BlockSpec((1, 1, d_exp2), lambda i, te, *_: (te[i], 0, 0)),
            pl.BlockSpec(memory_space=pl.ANY),
            pl.BlockSpec((1, 1, d), lambda i, te, *_: (te[i], 0, 0)),
        ],
        out_specs=tile,
        scratch_shapes=[pltpu.VMEM((2, d, d_exp2), F32), pltpu.VMEM((2, d_exp, d), F32),
                        pltpu.VMEM((d, d_exp2), BF16), pltpu.VMEM((d_exp, d), BF16),
                        pltpu.SemaphoreType.DMA((2, 2))],
    )
    return pl.pallas_call(
        functools.partial(_moe_kernel, ch=min(MOE_CH, d_exp), tm=tm),
        grid_spec=grid_spec,
        out_shape=jax.ShapeDtypeStruct(xs.shape, F32),
        compiler_params=_cparams(1),
        name="moe",
    )(te, nv, slot, nxt, xs, wgu, bgu, wd, bd)


def _combine_kernel(pos_ref, posn_ref, h2_ref, meta_ref, y_ref, gf_ref, o_ref, ybuf, sem, *, td, final_norm):
    i = pl.program_id(0)
    slot = i % 2

    def issue_block(p_ref, sl):
        def issue(r, c):
            for kk in range(TOP_K):
                _tile_copy(y_ref, p_ref[r * TOP_K + kk], ybuf.at[sl, kk], r * SUBLANES,
                           sem.at[sl]).start(priority=kk % 2)
            return c

        lax.fori_loop(0, td, issue, 0, unroll=4)

    @pl.when(i == 0)
    def _():
        issue_block(pos_ref, 0)

    @pl.when(i + 1 < pl.num_programs(0))
    def _():
        issue_block(posn_ref, 1 - slot)

    for kk in range(TOP_K):
        pltpu.make_async_copy(y_ref.at[pl.ds(0, td * SUBLANES), :], ybuf.at[slot, kk], sem.at[slot]).wait()

    meta = meta_ref[...]
    acc = h2_ref[...]
    for kk in range(TOP_K):
        gate = meta[:, 2 * TOP_K + kk:2 * TOP_K + kk + 1]
        acc = acc + gate * _load_token_tiles(ybuf.at[slot, kk], td)
    o_ref[...] = _rms(acc, gf_ref[...]) if final_norm else acc


def _combine(pos8, h2, meta, y, g_final, final_norm):
    n, d = h2.shape
    td = min(TD_ROWS, n)
    n_blocks = n // td
    grid_spec = pltpu.PrefetchScalarGridSpec(
        num_scalar_prefetch=0,
        grid=(n_blocks,),
        in_specs=[pl.BlockSpec((td * TOP_K,), lambda i: (i,), memory_space=pltpu.SMEM),
                  pl.BlockSpec((td * TOP_K,), lambda i: (jnp.minimum(i + 1, n_blocks - 1),), memory_space=pltpu.SMEM),
                  pl.BlockSpec((td, d), lambda i: (i, 0)),
                  pl.BlockSpec((td, LANES), lambda i: (i, 0)),
                  pl.BlockSpec(memory_space=pl.ANY),
                  pl.BlockSpec(g_final.shape, lambda i: (0, 0))],
        out_specs=pl.BlockSpec((td, d), lambda i: (i, 0)),
        scratch_shapes=[pltpu.VMEM((2, TOP_K, td * SUBLANES, LANES), F32), pltpu.SemaphoreType.DMA((2,))],
    )
    return pl.pallas_call(
        functools.partial(_combine_kernel, td=td, final_norm=final_norm),
        grid_spec=grid_spec,
        out_shape=jax.ShapeDtypeStruct((n, d), F32),
        compiler_params=_cparams(1),
        name="combine",
    )(pos8, pos8, h2, meta, y, g_final)


def _rope_tables(seq_len):
    inv = 1.0 / (ROPE_BASE ** (jnp.arange(0, MLA_ROPE_DIM, 2, dtype=F32) / MLA_ROPE_DIM))
    ang = jnp.arange(seq_len, dtype=F32)[:, None] * inv[None, :]
    cos, sin = jnp.cos(ang), jnp.sin(ang)
    pad = jnp.zeros((seq_len, LANES - 2 * MLA_ROPE_DIM), F32)
    cos_t = jnp.concatenate([cos, cos, cos, cos, pad], axis=1)
    sin_t = jnp.concatenate([sin, sin, sin, sin, pad], axis=1)
    return cos_t, sin_t


def _rot_cols(w):
    half = MLA_ROPE_DIM // 2
    return jnp.concatenate([-w[:, half:], w[:, :half]], axis=1)


def _rope_pair_block(r0, r1):
    pad = jnp.zeros((r0.shape[0], LANES - 2 * MLA_ROPE_DIM), F32)
    return jnp.concatenate([r0, r1, pad], axis=1)


def _prep_inproj_weights(w_in, w_uq, w_ukv):
    d = w_in.shape[0]
    pts = np.cumsum((FOX_WIDTH, FOX_WIDTH, FOX_WIDTH, FOX_HEADS, MLA_Q_LORA, MLA_KV_LORA, MLA_ROPE_DIM))
    w_fq, w_fk, w_fv = w_in[:, :pts[0]], w_in[:, pts[0]:pts[1]], w_in[:, pts[1]:pts[2]]
    w_fl, w_ql = w_in[:, pts[2]:pts[3]], w_in[:, pts[3]:pts[4]]
    w_kvl, w_kr = w_in[:, pts[4]:pts[5]], w_in[:, pts[5]:pts[6]]
    misc = jnp.concatenate([w_fl, jnp.zeros((d, LANES - FOX_HEADS), F32)], axis=1)
    w1 = jnp.concatenate([w_fq, w_fk, w_fv, w_ql, w_kvl, misc, _rope_pair_block(w_kr, w_kr),
                          _rope_pair_block(_rot_cols(w_kr), _rot_cols(w_kr))], axis=1).astype(BF16)
    qd = MLA_NOPE_DIM + MLA_ROPE_DIM
    kvd = MLA_NOPE_DIM + MLA_V_DIM
    q_nope = [w_uq[:, h * qd:h * qd + MLA_NOPE_DIM] for h in range(MLA_HEADS)]
    q_rope = [w_uq[:, h * qd + MLA_NOPE_DIM:(h + 1) * qd] for h in range(MLA_HEADS)]
    wqa = [_rope_pair_block(q_rope[h], q_rope[h + 1]) for h in range(0, MLA_HEADS, 2)]
    wqb = [_rope_pair_block(_rot_cols(q_rope[h]), _rot_cols(q_rope[h + 1])) for h in range(0, MLA_HEADS, 2)]
    wk = [w_ukv[:, h * kvd:h * kvd + MLA_NOPE_DIM] for h in range(MLA_HEADS)]
    wv = [w_ukv[:, h * kvd + MLA_NOPE_DIM:(h + 1) * kvd] for h in range(MLA_HEADS)]
    cat = lambda xs: jnp.concatenate(xs, axis=1).astype(BF16)
    return w1, cat(q_nope), cat(wqa), cat(wqb), cat(wk), cat(wv)


def _routing_tables(counts, n_tiles, tm):
    tiles_e = (counts + tm - 1) // tm
    tile_end = jnp.cumsum(tiles_e)
    off = ((tile_end - tiles_e) * tm).astype(I32)
    total = tile_end[-1]
    ti = jnp.arange(n_tiles, dtype=I32)
    te = jnp.minimum(jnp.sum(ti[:, None] >= tile_end[None, :], axis=1), N_EXPERTS - 1).astype(I32)
    valid = ti < total
    last_e = jnp.sum(jnp.where(ti == total - 1, te, 0))
    te = jnp.where(valid, te, last_e).astype(I32)
    last_tile_row = (off + (tiles_e - 1) * tm).astype(I32)
    used = tiles_e > 0
    ids = jnp.arange(N_EXPERTS, dtype=I32)
    slot_e = (jnp.cumsum(used.astype(I32)) - 1) % 2
    later = jnp.where(used[None, :] & (ids[None, :] > ids[:, None]), ids[None, :], N_EXPERTS)
    nxt_e = jnp.min(later, axis=1)
    nxt_e = jnp.where(nxt_e == N_EXPERTS, -1, nxt_e).astype(I32)
    pick = te[:, None] == ids[None, :]
    slot = jnp.sum(jnp.where(pick, slot_e[None, :], 0), axis=1).astype(I32)
    nxt = jnp.sum(jnp.where(pick, nxt_e[None, :], 0), axis=1).astype(I32)
    return off, last_tile_row, te, total.astype(I32).reshape(1), slot, nxt


def kernel(x, mem, g_mix, w_in, b_f, g_q_lat, w_uq, g_kv_lat, w_ukv, g_fox_out, g_mla_out, w_o, g_mem_q, w_mem_q,
           g_mem_kv, w_mem_kv, w_mem_o, g_ffn, w_router, b_router, w_gate_up, b_gate_up, w_down, b_down, g_final):
    b, s, d = x.shape
    n = b * s
    depth = g_mix.shape[0]
    cos_t, sin_t = _rope_tables(s)
    row = lambda v: v.reshape(1, -1)
    h = x
    for l in range(depth):
        w1, wqn, wqa, wqb, wk, wv = _prep_inproj_weights(w_in[l], w_uq[l], w_ukv[l])
        fq, fk, fv, flog, qm, km, vm = _inproj(h, row(g_mix[l]), w1, row(g_q_lat[l]), wqn, wqa, wqb,
                                                row(g_kv_lat[l]), wk, wv, cos_t, sin_t)
        cq, ck = _decay(flog, b_f[l])
        o_fox = _attention(True, fq, fk, fv, cq, ck)
        o_mla = _attention(False, qm, km, vm)
        h1, qmem = _mixout(o_fox.reshape(n, -1), o_mla.reshape(n, -1), h.reshape(n, d), row(g_fox_out[l]),
                           row(g_mla_out[l]), w_o[l].astype(BF16), row(g_mem_q[l]), w_mem_q[l].astype(BF16))
        kmem, vmem = _memkv(mem, row(g_mem_kv[l]), w_mem_kv[l].astype(BF16))
        wr = jnp.zeros((d, LANES), F32).at[:, :N_EXPERTS].set(w_router[l])
        wr_hi = wr.astype(BF16)
        wr_lo = (wr - wr_hi.astype(F32)).astype(BF16)
        br = jnp.zeros((1, LANES), F32).at[0, :N_EXPERTS].set(b_router[l])
        h2, at, meta, cnt = _memrouter(qmem.reshape(b, s, -1), kmem, vmem, h1.reshape(b, s, d),
                                       w_mem_o[l].astype(BF16), row(g_ffn[l]), wr_hi, wr_lo, br)
        meta2 = meta.reshape(n, LANES)
        ek = meta2[:, 0:TOP_K].astype(I32)
        rk = meta2[:, TOP_K:2 * TOP_K].astype(I32)
        n_tiles = n * TOP_K // TM_MOE + N_EXPERTS
        counts = cnt[0, :N_EXPERTS].astype(I32)
        off, last_tile_row, te, nv, slot, nxt = _routing_tables(counts, n_tiles, TM_MOE)
        off_of = jnp.sum(jnp.where(ek[..., None] == jnp.arange(N_EXPERTS, dtype=I32), off, 0), axis=-1)
        pos8 = ((off_of + rk) * SUBLANES).astype(I32).reshape(-1)
        xs = _dispatch(last_tile_row, counts, nv, pos8, at, n_tiles, TM_MOE)
        y = _moe(te, nv, slot, nxt, xs, w_gate_up[l], b_gate_up[l].reshape(N_EXPERTS, 1, -1), w_down[l],
                 b_down[l].reshape(N_EXPERTS, 1, -1), TM_MOE)
        h = _combine(pos8, h2.reshape(n, d), meta2, y, row(g_final), l == depth - 1).reshape(b, s, d)
    return h
```

```python
import functools

import numpy as np
import jax
import jax.numpy as jnp
from jax import lax
from jax.experimental import pallas as pl
from jax.experimental.pallas import tpu as pltpu

F32 = jnp.float32
BF16 = jnp.bfloat16
I32 = jnp.int32

LANES = 128
SUBLANES = 8
RMS_EPS = 1e-6
CHUNK = 64
FOX_HEADS = 8
FOX_HEAD_DIM = 64
FOX_WIDTH = FOX_HEADS * FOX_HEAD_DIM
MLA_HEADS = 8
MLA_Q_LORA = 384
MLA_KV_LORA = 256
MLA_NOPE_DIM = 64
MLA_ROPE_DIM = 32
MLA_V_DIM = 64
MLA_WIDTH = MLA_HEADS * MLA_V_DIM
ROPE_BASE = 10000.0
MEM_HEADS = 4
MEM_HEAD_DIM = 128
MEM_WIDTH = MEM_HEADS * MEM_HEAD_DIM
N_EXPERTS = 32
TOP_K = 4
SWIGLU_LIMIT = 7.0
SWIGLU_ALPHA = 1.702

LOG2E = 1.4426950408889634
NEG = -1e30
FOX_QSCALE = FOX_HEAD_DIM ** -0.5 * LOG2E
MLA_QSCALE = (MLA_NOPE_DIM + MLA_ROPE_DIM) ** -0.5 * LOG2E
MEM_QSCALE = MEM_HEAD_DIM ** -0.5 * LOG2E
FOX_DECAY_LANES = 16

VMEM_LIMIT = 56 * 1024 * 1024

TS_PROJ = 512
TS_MEM = 512
T_ATT = 256
DECAY_BLK = 256
TM_MOE = 256
TD_ROWS = 256
MOE_CH = 512


def _cparams(n_axes):
    return pltpu.CompilerParams(dimension_semantics=("arbitrary",) * n_axes, vmem_limit_bytes=VMEM_LIMIT)


def _dot(a, b):
    return jnp.dot(a, b, preferred_element_type=F32)


def _dot_nt(a, b):
    return lax.dot_general(a, b, (((1,), (1,)), ((), ())), preferred_element_type=F32)


def _rms(x, g):
    return x * lax.rsqrt(jnp.mean(x * x, axis=-1, keepdims=True) + RMS_EPS) * g


def _split3(x):
    hi = x.astype(BF16)
    r1 = x - hi.astype(F32)
    mid = r1.astype(BF16)
    lo = (r1 - mid.astype(F32)).astype(BF16)
    return hi, mid, lo


def _interleave_blocks(a, b):
    parts = []
    for p in range(a.shape[1] // LANES):
        parts += [a[:, p * LANES:(p + 1) * LANES], b[:, p * LANES:(p + 1) * LANES]]
    return jnp.concatenate(parts, axis=1)


def _inproj_kernel(x_ref, g_ref, w1_ref, gq_ref, wqn_ref, wqa_ref, wqb_ref, gkv_ref, wk_ref, wv_ref, cos_ref, sin_ref,
                   fq_ref, fk_ref, fv_ref, fl_ref, qm_ref, km_ref, vm_ref):
    a = _rms(x_ref[0], g_ref[...]).astype(BF16)

    def proj(lo, hi):
        return _dot(a, w1_ref[:, lo:hi])

    c0 = 3 * FOX_WIDTH
    c1 = c0 + MLA_Q_LORA
    c2 = c1 + MLA_KV_LORA
    fq_ref[0] = (proj(0, FOX_WIDTH) * FOX_QSCALE).astype(BF16)
    fk_ref[0] = proj(FOX_WIDTH, 2 * FOX_WIDTH).astype(BF16)
    fv_ref[0] = proj(2 * FOX_WIDTH, c0).astype(BF16)
    qn = _rms(proj(c0, c1), gq_ref[...]).astype(BF16)
    kvn = _rms(proj(c1, c2), gkv_ref[...]).astype(BF16)
    fl_ref[0] = proj(c2, c2 + LANES)
    cos = cos_ref[...]
    sin = sin_ref[...]
    pairs = MLA_HEADS // 2
    kpe = (proj(c2 + LANES, c2 + 2 * LANES) * cos + proj(c2 + 2 * LANES, c2 + 3 * LANES) * sin).astype(BF16)
    cos4 = jnp.concatenate([cos] * pairs, axis=1)
    sin4 = jnp.concatenate([sin] * pairs, axis=1)
    q_nope = (_dot(qn, wqn_ref[...]) * MLA_QSCALE).astype(BF16)
    q_rope = ((_dot(qn, wqa_ref[...]) * cos4 + _dot(qn, wqb_ref[...]) * sin4) * MLA_QSCALE).astype(BF16)
    qm_ref[0] = _interleave_blocks(q_nope, q_rope)
    k_nope = _dot(kvn, wk_ref[...]).astype(BF16)
    km_ref[0] = _interleave_blocks(k_nope, jnp.concatenate([kpe] * pairs, axis=1))
    vm_ref[0] = _dot(kvn, wv_ref[...]).astype(BF16)


def _inproj(x, g_mix, w1, g_q, wqn, wqa, wqb, g_kv, wk, wv, cos_t, sin_t):
    b, s, d = x.shape
    ts = min(TS_PROJ, s)

    def full(arr):
        return pl.BlockSpec(arr.shape, lambda bi, si: (0,) * arr.ndim)

    def tok(width):
        return pl.BlockSpec((1, ts, width), lambda bi, si: (bi, si, 0))

    tab = pl.BlockSpec((ts, LANES), lambda bi, si: (si, 0))
    pair_w = 2 * LANES * (MLA_HEADS // 2)
    out_shapes = (
        jax.ShapeDtypeStruct((b, s, FOX_WIDTH), BF16),
        jax.ShapeDtypeStruct((b, s, FOX_WIDTH), BF16),
        jax.ShapeDtypeStruct((b, s, FOX_WIDTH), BF16),
        jax.ShapeDtypeStruct((b, s, LANES), F32),
        jax.ShapeDtypeStruct((b, s, pair_w), BF16),
        jax.ShapeDtypeStruct((b, s, pair_w), BF16),
        jax.ShapeDtypeStruct((b, s, MLA_WIDTH), BF16),
    )
    return pl.pallas_call(
        _inproj_kernel,
        grid=(b, s // ts),
        in_specs=[tok(d), full(g_mix), full(w1), full(g_q), full(wqn), full(wqa), full(wqb), full(g_kv), full(wk),
                  full(wv), tab, tab],
        out_specs=(tok(FOX_WIDTH), tok(FOX_WIDTH), tok(FOX_WIDTH), tok(LANES), tok(pair_w), tok(pair_w),
                   tok(MLA_WIDTH)),
        out_shape=out_shapes,
        compiler_params=_cparams(2),
        name="inproj",
    )(x, g_mix, w1, g_q, wqn, wqa, wqb, g_kv, wk, wv, cos_t, sin_t)


def _decay_kernel(fl_ref, bf_ref, pq_ref, pk_ref, oq_ref, ok_ref, cq_ref, ck_ref, *, blk):
    s = fl_ref.shape[1]
    row = lax.broadcasted_iota(I32, (blk, blk), 0)
    col = lax.broadcasted_iota(I32, (blk, blk), 1)
    tri = jnp.where(row >= col, 1.0, 0.0).astype(BF16)
    carry = jnp.zeros((1, LANES), F32)
    for i in range(s // blk):
        sl = slice(i * blk, (i + 1) * blk)
        z = fl_ref[0, sl, :] + bf_ref[...]
        lf = (jnp.minimum(z, 0.0) - jnp.log1p(jnp.exp(-jnp.abs(z)))) * LOG2E
        h, m, l = _split3(lf)
        cs = _dot(tri, h) + _dot(tri, m) + _dot(tri, l) + carry
        carry = cs[blk - 1:blk, :]
        h, m, l = _split3(cs)
        cq = _dot(h, pq_ref[0]) + _dot(m, pq_ref[1]) + _dot(l, pq_ref[2]) + oq_ref[...]
        ck = _dot(h, pk_ref[0]) + _dot(m, pk_ref[1]) + _dot(l, pk_ref[2]) + ok_ref[...]
        cq_ref[0, sl, :] = cq.astype(BF16)
        ck_ref[0, sl, :] = ck.astype(BF16)


def _decay_tables():
    pq = np.zeros((3, LANES, LANES), np.float32)
    pk = np.zeros((3, LANES, LANES), np.float32)
    oq = np.zeros((1, LANES), np.float32)
    ok = np.zeros((1, LANES), np.float32)
    for h in range(FOX_HEADS):
        for part in range(3):
            pq[part, h, FOX_DECAY_LANES * h + part] = 1.0
            pk[part, h, FOX_DECAY_LANES * h + 3 + part] = -1.0
            oq[0, FOX_DECAY_LANES * h + 3 + part] = 1.0
            ok[0, FOX_DECAY_LANES * h + part] = 1.0
    return jnp.asarray(pq, BF16), jnp.asarray(pk, BF16), jnp.asarray(oq), jnp.asarray(ok)


def _decay(flog, b_f):
    b, s, _ = flog.shape
    blk = min(DECAY_BLK, s)
    pq, pk, oq, ok = _decay_tables()
    bf = jnp.zeros((1, LANES), F32).at[0, :FOX_HEADS].set(b_f)

    def full(arr):
        return pl.BlockSpec(arr.shape, lambda bi: (0,) * arr.ndim)

    seq = pl.BlockSpec((1, s, LANES), lambda bi: (bi, 0, 0))
    return pl.pallas_call(
        functools.partial(_decay_kernel, blk=blk),
        grid=(b,),
        in_specs=[seq, full(bf), full(pq), full(pk), full(oq), full(ok)],
        out_specs=(seq, seq),
        out_shape=(jax.ShapeDtypeStruct((b, s, LANES), BF16), jax.ShapeDtypeStruct((b, s, LANES), BF16)),
        compiler_params=_cparams(1),
        name="decay",
    )(flog, bf, pq, pk, oq, ok)


def _attn_pipe_kernel(*refs, fox, t, nq, pairs):
    if fox:
        q_ref, cq_ref, k_ref, ck_ref, v_ref, o_ref, s_even, s_odd = refs
        group = FOX_DECAY_LANES
    else:
        q_ref, k_ref, v_ref, o_ref, s_even, s_odd = refs
        group = MLA_ROPE_DIM
    g = pl.program_id(0)
    i = g % nq
    pair = (jnp.minimum(g, pl.num_programs(0) - 2) // nq) % pairs
    base = 2 * pair if fox else 0
    lane = lax.broadcasted_iota(I32, (1, LANES), 1)

    @pl.when(g == 0)
    def _():
        s_odd[...] = jnp.zeros_like(s_odd)

    def body(iv):
        s_new, s_old = (s_even, s_odd) if iv % 2 == 0 else (s_odd, s_even)
        row = lax.broadcasted_iota(I32, (t, t), 0)
        col = lax.broadcasted_iota(I32, (t, t), 1)
        allowed = (col <= row) if fox else ((col // CHUNK) <= (row // CHUNK))
        past = iv * t
        if fox:
            q_main, q_extra = q_ref[0], cq_ref[0]
        else:
            q_main, q_extra = q_ref[0, :, :LANES], q_ref[0, :, LANES:]
        zero = jnp.zeros_like(q_main)

        def keys(lo, hi):
            if fox:
                return jnp.concatenate([k_ref[0, lo:hi, :], ck_ref[0, lo:hi, :]], axis=1)
            return k_ref[0, lo:hi, :]

        qa = [jnp.concatenate([jnp.where((lane // (LANES // 2)) == hh, q_main, zero),
                               jnp.where((lane // group) == base + hh, q_extra, zero)], axis=1) for hh in range(2)]
        for hh in range(2):
            if iv > 0:
                s_new[hh, :, 0:past] = _dot_nt(qa[hh], keys(0, past))
            s_new[hh, :, past:past + t] = jnp.where(allowed, _dot_nt(qa[hh], keys(past, past + t)), NEG)

        seen = ((iv - 1) % nq + 1) * t
        outs = []
        for hh in range(2):
            sc = s_old[hh, :, 0:seen]
            p = jnp.exp2(sc - jnp.max(sc, axis=-1, keepdims=True))
            l = jnp.sum(p, axis=-1, keepdims=True)
            outs.append(_dot(p.astype(BF16), v_ref[0, 0:seen, :]) * (1.0 / l))
        o_ref[0] = jnp.where(lane < LANES // 2, outs[0], outs[1]).astype(o_ref.dtype)

    for iv in range(nq):
        pl.when(i == iv)(functools.partial(body, iv))


def _attention_pipelined(fox, q, k, v, cq=None, ck=None):
    b, s, _ = v.shape
    t = min(T_ATT, s)
    nq = s // t
    assert nq % 2 == 0
    pairs = v.shape[2] // LANES
    qw = q.shape[2] // pairs
    kw = k.shape[2] // pairs
    items = b * pairs * nq

    def item(g):
        return g // (pairs * nq), (g // nq) % pairs, g % nq

    def cur(g):
        return item(jnp.minimum(g, items - 1))

    def prev(g):
        return item(jnp.maximum(g - 1, 0))

    qspec = pl.BlockSpec((1, t, qw), lambda g: (cur(g)[0], cur(g)[2], cur(g)[1]))
    kspec = pl.BlockSpec((1, s, kw), lambda g: (cur(g)[0], 0, cur(g)[1]))
    vspec = pl.BlockSpec((1, s, LANES), lambda g: (prev(g)[0], 0, prev(g)[1]))
    ospec = pl.BlockSpec((1, t, LANES), lambda g: (prev(g)[0], prev(g)[2], prev(g)[1]))
    if fox:
        cqspec = pl.BlockSpec((1, t, LANES), lambda g: (cur(g)[0], cur(g)[2], 0))
        ckspec = pl.BlockSpec((1, s, LANES), lambda g: (cur(g)[0], 0, 0))
        in_specs = [qspec, cqspec, kspec, ckspec, vspec]
        args = (q, cq, k, ck, v)
    else:
        in_specs = [qspec, kspec, vspec]
        args = (q, k, v)
    return pl.pallas_call(
        functools.partial(_attn_pipe_kernel, fox=fox, t=t, nq=nq, pairs=pairs),
        grid=(items + 1,),
        in_specs=in_specs,
        out_specs=ospec,
        out_shape=jax.ShapeDtypeStruct((b, s, pairs * LANES), BF16),
        scratch_shapes=[pltpu.VMEM((2, t, s), F32), pltpu.VMEM((2, t, s), F32)],
        compiler_params=_cparams(1),
        name="fox_attn" if fox else "mla_attn",
    )(*args)


def _memkv_kernel(mem_ref, g_ref, w_ref, k_ref, v_ref):
    a = _rms(mem_ref[0], g_ref[...]).astype(BF16)
    k_ref[0] = _dot(a, w_ref[:, 0:MEM_WIDTH]).astype(BF16)
    v_ref[0] = _dot(a, w_ref[:, MEM_WIDTH:2 * MEM_WIDTH]).astype(BF16)


def _memkv(mem, g, w):
    b, m, d = mem.shape
    kv = pl.BlockSpec((1, m, MEM_WIDTH), lambda bi: (bi, 0, 0))
    return pl.pallas_call(
        _memkv_kernel,
        grid=(b,),
        in_specs=[pl.BlockSpec((1, m, d), lambda bi: (bi, 0, 0)), pl.BlockSpec(g.shape, lambda bi: (0, 0)),
                  pl.BlockSpec(w.shape, lambda bi: (0, 0))],
        out_specs=(kv, kv),
        out_shape=(jax.ShapeDtypeStruct((b, m, MEM_WIDTH), BF16), jax.ShapeDtypeStruct((b, m, MEM_WIDTH), BF16)),
        compiler_params=_cparams(1),
        name="memkv",
    )(mem, g, w)


def _store_token_tiles(ref, val):
    rows = val.shape[0]
    for j in range(SUBLANES):
        ref[pl.ds(j, rows, stride=SUBLANES), :] = val[:, j * LANES:(j + 1) * LANES]


def _load_token_tiles(ref, rows):
    return jnp.concatenate([ref[pl.ds(j, rows, stride=SUBLANES), :] for j in range(SUBLANES)], axis=1)


def _memrouter_kernel(of_ref, om_ref, x_ref, gf_ref, gm_ref, wmix_ref, gq_ref, wq_ref, k_ref, v_ref, wo_ref,
                      gffn_ref, wrh_ref, wrl_ref, br_ref, h2_ref, at_ref, meta_ref, cnt_ref, carry_ref, *, ts):
    first = jnp.logical_and(pl.program_id(0) == 0, pl.program_id(1) == 0)

    @pl.when(first)
    def _():
        carry_ref[...] = jnp.zeros_like(carry_ref)

    nf = _rms(of_ref[0].astype(F32), gf_ref[...]).astype(BF16)
    nm = _rms(om_ref[0].astype(F32), gm_ref[...]).astype(BF16)
    h1 = x_ref[0] + _dot(nf, wmix_ref[0:FOX_WIDTH, :]) + _dot(nm, wmix_ref[FOX_WIDTH:FOX_WIDTH + MLA_WIDTH, :])

    q = (_dot(_rms(h1, gq_ref[...]).astype(BF16), wq_ref[...]) * MEM_QSCALE).astype(BF16)
    heads = []
    for h in range(MEM_HEADS):
        sl = slice(h * MEM_HEAD_DIM, (h + 1) * MEM_HEAD_DIM)
        sc = _dot_nt(q[:, sl], k_ref[0, :, sl])
        m = jnp.max(sc, axis=-1, keepdims=True)
        pm = jnp.exp2(sc - m)
        l = jnp.sum(pm, axis=-1, keepdims=True)
        heads.append((_dot(pm.astype(BF16), v_ref[0, :, sl]) / l).astype(BF16))
    h2 = h1 + _dot(jnp.concatenate(heads, axis=1), wo_ref[...])
    h2_ref[0] = h2

    a = _rms(h2, gffn_ref[...])
    _store_token_tiles(at_ref, a)
    a_hi = a.astype(BF16)
    a_lo = (a - a_hi.astype(F32)).astype(BF16)
    logits = _dot(a_hi, wrh_ref[...]) + _dot(a_lo, wrh_ref[...]) + _dot(a_hi, wrl_ref[...]) + br_ref[...]
    lane = lax.broadcasted_iota(I32, (ts, LANES), 1)
    work = jnp.where(lane < N_EXPERTS, logits, NEG)
    vals, idxs, sels = [], [], []
    for _ in range(TOP_K):
        mk = jnp.max(work, axis=-1, keepdims=True)
        ik = jnp.min(jnp.where(work == mk, lane, LANES), axis=-1, keepdims=True)
        sel = lane == ik
        work = jnp.where(sel, NEG, work)
        vals.append(mk)
        idxs.append(ik)
        sels.append(sel)
    exps = [jnp.exp(vk - vals[0]) for vk in vals]
    denom = exps[0] + exps[1] + exps[2] + exps[3]
    chosen = jnp.logical_or(jnp.logical_or(sels[0], sels[1]), jnp.logical_or(sels[2], sels[3]))
    onehot = jnp.where(chosen, 1.0, 0.0)
    row = lax.broadcasted_iota(I32, (ts, ts), 0)
    col = lax.broadcasted_iota(I32, (ts, ts), 1)
    tri = jnp.where(row > col, 1.0, 0.0).astype(BF16)
    carry = carry_ref[...]
    rank = _dot(tri, onehot.astype(BF16)) + carry
    carry_new = carry + jnp.sum(onehot, axis=0, keepdims=True)
    carry_ref[...] = carry_new
    cnt_ref[...] = carry_new
    meta = jnp.zeros((ts, LANES), F32)
    for kk in range(TOP_K):
        rk = jnp.sum(jnp.where(sels[kk], rank, 0.0), axis=-1, keepdims=True)
        meta = jnp.where(lane == kk, idxs[kk].astype(F32), meta)
        meta = jnp.where(lane == TOP_K + kk, rk, meta)
        meta = jnp.where(lane == 2 * TOP_K + kk, exps[kk] / denom, meta)
    meta_ref[0] = meta


def _memrouter(o_fox, o_mla, x, g_fox, g_mla, w_mix, g_mq, w_mq, kmem, vmem, w_mo, g_ffn, wr_hi, wr_lo, b_r):
    b, s, d = x.shape
    m = kmem.shape[1]
    ts = min(TS_MEM, s)

    def full(arr):
        return pl.BlockSpec(arr.shape, lambda bi, si: (0,) * arr.ndim)

    def tok(width):
        return pl.BlockSpec((1, ts, width), lambda bi, si: (bi, si, 0))

    kv = pl.BlockSpec((1, m, MEM_WIDTH), lambda bi, si: (bi, 0, 0))
    cnt = pl.BlockSpec((1, LANES), lambda bi, si: (0, 0))
    return pl.pallas_call(
        functools.partial(_memrouter_kernel, ts=ts),
        grid=(b, s // ts),
        in_specs=[tok(FOX_WIDTH), tok(MLA_WIDTH), tok(d), full(g_fox), full(g_mla), full(w_mix), full(g_mq),
                  full(w_mq), kv, kv, full(w_mo), full(g_ffn), full(wr_hi), full(wr_lo), full(b_r)],
        out_specs=(tok(d), pl.BlockSpec((ts * SUBLANES, LANES), lambda bi, si: (bi * (s // ts) + si, 0)),
                   tok(LANES), cnt),
        out_shape=(jax.ShapeDtypeStruct((b, s, d), F32), jax.ShapeDtypeStruct((b * s * SUBLANES, LANES), F32),
                   jax.ShapeDtypeStruct((b, s, LANES), F32), jax.ShapeDtypeStruct((1, LANES), F32)),
        scratch_shapes=[pltpu.VMEM((1, LANES), F32)],
        compiler_params=_cparams(2),
        name="memrouter",
    )(o_fox, o_mla, x, g_fox, g_mla, w_mix, g_mq, w_mq, kmem, vmem, w_mo, g_ffn, wr_hi, wr_lo, b_r)


def _tile_copy(src_ref, src_row8, dst_ref, dst_row8, sem):
    return pltpu.make_async_copy(src_ref.at[pl.ds(pl.multiple_of(src_row8, SUBLANES), SUBLANES), :],
                                 dst_ref.at[pl.ds(pl.multiple_of(dst_row8, SUBLANES), SUBLANES), :], sem)


def _dispatch_kernel(last_ref, cnt_ref, nv_ref, pos_ref, at_ref, xs_ref, zbuf, zsem, sem, *, td, tm, min_used,
                     n_tiles):
    @pl.when(pl.program_id(0) == 0)
    def _():
        zbuf[...] = jnp.zeros_like(zbuf)

        def fill(row):
            return pltpu.make_async_copy(
                zbuf, xs_ref.at[pl.ds(pl.multiple_of(row * SUBLANES, SUBLANES), tm * SUBLANES), :], zsem)

        fills = [(cnt_ref[e] > 0, last_ref[e]) for e in range(N_EXPERTS)]
        fills += [(t >= nv_ref[0], t * tm) for t in range(min_used, n_tiles)]
        for pred, row in fills:
            pl.when(pred)(lambda row=row: fill(row).start())
        for pred, row in fills:
            pl.when(pred)(lambda row=row: fill(row).wait())

    def issue(r, c):
        for kk in range(TOP_K):
            _tile_copy(at_ref, r * SUBLANES, xs_ref, pos_ref[r * TOP_K + kk], sem).start(priority=kk % 2)
        return c

    lax.fori_loop(0, td, issue, 0, unroll=4)
    for _ in range(TOP_K):
        pltpu.make_async_copy(at_ref, xs_ref.at[pl.ds(0, td * SUBLANES), :], sem).wait()


def _dispatch(last_tile_row, cnt, nv, pos8, at, n_tiles, tm):
    n = at.shape[0] // SUBLANES
    td = min(TD_ROWS, n)
    grid_spec = pltpu.PrefetchScalarGridSpec(
        num_scalar_prefetch=3,
        grid=(n // td,),
        in_specs=[pl.BlockSpec((td * TOP_K,), lambda i, *_: (i,), memory_space=pltpu.SMEM),
                  pl.BlockSpec((td * SUBLANES, LANES), lambda i, *_: (i, 0))],
        out_specs=pl.BlockSpec(memory_space=pl.ANY),
        scratch_shapes=[pltpu.VMEM((tm * SUBLANES, LANES), F32), pltpu.SemaphoreType.DMA, pltpu.SemaphoreType.DMA],
    )
    min_used = n * TOP_K // tm
    return pl.pallas_call(
        functools.partial(_dispatch_kernel, td=td, tm=tm, min_used=min_used, n_tiles=n_tiles),
        grid_spec=grid_spec,
        out_shape=jax.ShapeDtypeStruct((n_tiles * tm * SUBLANES, LANES), F32),
        compiler_params=_cparams(1),
        name="dispatch",
    )(last_tile_row, cnt, nv, pos8, at)


def _moe_kernel(te_ref, nv_ref, slot_ref, nxt_ref, xs_ref, wgu_hbm, bgu_ref, wd_hbm, bd_ref, y_ref,
                wgu_f32, wd_f32, wgu_bf, wd_bf, sem, *, ch, tm):
    i = pl.program_id(0)
    d_exp = wd_hbm.shape[1]
    valid = i < nv_ref[0]
    new_expert = jnp.logical_or(i == 0, te_ref[i] != te_ref[jnp.maximum(i - 1, 0)])

    def weight_copies(e, sl):
        return (pltpu.make_async_copy(wgu_hbm.at[e], wgu_f32.at[sl], sem.at[sl, 0]),
                pltpu.make_async_copy(wd_hbm.at[e], wd_f32.at[sl], sem.at[sl, 1]))

    @pl.when(i == 0)
    def _():
        for c in weight_copies(te_ref[0], slot_ref[0]):
            c.start()

    @pl.when(jnp.logical_and(new_expert, valid))
    def _():
        sl = slot_ref[i]
        for c in weight_copies(te_ref[i], sl):
            c.wait()
        wgu_bf[...] = wgu_f32[sl].astype(BF16)
        wd_bf[...] = wd_f32[sl].astype(BF16)

        @pl.when(nxt_ref[i] >= 0)
        def _():
            for c in weight_copies(nxt_ref[i], 1 - sl):
                c.start()

    @pl.when(valid)
    def _():
        x = _load_token_tiles(xs_ref, tm).astype(BF16)
        acc = jnp.zeros((tm, wd_hbm.shape[2]), F32)
        for c in range(d_exp // ch):
            def gu(lo):
                return _dot(x, wgu_bf[:, lo:lo + ch]) + bgu_ref[0, :, lo:lo + ch]

            gate = jnp.minimum(gu(c * ch), SWIGLU_LIMIT)
            up = jnp.clip(gu(d_exp + c * ch), -SWIGLU_LIMIT, SWIGLU_LIMIT)
            act = gate * (1.0 / (1.0 + jnp.exp(-SWIGLU_ALPHA * gate))) * (up + 1.0)
            acc = acc + _dot(act.astype(BF16), wd_bf[c * ch:(c + 1) * ch, :])
        _store_token_tiles(y_ref, acc + bd_ref[0])

    @pl.when(jnp.logical_not(valid))
    def _():
        y_ref[...] = jnp.zeros_like(y_ref)


def _moe(te, nv, slot, nxt, xs, wgu, bgu, wd, bd, tm):
    n_tiles = xs.shape[0] // (tm * SUBLANES)
    _, d, d_exp2 = wgu.shape
    d_exp = wd.shape[1]
    assert d == SUBLANES * LANES and wd.shape[2] == d
    tile = pl.BlockSpec((tm * SUBLANES, LANES), lambda i, te, *_: (i, 0))
    grid_spec = pltpu.PrefetchScalarGridSpec(
        num_scalar_prefetch=4,
        grid=(n_tiles,),
        in_specs=[
            tile,
            pl.BlockSpec(memory_space=pl.ANY),
            pl.BlockSpec((1, 1, d_exp2), lambda i, te, *_: (te[i], 0, 0)),
            pl.BlockSpec(memory_space=pl.ANY),
            pl.BlockSpec((1, 1, d), lambda i, te, *_: (te[i], 0, 0)),
        ],
        out_specs=tile,
        scratch_shapes=[pltpu.VMEM((2, d, d_exp2), F32), pltpu.VMEM((2, d_exp, d), F32),
                        pltpu.VMEM((d, d_exp2), BF16), pltpu.VMEM((d_exp, d), BF16),
                        pltpu.SemaphoreType.DMA((2, 2))],
    )
    return pl.pallas_call(
        functools.partial(_moe_kernel, ch=min(MOE_CH, d_exp), tm=tm),
        grid_spec=grid_spec,
        out_shape=jax.ShapeDtypeStruct(xs.shape, F32),
        compiler_params=_cparams(1),
        name="moe",
    )(te, nv, slot, nxt, xs, wgu, bgu, wd, bd)


def _combine_kernel(pos_ref, posn_ref, h2_ref, meta_ref, y_ref, gf_ref, o_ref, ybuf, sem, *, td, final_norm):
    i = pl.program_id(0)
    slot = i % 2

    def issue_block(p_ref, sl):
        def issue(r, c):
            for kk in range(TOP_K):
                _tile_copy(y_ref, p_ref[r * TOP_K + kk], ybuf.at[sl, kk], r * SUBLANES,
                           sem.at[sl]).start(priority=kk % 2)
            return c

        lax.fori_loop(0, td, issue, 0, unroll=4)

    @pl.when(i == 0)
    def _():
        issue_block(pos_ref, 0)

    @pl.when(i + 1 < pl.num_programs(0))
    def _():
        issue_block(posn_ref, 1 - slot)

    for kk in range(TOP_K):
        pltpu.make_async_copy(y_ref.at[pl.ds(0, td * SUBLANES), :], ybuf.at[slot, kk], sem.at[slot]).wait()

    meta = meta_ref[...]
    acc = h2_ref[...]
    for kk in range(TOP_K):
        gate = meta[:, 2 * TOP_K + kk:2 * TOP_K + kk + 1]
        acc = acc + gate * _load_token_tiles(ybuf.at[slot, kk], td)
    o_ref[...] = _rms(acc, gf_ref[...]) if final_norm else acc


def _combine(pos8, h2, meta, y, g_final, final_norm):
    n, d = h2.shape
    td = min(TD_ROWS, n)
    n_blocks = n // td
    grid_spec = pltpu.PrefetchScalarGridSpec(
        num_scalar_prefetch=0,
        grid=(n_blocks,),
        in_specs=[pl.BlockSpec((td * TOP_K,), lambda i: (i,), memory_space=pltpu.SMEM),
                  pl.BlockSpec((td * TOP_K,), lambda i: (jnp.minimum(i + 1, n_blocks - 1),), memory_space=pltpu.SMEM),
                  pl.BlockSpec((td, d), lambda i: (i, 0)),
                  pl.BlockSpec((td, LANES), lambda i: (i, 0)),
                  pl.BlockSpec(memory_space=pl.ANY),
                  pl.BlockSpec(g_final.shape, lambda i: (0, 0))],
        out_specs=pl.BlockSpec((td, d), lambda i: (i, 0)),
        scratch_shapes=[pltpu.VMEM((2, TOP_K, td * SUBLANES, LANES), F32), pltpu.SemaphoreType.DMA((2,))],
    )
    return pl.pallas_call(
        functools.partial(_combine_kernel, td=td, final_norm=final_norm),
        grid_spec=grid_spec,
        out_shape=jax.ShapeDtypeStruct((n, d), F32),
        compiler_params=_cparams(1),
        name="combine",
    )(pos8, pos8, h2, meta, y, g_final)


def _rope_tables(seq_len):
    inv = 1.0 / (ROPE_BASE ** (jnp.arange(0, MLA_ROPE_DIM, 2, dtype=F32) / MLA_ROPE_DIM))
    ang = jnp.arange(seq_len, dtype=F32)[:, None] * inv[None, :]
    cos, sin = jnp.cos(ang), jnp.sin(ang)
    pad = jnp.zeros((seq_len, LANES - 2 * MLA_ROPE_DIM), F32)
    cos_t = jnp.concatenate([cos, cos, cos, cos, pad], axis=1)
    sin_t = jnp.concatenate([sin, sin, sin, sin, pad], axis=1)
    return cos_t, sin_t


def _rot_cols(w):
    half = MLA_ROPE_DIM // 2
    return jnp.concatenate([-w[:, half:], w[:, :half]], axis=1)


def _rope_pair_block(r0, r1):
    pad = jnp.zeros((r0.shape[0], LANES - 2 * MLA_ROPE_DIM), F32)
    return jnp.concatenate([r0, r1, pad], axis=1)


def _prep_inproj_weights(w_in, w_uq, w_ukv):
    d = w_in.shape[0]
    pts = np.cumsum((FOX_WIDTH, FOX_WIDTH, FOX_WIDTH, FOX_HEADS, MLA_Q_LORA, MLA_KV_LORA, MLA_ROPE_DIM))
    w_fq, w_fk, w_fv = w_in[:, :pts[0]], w_in[:, pts[0]:pts[1]], w_in[:, pts[1]:pts[2]]
    w_fl, w_ql = w_in[:, pts[2]:pts[3]], w_in[:, pts[3]:pts[4]]
    w_kvl, w_kr = w_in[:, pts[4]:pts[5]], w_in[:, pts[5]:pts[6]]
    misc = jnp.concatenate([w_fl, jnp.zeros((d, LANES - FOX_HEADS), F32)], axis=1)
    w1 = jnp.concatenate([w_fq, w_fk, w_fv, w_ql, w_kvl, misc, _rope_pair_block(w_kr, w_kr),
                          _rope_pair_block(_rot_cols(w_kr), _rot_cols(w_kr))], axis=1).astype(BF16)
    qd = MLA_NOPE_DIM + MLA_ROPE_DIM
    kvd = MLA_NOPE_DIM + MLA_V_DIM
    q_nope = [w_uq[:, h * qd:h * qd + MLA_NOPE_DIM] for h in range(MLA_HEADS)]
    q_rope = [w_uq[:, h * qd + MLA_NOPE_DIM:(h + 1) * qd] for h in range(MLA_HEADS)]
    wqa = [_rope_pair_block(q_rope[h], q_rope[h + 1]) for h in range(0, MLA_HEADS, 2)]
    wqb = [_rope_pair_block(_rot_cols(q_rope[h]), _rot_cols(q_rope[h + 1])) for h in range(0, MLA_HEADS, 2)]
    wk = [w_ukv[:, h * kvd:h * kvd + MLA_NOPE_DIM] for h in range(MLA_HEADS)]
    wv = [w_ukv[:, h * kvd + MLA_NOPE_DIM:(h + 1) * kvd] for h in range(MLA_HEADS)]
    cat = lambda xs: jnp.concatenate(xs, axis=1).astype(BF16)
    return w1, cat(q_nope), cat(wqa), cat(wqb), cat(wk), cat(wv)


def _routing_tables(counts, n_tiles, tm):
    tiles_e = (counts + tm - 1) // tm
    tile_end = jnp.cumsum(tiles_e)
    off = ((tile_end - tiles_e) * tm).astype(I32)
    total = tile_end[-1]
    ti = jnp.arange(n_tiles, dtype=I32)
    te = jnp.minimum(jnp.sum(ti[:, None] >= tile_end[None, :], axis=1), N_EXPERTS - 1).astype(I32)
    valid = ti < total
    last_e = jnp.sum(jnp.where(ti == total - 1, te, 0))
    te = jnp.where(valid, te, last_e).astype(I32)
    last_tile_row = (off + (tiles_e - 1) * tm).astype(I32)
    used = tiles_e > 0
    ids = jnp.arange(N_EXPERTS, dtype=I32)
    slot_e = (jnp.cumsum(used.astype(I32)) - 1) % 2
    later = jnp.where(used[None, :] & (ids[None, :] > ids[:, None]), ids[None, :], N_EXPERTS)
    nxt_e = jnp.min(later, axis=1)
    nxt_e = jnp.where(nxt_e == N_EXPERTS, -1, nxt_e).astype(I32)
    pick = te[:, None] == ids[None, :]
    slot = jnp.sum(jnp.where(pick, slot_e[None, :], 0), axis=1).astype(I32)
    nxt = jnp.sum(jnp.where(pick, nxt_e[None, :], 0), axis=1).astype(I32)
    return off, last_tile_row, te, total.astype(I32).reshape(1), slot, nxt


def kernel(x, mem, g_mix, w_in, b_f, g_q_lat, w_uq, g_kv_lat, w_ukv, g_fox_out, g_mla_out, w_o, g_mem_q, w_mem_q,
           g_mem_kv, w_mem_kv, w_mem_o, g_ffn, w_router, b_router, w_gate_up, b_gate_up, w_down, b_down, g_final):
    b, s, d = x.shape
    n = b * s
    depth = g_mix.shape[0]
    cos_t, sin_t = _rope_tables(s)
    row = lambda v: v.reshape(1, -1)
    h = x
    for l in range(depth):
        w1, wqn, wqa, wqb, wk, wv = _prep_inproj_weights(w_in[l], w_uq[l], w_ukv[l])
        fq, fk, fv, flog, qm, km, vm = _inproj(h, row(g_mix[l]), w1, row(g_q_lat[l]), wqn, wqa, wqb,
                                                row(g_kv_lat[l]), wk, wv, cos_t, sin_t)
        cq, ck = _decay(flog, b_f[l])
        o_fox = _attention_pipelined(True, fq, fk, fv, cq, ck)
        o_mla = _attention_pipelined(False, qm, km, vm)
        kmem, vmem = _memkv(mem, row(g_mem_kv[l]), w_mem_kv[l].astype(BF16))
        wr = jnp.zeros((d, LANES), F32).at[:, :N_EXPERTS].set(w_router[l])
        wr_hi = wr.astype(BF16)
        wr_lo = (wr - wr_hi.astype(F32)).astype(BF16)
        br = jnp.zeros((1, LANES), F32).at[0, :N_EXPERTS].set(b_router[l])
        h2, at, meta, cnt = _memrouter(o_fox, o_mla, h, row(g_fox_out[l]), row(g_mla_out[l]), w_o[l].astype(BF16),
                                       row(g_mem_q[l]), w_mem_q[l].astype(BF16), kmem, vmem,
                                       w_mem_o[l].astype(BF16), row(g_ffn[l]), wr_hi, wr_lo, br)
        meta2 = meta.reshape(n, LANES)
        ek = meta2[:, 0:TOP_K].astype(I32)
        rk = meta2[:, TOP_K:2 * TOP_K].astype(I32)
        n_tiles = n * TOP_K // TM_MOE + N_EXPERTS
        counts = cnt[0, :N_EXPERTS].astype(I32)
        off, last_tile_row, te, nv, slot, nxt = _routing_tables(counts, n_tiles, TM_MOE)
        off_of = jnp.sum(jnp.where(ek[..., None] == jnp.arange(N_EXPERTS, dtype=I32), off, 0), axis=-1)
        pos8 = ((off_of + rk) * SUBLANES).astype(I32).reshape(-1)
        xs = _dispatch(last_tile_row, counts, nv, pos8, at, n_tiles, TM_MOE)
        y = _moe(te, nv, slot, nxt, xs, w_gate_up[l], b_gate_up[l].reshape(N_EXPERTS, 1, -1), w_down[l],
                 b_down[l].reshape(N_EXPERTS, 1, -1), TM_MOE)
        h = _combine(pos8, h2.reshape(n, d), meta2, y, row(g_final), l == depth - 1).reshape(b, s, d)
    return h
```

```python
import functools

import numpy as np
import jax
import jax.numpy as jnp
from jax import lax
from jax.experimental import pallas as pl
from jax.experimental.pallas import tpu as pltpu

F32 = jnp.float32
BF16 = jnp.bfloat16
I32 = jnp.int32

LANES = 128
SUBLANES = 8
RMS_EPS = 1e-6
CHUNK = 64
FOX_HEADS = 8
FOX_HEAD_DIM = 64
FOX_WIDTH = FOX_HEADS * FOX_HEAD_DIM
MLA_HEADS = 8
MLA_Q_LORA = 384
MLA_KV_LORA = 256
MLA_NOPE_DIM = 64
MLA_ROPE_DIM = 32
MLA_V_DIM = 64
MLA_WIDTH = MLA_HEADS * MLA_V_DIM
ROPE_BASE = 10000.0
MEM_HEADS = 4
MEM_HEAD_DIM = 128
MEM_WIDTH = MEM_HEADS * MEM_HEAD_DIM
N_EXPERTS = 32
TOP_K = 4
SWIGLU_LIMIT = 7.0
SWIGLU_ALPHA = 1.702

LOG2E = 1.4426950408889634
NEG = -1e30
FOX_QSCALE = FOX_HEAD_DIM ** -0.5 * LOG2E
MLA_QSCALE = (MLA_NOPE_DIM + MLA_ROPE_DIM) ** -0.5 * LOG2E
MEM_QSCALE = MEM_HEAD_DIM ** -0.5 * LOG2E
FOX_DECAY_LANES = 16

VMEM_LIMIT = 56 * 1024 * 1024

TS_PROJ = 512
TS_MEM = 512
T_ATT = 512
DECAY_BLK = 256
TM_MOE = 512
TD_ROWS = 256
MOE_CH = 512


def _cparams(n_axes):
    return pltpu.CompilerParams(dimension_semantics=("arbitrary",) * n_axes, vmem_limit_bytes=VMEM_LIMIT)


def _dot(a, b):
    return jnp.dot(a, b, preferred_element_type=F32)


def _dot_nt(a, b):
    return lax.dot_general(a, b, (((1,), (1,)), ((), ())), preferred_element_type=F32)


def _rms(x, g):
    return x * lax.rsqrt(jnp.mean(x * x, axis=-1, keepdims=True) + RMS_EPS) * g


def _split3(x):
    hi = x.astype(BF16)
    r1 = x - hi.astype(F32)
    mid = r1.astype(BF16)
    lo = (r1 - mid.astype(F32)).astype(BF16)
    return hi, mid, lo


def _interleave_blocks(a, b):
    parts = []
    for p in range(a.shape[1] // LANES):
        parts += [a[:, p * LANES:(p + 1) * LANES], b[:, p * LANES:(p + 1) * LANES]]
    return jnp.concatenate(parts, axis=1)


def _inproj_kernel(x_ref, g_ref, w1_ref, gq_ref, wqn_ref, wqa_ref, wqb_ref, gkv_ref, wk_ref, wv_ref, cos_ref, sin_ref,
                   fq_ref, fk_ref, fv_ref, fl_ref, qm_ref, km_ref, vm_ref):
    a = _rms(x_ref[0], g_ref[...]).astype(BF16)

    def proj(lo, hi):
        return _dot(a, w1_ref[:, lo:hi])

    c0 = 3 * FOX_WIDTH
    c1 = c0 + MLA_Q_LORA
    c2 = c1 + MLA_KV_LORA
    fq_ref[0] = (proj(0, FOX_WIDTH) * FOX_QSCALE).astype(BF16)
    fk_ref[0] = proj(FOX_WIDTH, 2 * FOX_WIDTH).astype(BF16)
    fv_ref[0] = proj(2 * FOX_WIDTH, c0).astype(BF16)
    qn = _rms(proj(c0, c1), gq_ref[...]).astype(BF16)
    kvn = _rms(proj(c1, c2), gkv_ref[...]).astype(BF16)
    fl_ref[0] = proj(c2, c2 + LANES)
    cos = cos_ref[...]
    sin = sin_ref[...]
    pairs = MLA_HEADS // 2
    kpe = (proj(c2 + LANES, c2 + 2 * LANES) * cos + proj(c2 + 2 * LANES, c2 + 3 * LANES) * sin).astype(BF16)
    cos4 = jnp.concatenate([cos] * pairs, axis=1)
    sin4 = jnp.concatenate([sin] * pairs, axis=1)
    q_nope = (_dot(qn, wqn_ref[...]) * MLA_QSCALE).astype(BF16)
    q_rope = ((_dot(qn, wqa_ref[...]) * cos4 + _dot(qn, wqb_ref[...]) * sin4) * MLA_QSCALE).astype(BF16)
    qm_ref[0] = _interleave_blocks(q_nope, q_rope)
    k_nope = _dot(kvn, wk_ref[...]).astype(BF16)
    km_ref[0] = _interleave_blocks(k_nope, jnp.concatenate([kpe] * pairs, axis=1))
    vm_ref[0] = _dot(kvn, wv_ref[...]).astype(BF16)


def _inproj(x, g_mix, w1, g_q, wqn, wqa, wqb, g_kv, wk, wv, cos_t, sin_t):
    b, s, d = x.shape
    ts = min(TS_PROJ, s)

    def full(arr):
        return pl.BlockSpec(arr.shape, lambda bi, si: (0,) * arr.ndim)

    def tok(width):
        return pl.BlockSpec((1, ts, width), lambda bi, si: (bi, si, 0))

    tab = pl.BlockSpec((ts, LANES), lambda bi, si: (si, 0))
    pair_w = 2 * LANES * (MLA_HEADS // 2)
    out_shapes = (
        jax.ShapeDtypeStruct((b, s, FOX_WIDTH), BF16),
        jax.ShapeDtypeStruct((b, s, FOX_WIDTH), BF16),
        jax.ShapeDtypeStruct((b, s, FOX_WIDTH), BF16),
        jax.ShapeDtypeStruct((b, s, LANES), F32),
        jax.ShapeDtypeStruct((b, s, pair_w), BF16),
        jax.ShapeDtypeStruct((b, s, pair_w), BF16),
        jax.ShapeDtypeStruct((b, s, MLA_WIDTH), BF16),
    )
    return pl.pallas_call(
        _inproj_kernel,
        grid=(b, s // ts),
        in_specs=[tok(d), full(g_mix), full(w1), full(g_q), full(wqn), full(wqa), full(wqb), full(g_kv), full(wk),
                  full(wv), tab, tab],
        out_specs=(tok(FOX_WIDTH), tok(FOX_WIDTH), tok(FOX_WIDTH), tok(LANES), tok(pair_w), tok(pair_w),
                   tok(MLA_WIDTH)),
        out_shape=out_shapes,
        compiler_params=_cparams(2),
        name="inproj",
    )(x, g_mix, w1, g_q, wqn, wqa, wqb, g_kv, wk, wv, cos_t, sin_t)


def _decay_kernel(fl_ref, bf_ref, pq_ref, pk_ref, oq_ref, ok_ref, cq_ref, ck_ref, *, blk):
    s = fl_ref.shape[1]
    row = lax.broadcasted_iota(I32, (blk, blk), 0)
    col = lax.broadcasted_iota(I32, (blk, blk), 1)
    tri = jnp.where(row >= col, 1.0, 0.0).astype(BF16)
    carry = jnp.zeros((1, LANES), F32)
    for i in range(s // blk):
        sl = slice(i * blk, (i + 1) * blk)
        z = fl_ref[0, sl, :] + bf_ref[...]
        lf = (jnp.minimum(z, 0.0) - jnp.log1p(jnp.exp(-jnp.abs(z)))) * LOG2E
        h, m, l = _split3(lf)
        cs = _dot(tri, h) + _dot(tri, m) + _dot(tri, l) + carry
        carry = cs[blk - 1:blk, :]
        h, m, l = _split3(cs)
        cq = _dot(h, pq_ref[0]) + _dot(m, pq_ref[1]) + _dot(l, pq_ref[2]) + oq_ref[...]
        ck = _dot(h, pk_ref[0]) + _dot(m, pk_ref[1]) + _dot(l, pk_ref[2]) + ok_ref[...]
        cq_ref[0, sl, :] = cq.astype(BF16)
        ck_ref[0, sl, :] = ck.astype(BF16)


def _decay_tables():
    pq = np.zeros((3, LANES, LANES), np.float32)
    pk = np.zeros((3, LANES, LANES), np.float32)
    oq = np.zeros((1, LANES), np.float32)
    ok = np.zeros((1, LANES), np.float32)
    for h in range(FOX_HEADS):
        for part in range(3):
            pq[part, h, FOX_DECAY_LANES * h + part] = 1.0
            pk[part, h, FOX_DECAY_LANES * h + 3 + part] = -1.0
            oq[0, FOX_DECAY_LANES * h + 3 + part] = 1.0
            ok[0, FOX_DECAY_LANES * h + part] = 1.0
    return jnp.asarray(pq, BF16), jnp.asarray(pk, BF16), jnp.asarray(oq), jnp.asarray(ok)


def _decay(flog, b_f):
    b, s, _ = flog.shape
    blk = min(DECAY_BLK, s)
    pq, pk, oq, ok = _decay_tables()
    bf = jnp.zeros((1, LANES), F32).at[0, :FOX_HEADS].set(b_f)

    def full(arr):
        return pl.BlockSpec(arr.shape, lambda bi: (0,) * arr.ndim)

    seq = pl.BlockSpec((1, s, LANES), lambda bi: (bi, 0, 0))
    return pl.pallas_call(
        functools.partial(_decay_kernel, blk=blk),
        grid=(b,),
        in_specs=[seq, full(bf), full(pq), full(pk), full(oq), full(ok)],
        out_specs=(seq, seq),
        out_shape=(jax.ShapeDtypeStruct((b, s, LANES), BF16), jax.ShapeDtypeStruct((b, s, LANES), BF16)),
        compiler_params=_cparams(1),
        name="decay",
    )(flog, bf, pq, pk, oq, ok)


def _attn_pipe_kernel(*refs, fox, t, nq, pairs):
    if fox:
        q_ref, cq_ref, k_ref, ck_ref, v_ref, o_ref, s_even, s_odd = refs
        group = FOX_DECAY_LANES
    else:
        q_ref, k_ref, v_ref, o_ref, s_even, s_odd = refs
        group = MLA_ROPE_DIM
    g = pl.program_id(0)
    i = g % nq
    pair = (jnp.minimum(g, pl.num_programs(0) - 2) // nq) % pairs
    base = 2 * pair if fox else 0
    lane = lax.broadcasted_iota(I32, (1, LANES), 1)

    @pl.when(g == 0)
    def _():
        s_odd[...] = jnp.zeros_like(s_odd)

    def body(iv):
        s_new, s_old = (s_even, s_odd) if iv % 2 == 0 else (s_odd, s_even)
        half = t // 2
        row = lax.broadcasted_iota(I32, (half, half), 0)
        col = lax.broadcasted_iota(I32, (half, half), 1)
        allowed = (col <= row) if fox else ((col // CHUNK) <= (row // CHUNK))
        past = iv * t
        if fox:
            q_main, q_extra = q_ref[0], cq_ref[0]
        else:
            q_main, q_extra = q_ref[0, :, :LANES], q_ref[0, :, LANES:]
        zero = jnp.zeros_like(q_main)

        def keys(lo, hi):
            if fox:
                return jnp.concatenate([k_ref[0, lo:hi, :], ck_ref[0, lo:hi, :]], axis=1)
            return k_ref[0, lo:hi, :]

        qa = [jnp.concatenate([jnp.where((lane // (LANES // 2)) == hh, q_main, zero),
                               jnp.where((lane // group) == base + hh, q_extra, zero)], axis=1) for hh in range(2)]
        for hh in range(2):
            if iv > 0:
                s_new[hh, :, 0:past] = _dot_nt(qa[hh], keys(0, past))
            upper = _dot_nt(qa[hh][:half], keys(past, past + half))
            s_new[hh, :half, past:past + half] = jnp.where(allowed, upper, NEG)
            lower = _dot_nt(qa[hh][half:], keys(past, past + t))
            s_new[hh, half:, past:past + half] = lower[:, :half]
            s_new[hh, half:, past + half:past + t] = jnp.where(allowed, lower[:, half:], NEG)

        seen = ((iv - 1) % nq + 1) * t
        outs = []
        for hh in range(2):
            sc = s_old[hh, :, 0:seen - half]
            sc_tail = s_old[hh, half:, seen - half:seen]
            m_main = jnp.max(sc, axis=-1, keepdims=True)
            m_low = jnp.maximum(m_main[half:], jnp.max(sc_tail, axis=-1, keepdims=True))
            p = jnp.exp2(sc - jnp.concatenate([m_main[:half], m_low], axis=0))
            p_tail = jnp.exp2(sc_tail - m_low)
            l_main = jnp.sum(p, axis=-1, keepdims=True)
            l = jnp.concatenate([l_main[:half], l_main[half:] + jnp.sum(p_tail, axis=-1, keepdims=True)], axis=0)
            acc = _dot(p.astype(BF16), v_ref[0, 0:seen - half, :])
            acc_tail = _dot(p_tail.astype(BF16), v_ref[0, seen - half:seen, :])
            acc = jnp.concatenate([acc[:half], acc[half:] + acc_tail], axis=0)
            outs.append(acc * (1.0 / l))
        o_ref[0] = jnp.where(lane < LANES // 2, outs[0], outs[1]).astype(o_ref.dtype)

    for iv in range(nq):
        pl.when(i == iv)(functools.partial(body, iv))


def _attention_pipelined(fox, q, k, v, cq=None, ck=None):
    b, s, _ = v.shape
    t = min(T_ATT, s)
    nq = s // t
    assert nq % 2 == 0
    pairs = v.shape[2] // LANES
    qw = q.shape[2] // pairs
    kw = k.shape[2] // pairs
    items = b * pairs * nq

    def item(g):
        return g // (pairs * nq), (g // nq) % pairs, g % nq

    def cur(g):
        return item(jnp.minimum(g, items - 1))

    def prev(g):
        return item(jnp.maximum(g - 1, 0))

    qspec = pl.BlockSpec((1, t, qw), lambda g: (cur(g)[0], cur(g)[2], cur(g)[1]))
    kspec = pl.BlockSpec((1, s, kw), lambda g: (cur(g)[0], 0, cur(g)[1]))
    vspec = pl.BlockSpec((1, s, LANES), lambda g: (prev(g)[0], 0, prev(g)[1]))
    ospec = pl.BlockSpec((1, t, LANES), lambda g: (prev(g)[0], prev(g)[2], prev(g)[1]))
    if fox:
        cqspec = pl.BlockSpec((1, t, LANES), lambda g: (cur(g)[0], cur(g)[2], 0))
        ckspec = pl.BlockSpec((1, s, LANES), lambda g: (cur(g)[0], 0, 0))
        in_specs = [qspec, cqspec, kspec, ckspec, vspec]
        args = (q, cq, k, ck, v)
    else:
        in_specs = [qspec, kspec, vspec]
        args = (q, k, v)
    return pl.pallas_call(
        functools.partial(_attn_pipe_kernel, fox=fox, t=t, nq=nq, pairs=pairs),
        grid=(items + 1,),
        in_specs=in_specs,
        out_specs=ospec,
        out_shape=jax.ShapeDtypeStruct((b, s, pairs * LANES), BF16),
        scratch_shapes=[pltpu.VMEM((2, t, s), F32), pltpu.VMEM((2, t, s), F32)],
        compiler_params=_cparams(1),
        name="fox_attn" if fox else "mla_attn",
    )(*args)


def _memkv_kernel(mem_ref, g_ref, w_ref, k_ref, v_ref):
    a = _rms(mem_ref[0], g_ref[...]).astype(BF16)
    k_ref[0] = _dot(a, w_ref[:, 0:MEM_WIDTH]).astype(BF16)
    v_ref[0] = _dot(a, w_ref[:, MEM_WIDTH:2 * MEM_WIDTH]).astype(BF16)


def _memkv(mem, g, w):
    b, m, d = mem.shape
    kv = pl.BlockSpec((1, m, MEM_WIDTH), lambda bi: (bi, 0, 0))
    return pl.pallas_call(
        _memkv_kernel,
        grid=(b,),
        in_specs=[pl.BlockSpec((1, m, d), lambda bi: (bi, 0, 0)), pl.BlockSpec(g.shape, lambda bi: (0, 0)),
                  pl.BlockSpec(w.shape, lambda bi: (0, 0))],
        out_specs=(kv, kv),
        out_shape=(jax.ShapeDtypeStruct((b, m, MEM_WIDTH), BF16), jax.ShapeDtypeStruct((b, m, MEM_WIDTH), BF16)),
        compiler_params=_cparams(1),
        name="memkv",
    )(mem, g, w)


def _store_token_tiles(ref, val):
    rows = val.shape[0]
    for j in range(SUBLANES):
        ref[pl.ds(j, rows, stride=SUBLANES), :] = val[:, j * LANES:(j + 1) * LANES]


def _load_token_tiles(ref, rows):
    return jnp.concatenate([ref[pl.ds(j, rows, stride=SUBLANES), :] for j in range(SUBLANES)], axis=1)


def _memrouter_kernel(of_ref, om_ref, x_ref, gf_ref, gm_ref, wmix_ref, gq_ref, wq_ref, k_ref, v_ref, wo_ref,
                      gffn_ref, wrh_ref, wrl_ref, br_ref, h2_ref, at_ref, meta_ref, cnt_ref, carry_ref, *, ts):
    first = jnp.logical_and(pl.program_id(0) == 0, pl.program_id(1) == 0)

    @pl.when(first)
    def _():
        carry_ref[...] = jnp.zeros_like(carry_ref)

    nf = _rms(of_ref[0].astype(F32), gf_ref[...]).astype(BF16)
    nm = _rms(om_ref[0].astype(F32), gm_ref[...]).astype(BF16)
    h1 = x_ref[0] + _dot(nf, wmix_ref[0:FOX_WIDTH, :]) + _dot(nm, wmix_ref[FOX_WIDTH:FOX_WIDTH + MLA_WIDTH, :])

    q = (_dot(_rms(h1, gq_ref[...]).astype(BF16), wq_ref[...]) * MEM_QSCALE).astype(BF16)
    heads = []
    for h in range(MEM_HEADS):
        sl = slice(h * MEM_HEAD_DIM, (h + 1) * MEM_HEAD_DIM)
        sc = _dot_nt(q[:, sl], k_ref[0, :, sl])
        m = jnp.max(sc, axis=-1, keepdims=True)
        pm = jnp.exp2(sc - m)
        l = jnp.sum(pm, axis=-1, keepdims=True)
        heads.append((_dot(pm.astype(BF16), v_ref[0, :, sl]) / l).astype(BF16))
    h2 = h1 + _dot(jnp.concatenate(heads, axis=1), wo_ref[...])
    h2_ref[0] = h2

    a = _rms(h2, gffn_ref[...])
    _store_token_tiles(at_ref, a)
    a_hi = a.astype(BF16)
    a_lo = (a - a_hi.astype(F32)).astype(BF16)
    logits = _dot(a_hi, wrh_ref[...]) + _dot(a_lo, wrh_ref[...]) + _dot(a_hi, wrl_ref[...]) + br_ref[...]
    lane = lax.broadcasted_iota(I32, (ts, LANES), 1)
    work = jnp.where(lane < N_EXPERTS, logits, NEG)
    vals, idxs, sels = [], [], []
    for _ in range(TOP_K):
        mk = jnp.max(work, axis=-1, keepdims=True)
        ik = jnp.min(jnp.where(work == mk, lane, LANES), axis=-1, keepdims=True)
        sel = lane == ik
        work = jnp.where(sel, NEG, work)
        vals.append(mk)
        idxs.append(ik)
        sels.append(sel)
    exps = [jnp.exp(vk - vals[0]) for vk in vals]
    denom = exps[0] + exps[1] + exps[2] + exps[3]
    chosen = jnp.logical_or(jnp.logical_or(sels[0], sels[1]), jnp.logical_or(sels[2], sels[3]))
    onehot = jnp.where(chosen, 1.0, 0.0)
    row = lax.broadcasted_iota(I32, (ts, ts), 0)
    col = lax.broadcasted_iota(I32, (ts, ts), 1)
    tri = jnp.where(row > col, 1.0, 0.0).astype(BF16)
    carry = carry_ref[...]
    rank = _dot(tri, onehot.astype(BF16)) + carry
    carry_new = carry + jnp.sum(onehot, axis=0, keepdims=True)
    carry_ref[...] = carry_new
    cnt_ref[...] = carry_new
    meta = jnp.zeros((ts, LANES), F32)
    for kk in range(TOP_K):
        rk = jnp.sum(jnp.where(sels[kk], rank, 0.0), axis=-1, keepdims=True)
        meta = jnp.where(lane == kk, idxs[kk].astype(F32), meta)
        meta = jnp.where(lane == TOP_K + kk, rk, meta)
        meta = jnp.where(lane == 2 * TOP_K + kk, exps[kk] / denom, meta)
    meta_ref[0] = meta


def _memrouter(o_fox, o_mla, x, g_fox, g_mla, w_mix, g_mq, w_mq, kmem, vmem, w_mo, g_ffn, wr_hi, wr_lo, b_r):
    b, s, d = x.shape
    m = kmem.shape[1]
    ts = min(TS_MEM, s)

    def full(arr):
        return pl.BlockSpec(arr.shape, lambda bi, si: (0,) * arr.ndim)

    def tok(width):
        return pl.BlockSpec((1, ts, width), lambda bi, si: (bi, si, 0))

    kv = pl.BlockSpec((1, m, MEM_WIDTH), lambda bi, si: (bi, 0, 0))
    cnt = pl.BlockSpec((1, LANES), lambda bi, si: (0, 0))
    return pl.pallas_call(
        functools.partial(_memrouter_kernel, ts=ts),
        grid=(b, s // ts),
        in_specs=[tok(FOX_WIDTH), tok(MLA_WIDTH), tok(d), full(g_fox), full(g_mla), full(w_mix), full(g_mq),
                  full(w_mq), kv, kv, full(w_mo), full(g_ffn), full(wr_hi), full(wr_lo), full(b_r)],
        out_specs=(tok(d), pl.BlockSpec((ts * SUBLANES, LANES), lambda bi, si: (bi * (s // ts) + si, 0)),
                   tok(LANES), cnt),
        out_shape=(jax.ShapeDtypeStruct((b, s, d), F32), jax.ShapeDtypeStruct((b * s * SUBLANES, LANES), F32),
                   jax.ShapeDtypeStruct((b, s, LANES), F32), jax.ShapeDtypeStruct((1, LANES), F32)),
        scratch_shapes=[pltpu.VMEM((1, LANES), F32)],
        compiler_params=_cparams(2),
        name="memrouter",
    )(o_fox, o_mla, x, g_fox, g_mla, w_mix, g_mq, w_mq, kmem, vmem, w_mo, g_ffn, wr_hi, wr_lo, b_r)


def _tile_copy(src_ref, src_row8, dst_ref, dst_row8, sem):
    return pltpu.make_async_copy(src_ref.at[pl.ds(pl.multiple_of(src_row8, SUBLANES), SUBLANES), :],
                                 dst_ref.at[pl.ds(pl.multiple_of(dst_row8, SUBLANES), SUBLANES), :], sem)


def _dispatch_kernel(last_ref, cnt_ref, nv_ref, pos_ref, at_ref, xs_ref, zbuf, zsem, sem, *, td, tm, min_used,
                     n_tiles):
    @pl.when(pl.program_id(0) == 0)
    def _():
        zbuf[...] = jnp.zeros_like(zbuf)

        def fill(row):
            return pltpu.make_async_copy(
                zbuf, xs_ref.at[pl.ds(pl.multiple_of(row * SUBLANES, SUBLANES), tm * SUBLANES), :], zsem)

        fills = [(cnt_ref[e] > 0, last_ref[e]) for e in range(N_EXPERTS)]
        fills += [(t >= nv_ref[0], t * tm) for t in range(min_used, n_tiles)]
        for pred, row in fills:
            pl.when(pred)(lambda row=row: fill(row).start())
        for pred, row in fills:
            pl.when(pred)(lambda row=row: fill(row).wait())

    def issue(r, c):
        for kk in range(TOP_K):
            _tile_copy(at_ref, r * SUBLANES, xs_ref, pos_ref[r * TOP_K + kk], sem).start(priority=kk % 2)
        return c

    lax.fori_loop(0, td, issue, 0, unroll=4)
    for _ in range(TOP_K):
        pltpu.make_async_copy(at_ref, xs_ref.at[pl.ds(0, td * SUBLANES), :], sem).wait()


def _dispatch(last_tile_row, cnt, nv, pos8, at, n_tiles, tm):
    n = at.shape[0] // SUBLANES
    td = min(TD_ROWS, n)
    grid_spec = pltpu.PrefetchScalarGridSpec(
        num_scalar_prefetch=3,
        grid=(n // td,),
        in_specs=[pl.BlockSpec((td * TOP_K,), lambda i, *_: (i,), memory_space=pltpu.SMEM),
                  pl.BlockSpec((td * SUBLANES, LANES), lambda i, *_: (i, 0))],
        out_specs=pl.BlockSpec(memory_space=pl.ANY),
        scratch_shapes=[pltpu.VMEM((tm * SUBLANES, LANES), F32), pltpu.SemaphoreType.DMA, pltpu.SemaphoreType.DMA],
    )
    min_used = n * TOP_K // tm
    return pl.pallas_call(
        functools.partial(_dispatch_kernel, td=td, tm=tm, min_used=min_used, n_tiles=n_tiles),
        grid_spec=grid_spec,
        out_shape=jax.ShapeDtypeStruct((n_tiles * tm * SUBLANES, LANES), F32),
        compiler_params=_cparams(1),
        name="dispatch",
    )(last_tile_row, cnt, nv, pos8, at)


def _moe_kernel(te_ref, nv_ref, slot_ref, nxt_ref, xs_ref, wgu_hbm, bgu_ref, wd_hbm, bd_ref, y_ref,
                wgu_f32, wd_f32, wgu_bf, wd_bf, sem, *, ch, tm):
    i = pl.program_id(0)
    d_exp = wd_hbm.shape[1]
    valid = i < nv_ref[0]
    new_expert = jnp.logical_or(i == 0, te_ref[i] != te_ref[jnp.maximum(i - 1, 0)])

    def weight_copies(e, sl):
        return (pltpu.make_async_copy(wgu_hbm.at[e], wgu_f32.at[sl], sem.at[sl, 0]),
                pltpu.make_async_copy(wd_hbm.at[e], wd_f32.at[sl], sem.at[sl, 1]))

    @pl.when(i == 0)
    def _():
        for c in weight_copies(te_ref[0], slot_ref[0]):
            c.start()

    @pl.when(jnp.logical_and(new_expert, valid))
    def _():
        sl = slot_ref[i]
        for c in weight_copies(te_ref[i], sl):
            c.wait()
        wgu_bf[...] = wgu_f32[sl].astype(BF16)
        wd_bf[...] = wd_f32[sl].astype(BF16)

        @pl.when(nxt_ref[i] >= 0)
        def _():
            for c in weight_copies(nxt_ref[i], 1 - sl):
                c.start()

    @pl.when(valid)
    def _():
        x = _load_token_tiles(xs_ref, tm).astype(BF16)
        acc = jnp.zeros((tm, wd_hbm.shape[2]), F32)
        for c in range(d_exp // ch):
            def gu(lo):
                return _dot(x, wgu_bf[:, lo:lo + ch]) + bgu_ref[0, :, lo:lo + ch]

            gate = jnp.minimum(gu(c * ch), SWIGLU_LIMIT)
            up = jnp.clip(gu(d_exp + c * ch), -SWIGLU_LIMIT, SWIGLU_LIMIT)
            act = gate * (1.0 / (1.0 + jnp.exp(-SWIGLU_ALPHA * gate))) * (up + 1.0)
            acc = acc + _dot(act.astype(BF16), wd_bf[c * ch:(c + 1) * ch, :])
        _store_token_tiles(y_ref, acc + bd_ref[0])

    @pl.when(jnp.logical_not(valid))
    def _():
        y_ref[...] = jnp.zeros_like(y_ref)


def _moe(te, nv, slot, nxt, xs, wgu, bgu, wd, bd, tm):
    n_tiles = xs.shape[0] // (tm * SUBLANES)
    _, d, d_exp2 = wgu.shape
    d_exp = wd.shape[1]
    assert d == SUBLANES * LANES and wd.shape[2] == d
    tile = pl.BlockSpec((tm * SUBLANES, LANES), lambda i, te, *_: (i, 0))
    grid_spec = pltpu.PrefetchScalarGridSpec(
        num_scalar_prefetch=4,
        grid=(n_tiles,),
        in_specs=[
            tile,
            pl.BlockSpec(memory_space=pl.ANY),
            pl.BlockSpec((1, 1, d_exp2), lambda i, te, *_: (te[i], 0, 0)),
            pl.BlockSpec(memory_space=pl.ANY),
            pl.BlockSpec((1, 1, d), lambda i, te, *_: (te[i], 0, 0)),
        ],
        out_specs=tile,
        scratch_shapes=[pltpu.VMEM((2, d, d_exp2), F32), pltpu.VMEM((2, d_exp, d), F32),
                        pltpu.VMEM((d, d_exp2), BF16), pltpu.VMEM((d_exp, d), BF16),
                        pltpu.SemaphoreType.DMA((2, 2))],
    )
    return pl.pallas_call(
        functools.partial(_moe_kernel, ch=min(MOE_CH, d_exp), tm=tm),
        grid_spec=grid_spec,
        out_shape=jax.ShapeDtypeStruct(xs.shape, F32),
        compiler_params=_cparams(1),
        name="moe",
    )(te, nv, slot, nxt, xs, wgu, bgu, wd, bd)


def _combine_kernel(pos_ref, posn_ref, h2_ref, meta_ref, y_ref, gf_ref, o_ref, ybuf, sem, *, td, final_norm):
    i = pl.program_id(0)
    slot = i % 2

    def issue_block(p_ref, sl):
        def issue(r, c):
            for kk in range(TOP_K):
                _tile_copy(y_ref, p_ref[r * TOP_K + kk], ybuf.at[sl, kk], r * SUBLANES,
                           sem.at[sl]).start(priority=kk % 2)
            return c

        lax.fori_loop(0, td, issue, 0, unroll=4)

    @pl.when(i == 0)
    def _():
        issue_block(pos_ref, 0)

    @pl.when(i + 1 < pl.num_programs(0))
    def _():
        issue_block(posn_ref, 1 - slot)

    for kk in range(TOP_K):
        pltpu.make_async_copy(y_ref.at[pl.ds(0, td * SUBLANES), :], ybuf.at[slot, kk], sem.at[slot]).wait()

    meta = meta_ref[...]
    acc = h2_ref[...]
    for kk in range(TOP_K):
        gate = meta[:, 2 * TOP_K + kk:2 * TOP_K + kk + 1]
        acc = acc + gate * _load_token_tiles(ybuf.at[slot, kk], td)
    o_ref[...] = _rms(acc, gf_ref[...]) if final_norm else acc


def _combine(pos8, h2, meta, y, g_final, final_norm):
    n, d = h2.shape
    td = min(TD_ROWS, n)
    n_blocks = n // td
    grid_spec = pltpu.PrefetchScalarGridSpec(
        num_scalar_prefetch=0,
        grid=(n_blocks,),
        in_specs=[pl.BlockSpec((td * TOP_K,), lambda i: (i,), memory_space=pltpu.SMEM),
                  pl.BlockSpec((td * TOP_K,), lambda i: (jnp.minimum(i + 1, n_blocks - 1),), memory_space=pltpu.SMEM),
                  pl.BlockSpec((td, d), lambda i: (i, 0)),
                  pl.BlockSpec((td, LANES), lambda i: (i, 0)),
                  pl.BlockSpec(memory_space=pl.ANY),
                  pl.BlockSpec(g_final.shape, lambda i: (0, 0))],
        out_specs=pl.BlockSpec((td, d), lambda i: (i, 0)),
        scratch_shapes=[pltpu.VMEM((2, TOP_K, td * SUBLANES, LANES), F32), pltpu.SemaphoreType.DMA((2,))],
    )
    return pl.pallas_call(
        functools.partial(_combine_kernel, td=td, final_norm=final_norm),
        grid_spec=grid_spec,
        out_shape=jax.ShapeDtypeStruct((n, d), F32),
        compiler_params=_cparams(1),
        name="combine",
    )(pos8, pos8, h2, meta, y, g_final)


def _rope_tables(seq_len):
    inv = 1.0 / (ROPE_BASE ** (jnp.arange(0, MLA_ROPE_DIM, 2, dtype=F32) / MLA_ROPE_DIM))
    ang = jnp.arange(seq_len, dtype=F32)[:, None] * inv[None, :]
    cos, sin = jnp.cos(ang), jnp.sin(ang)
    pad = jnp.zeros((seq_len, LANES - 2 * MLA_ROPE_DIM), F32)
    cos_t = jnp.concatenate([cos, cos, cos, cos, pad], axis=1)
    sin_t = jnp.concatenate([sin, sin, sin, sin, pad], axis=1)
    return cos_t, sin_t


def _rot_cols(w):
    half = MLA_ROPE_DIM // 2
    return jnp.concatenate([-w[:, half:], w[:, :half]], axis=1)


def _rope_pair_block(r0, r1):
    pad = jnp.zeros((r0.shape[0], LANES - 2 * MLA_ROPE_DIM), F32)
    return jnp.concatenate([r0, r1, pad], axis=1)


def _prep_inproj_weights(w_in, w_uq, w_ukv):
    d = w_in.shape[0]
    pts = np.cumsum((FOX_WIDTH, FOX_WIDTH, FOX_WIDTH, FOX_HEADS, MLA_Q_LORA, MLA_KV_LORA, MLA_ROPE_DIM))
    w_fq, w_fk, w_fv = w_in[:, :pts[0]], w_in[:, pts[0]:pts[1]], w_in[:, pts[1]:pts[2]]
    w_fl, w_ql = w_in[:, pts[2]:pts[3]], w_in[:, pts[3]:pts[4]]
    w_kvl, w_kr = w_in[:, pts[4]:pts[5]], w_in[:, pts[5]:pts[6]]
    misc = jnp.concatenate([w_fl, jnp.zeros((d, LANES - FOX_HEADS), F32)], axis=1)
    w1 = jnp.concatenate([w_fq, w_fk, w_fv, w_ql, w_kvl, misc, _rope_pair_block(w_kr, w_kr),
                          _rope_pair_block(_rot_cols(w_kr), _rot_cols(w_kr))], axis=1).astype(BF16)
    qd = MLA_NOPE_DIM + MLA_ROPE_DIM
    kvd = MLA_NOPE_DIM + MLA_V_DIM
    q_nope = [w_uq[:, h * qd:h * qd + MLA_NOPE_DIM] for h in range(MLA_HEADS)]
    q_rope = [w_uq[:, h * qd + MLA_NOPE_DIM:(h + 1) * qd] for h in range(MLA_HEADS)]
    wqa = [_rope_pair_block(q_rope[h], q_rope[h + 1]) for h in range(0, MLA_HEADS, 2)]
    wqb = [_rope_pair_block(_rot_cols(q_rope[h]), _rot_cols(q_rope[h + 1])) for h in range(0, MLA_HEADS, 2)]
    wk = [w_ukv[:, h * kvd:h * kvd + MLA_NOPE_DIM] for h in range(MLA_HEADS)]
    wv = [w_ukv[:, h * kvd + MLA_NOPE_DIM:(h + 1) * kvd] for h in range(MLA_HEADS)]
    cat = lambda xs: jnp.concatenate(xs, axis=1).astype(BF16)
    return w1, cat(q_nope), cat(wqa), cat(wqb), cat(wk), cat(wv)


def _routing_tables(counts, n_tiles, tm):
    tiles_e = (counts + tm - 1) // tm
    tile_end = jnp.cumsum(tiles_e)
    off = ((tile_end - tiles_e) * tm).astype(I32)
    total = tile_end[-1]
    ti = jnp.arange(n_tiles, dtype=I32)
    te = jnp.minimum(jnp.sum(ti[:, None] >= tile_end[None, :], axis=1), N_EXPERTS - 1).astype(I32)
    valid = ti < total
    last_e = jnp.sum(jnp.where(ti == total - 1, te, 0))
    te = jnp.where(valid, te, last_e).astype(I32)
    last_tile_row = (off + (tiles_e - 1) * tm).astype(I32)
    used = tiles_e > 0
    ids = jnp.arange(N_EXPERTS, dtype=I32)
    slot_e = (jnp.cumsum(used.astype(I32)) - 1) % 2
    later = jnp.where(used[None, :] & (ids[None, :] > ids[:, None]), ids[None, :], N_EXPERTS)
    nxt_e = jnp.min(later, axis=1)
    nxt_e = jnp.where(nxt_e == N_EXPERTS, -1, nxt_e).astype(I32)
    pick = te[:, None] == ids[None, :]
    slot = jnp.sum(jnp.where(pick, slot_e[None, :], 0), axis=1).astype(I32)
    nxt = jnp.sum(jnp.where(pick, nxt_e[None, :], 0), axis=1).astype(I32)
    return off, last_tile_row, te, total.astype(I32).reshape(1), slot, nxt


def kernel(x, mem, g_mix, w_in, b_f, g_q_lat, w_uq, g_kv_lat, w_ukv, g_fox_out, g_mla_out, w_o, g_mem_q, w_mem_q,
           g_mem_kv, w_mem_kv, w_mem_o, g_ffn, w_router, b_router, w_gate_up, b_gate_up, w_down, b_down, g_final):
    b, s, d = x.shape
    n = b * s
    depth = g_mix.shape[0]
    cos_t, sin_t = _rope_tables(s)
    row = lambda v: v.reshape(1, -1)
    h = x
    for l in range(depth):
        w1, wqn, wqa, wqb, wk, wv = _prep_inproj_weights(w_in[l], w_uq[l], w_ukv[l])
        fq, fk, fv, flog, qm, km, vm = _inproj(h, row(g_mix[l]), w1, row(g_q_lat[l]), wqn, wqa, wqb,
                                                row(g_kv_lat[l]), wk, wv, cos_t, sin_t)
        cq, ck = _decay(flog, b_f[l])
        o_fox = _attention_pipelined(True, fq, fk, fv, cq, ck)
        o_mla = _attention_pipelined(False, qm, km, vm)
        kmem, vmem = _memkv(mem, row(g_mem_kv[l]), w_mem_kv[l].astype(BF16))
        wr = jnp.zeros((d, LANES), F32).at[:, :N_EXPERTS].set(w_router[l])
        wr_hi = wr.astype(BF16)
        wr_lo = (wr - wr_hi.astype(F32)).astype(BF16)
        br = jnp.zeros((1, LANES), F32).at[0, :N_EXPERTS].set(b_router[l])
        h2, at, meta, cnt = _memrouter(o_fox, o_mla, h, row(g_fox_out[l]), row(g_mla_out[l]), w_o[l].astype(BF16),
                                       row(g_mem_q[l]), w_mem_q[l].astype(BF16), kmem, vmem,
                                       w_mem_o[l].astype(BF16), row(g_ffn[l]), wr_hi, wr_lo, br)
        meta2 = meta.reshape(n, LANES)
        ek = meta2[:, 0:TOP_K].astype(I32)
        rk = meta2[:, TOP_K:2 * TOP_K].astype(I32)
        n_tiles = n * TOP_K // TM_MOE + N_EXPERTS
        counts = cnt[0, :N_EXPERTS].astype(I32)
        off, last_tile_row, te, nv, slot, nxt = _routing_tables(counts, n_tiles, TM_MOE)
        off_of = jnp.sum(jnp.where(ek[..., None] == jnp.arange(N_EXPERTS, dtype=I32), off, 0), axis=-1)
        pos8 = ((off_of + rk) * SUBLANES).astype(I32).reshape(-1)
        xs = _dispatch(last_tile_row, counts, nv, pos8, at, n_tiles, TM_MOE)
        y = _moe(te, nv, slot, nxt, xs, w_gate_up[l], b_gate_up[l].reshape(N_EXPERTS, 1, -1), w_down[l],
                 b_down[l].reshape(N_EXPERTS, 1, -1), TM_MOE)
        h = _combine(pos8, h2.reshape(n, d), meta2, y, row(g_final), l == depth - 1).reshape(b, s, d)
    return h
```

```python
import functools

import numpy as np
import jax
import jax.numpy as jnp
from jax import lax
from jax.experimental import pallas as pl
from jax.experimental.pallas import tpu as pltpu

F32 = jnp.float32
BF16 = jnp.bfloat16
I32 = jnp.int32

LANES = 128
SUBLANES = 8
RMS_EPS = 1e-6
CHUNK = 64
FOX_HEADS = 8
FOX_HEAD_DIM = 64
FOX_WIDTH = FOX_HEADS * FOX_HEAD_DIM
MLA_HEADS = 8
MLA_Q_LORA = 384
MLA_KV_LORA = 256
MLA_NOPE_DIM = 64
MLA_ROPE_DIM = 32
MLA_V_DIM = 64
MLA_WIDTH = MLA_HEADS * MLA_V_DIM
ROPE_BASE = 10000.0
MEM_HEADS = 4
MEM_HEAD_DIM = 128
MEM_WIDTH = MEM_HEADS * MEM_HEAD_DIM
N_EXPERTS = 32
TOP_K = 4
SWIGLU_LIMIT = 7.0
SWIGLU_ALPHA = 1.702

LOG2E = 1.4426950408889634
NEG = -1e30
FOX_QSCALE = FOX_HEAD_DIM ** -0.5 * LOG2E
MLA_QSCALE = (MLA_NOPE_DIM + MLA_ROPE_DIM) ** -0.5 * LOG2E
MEM_QSCALE = MEM_HEAD_DIM ** -0.5 * LOG2E
FOX_DECAY_LANES = 16
META_ROWS = 16

VMEM_LIMIT = 56 * 1024 * 1024

TS_PROJ = 512
TS_MEM = 512
T_ATT = 512
DECAY_BLK = 256
TM_MOE = 512
TD_ROWS = 256
MOE_CH = 512


def _cparams(n_axes):
    return pltpu.CompilerParams(dimension_semantics=("arbitrary",) * n_axes, vmem_limit_bytes=VMEM_LIMIT)


def _dot(a, b):
    return jnp.dot(a, b, preferred_element_type=F32)


def _dot_nt(a, b):
    return lax.dot_general(a, b, (((1,), (1,)), ((), ())), preferred_element_type=F32)


def _rms(x, g):
    return x * lax.rsqrt(jnp.mean(x * x, axis=-1, keepdims=True) + RMS_EPS) * g


def _split3(x):
    hi = x.astype(BF16)
    r1 = x - hi.astype(F32)
    mid = r1.astype(BF16)
    lo = (r1 - mid.astype(F32)).astype(BF16)
    return hi, mid, lo


def _interleave_blocks(a, b):
    parts = []
    for p in range(a.shape[1] // LANES):
        parts += [a[:, p * LANES:(p + 1) * LANES], b[:, p * LANES:(p + 1) * LANES]]
    return jnp.concatenate(parts, axis=1)


def _inproj_kernel(x_ref, g_ref, w1_ref, gq_ref, wqn_ref, wqa_ref, wqb_ref, gkv_ref, wk_ref, wv_ref, cos_ref, sin_ref,
                   fq_ref, fk_ref, fv_ref, fl_ref, qm_ref, km_ref, vm_ref):
    a = _rms(x_ref[0], g_ref[...]).astype(BF16)

    def proj(lo, hi):
        return _dot(a, w1_ref[:, lo:hi])

    c0 = 3 * FOX_WIDTH
    c1 = c0 + MLA_Q_LORA
    c2 = c1 + MLA_KV_LORA
    fq_ref[0] = (proj(0, FOX_WIDTH) * FOX_QSCALE).astype(BF16)
    fk_ref[0] = proj(FOX_WIDTH, 2 * FOX_WIDTH).astype(BF16)
    fv_ref[0] = proj(2 * FOX_WIDTH, c0).astype(BF16)
    qn = _rms(proj(c0, c1), gq_ref[...]).astype(BF16)
    kvn = _rms(proj(c1, c2), gkv_ref[...]).astype(BF16)
    fl_ref[0] = proj(c2, c2 + LANES)
    cos = cos_ref[...]
    sin = sin_ref[...]
    pairs = MLA_HEADS // 2
    kpe = (proj(c2 + LANES, c2 + 2 * LANES) * cos + proj(c2 + 2 * LANES, c2 + 3 * LANES) * sin).astype(BF16)
    cos4 = jnp.concatenate([cos] * pairs, axis=1)
    sin4 = jnp.concatenate([sin] * pairs, axis=1)
    q_nope = (_dot(qn, wqn_ref[...]) * MLA_QSCALE).astype(BF16)
    q_rope = ((_dot(qn, wqa_ref[...]) * cos4 + _dot(qn, wqb_ref[...]) * sin4) * MLA_QSCALE).astype(BF16)
    qm_ref[0] = _interleave_blocks(q_nope, q_rope)
    k_nope = _dot(kvn, wk_ref[...]).astype(BF16)
    km_ref[0] = _interleave_blocks(k_nope, jnp.concatenate([kpe] * pairs, axis=1))
    vm_ref[0] = _dot(kvn, wv_ref[...]).astype(BF16)


def _inproj(x, g_mix, w1, g_q, wqn, wqa, wqb, g_kv, wk, wv, cos_t, sin_t):
    b, s, d = x.shape
    ts = min(TS_PROJ, s)

    def full(arr):
        return pl.BlockSpec(arr.shape, lambda bi, si: (0,) * arr.ndim)

    def tok(width):
        return pl.BlockSpec((1, ts, width), lambda bi, si: (bi, si, 0))

    tab = pl.BlockSpec((ts, LANES), lambda bi, si: (si, 0))
    pair_w = 2 * LANES * (MLA_HEADS // 2)
    out_shapes = (
        jax.ShapeDtypeStruct((b, s, FOX_WIDTH), BF16),
        jax.ShapeDtypeStruct((b, s, FOX_WIDTH), BF16),
        jax.ShapeDtypeStruct((b, s, FOX_WIDTH), BF16),
        jax.ShapeDtypeStruct((b, s, LANES), F32),
        jax.ShapeDtypeStruct((b, s, pair_w), BF16),
        jax.ShapeDtypeStruct((b, s, pair_w), BF16),
        jax.ShapeDtypeStruct((b, s, MLA_WIDTH), BF16),
    )
    return pl.pallas_call(
        _inproj_kernel,
        grid=(b, s // ts),
        in_specs=[tok(d), full(g_mix), full(w1), full(g_q), full(wqn), full(wqa), full(wqb), full(g_kv), full(wk),
                  full(wv), tab, tab],
        out_specs=(tok(FOX_WIDTH), tok(FOX_WIDTH), tok(FOX_WIDTH), tok(LANES), tok(pair_w), tok(pair_w),
                   tok(MLA_WIDTH)),
        out_shape=out_shapes,
        compiler_params=_cparams(2),
        name="inproj",
    )(x, g_mix, w1, g_q, wqn, wqa, wqb, g_kv, wk, wv, cos_t, sin_t)


def _decay_kernel(fl_ref, bf_ref, pq_ref, pk_ref, oq_ref, ok_ref, cq_ref, ck_ref, *, blk):
    s = fl_ref.shape[1]
    row = lax.broadcasted_iota(I32, (blk, blk), 0)
    col = lax.broadcasted_iota(I32, (blk, blk), 1)
    tri = jnp.where(row >= col, 1.0, 0.0).astype(BF16)
    carry = jnp.zeros((1, LANES), F32)
    for i in range(s // blk):
        sl = slice(i * blk, (i + 1) * blk)
        z = fl_ref[0, sl, :] + bf_ref[...]
        lf = (jnp.minimum(z, 0.0) - jnp.log1p(jnp.exp(-jnp.abs(z)))) * LOG2E
        h, m, l = _split3(lf)
        cs = _dot(tri, h) + _dot(tri, m) + _dot(tri, l) + carry
        carry = cs[blk - 1:blk, :]
        h, m, l = _split3(cs)
        cq = _dot(h, pq_ref[0]) + _dot(m, pq_ref[1]) + _dot(l, pq_ref[2]) + oq_ref[...]
        ck = _dot(h, pk_ref[0]) + _dot(m, pk_ref[1]) + _dot(l, pk_ref[2]) + ok_ref[...]
        cq_ref[0, sl, :] = cq.astype(BF16)
        ck_ref[0, sl, :] = ck.astype(BF16)


def _decay_tables():
    pq = np.zeros((3, LANES, LANES), np.float32)
    pk = np.zeros((3, LANES, LANES), np.float32)
    oq = np.zeros((1, LANES), np.float32)
    ok = np.zeros((1, LANES), np.float32)
    for h in range(FOX_HEADS):
        for part in range(3):
            pq[part, h, FOX_DECAY_LANES * h + part] = 1.0
            pk[part, h, FOX_DECAY_LANES * h + 3 + part] = -1.0
            oq[0, FOX_DECAY_LANES * h + 3 + part] = 1.0
            ok[0, FOX_DECAY_LANES * h + part] = 1.0
    return jnp.asarray(pq, BF16), jnp.asarray(pk, BF16), jnp.asarray(oq), jnp.asarray(ok)


def _decay(flog, b_f):
    b, s, _ = flog.shape
    blk = min(DECAY_BLK, s)
    pq, pk, oq, ok = _decay_tables()
    bf = jnp.zeros((1, LANES), F32).at[0, :FOX_HEADS].set(b_f)

    def full(arr):
        return pl.BlockSpec(arr.shape, lambda bi: (0,) * arr.ndim)

    seq = pl.BlockSpec((1, s, LANES), lambda bi: (bi, 0, 0))
    return pl.pallas_call(
        functools.partial(_decay_kernel, blk=blk),
        grid=(b,),
        in_specs=[seq, full(bf), full(pq), full(pk), full(oq), full(ok)],
        out_specs=(seq, seq),
        out_shape=(jax.ShapeDtypeStruct((b, s, LANES), BF16), jax.ShapeDtypeStruct((b, s, LANES), BF16)),
        compiler_params=_cparams(1),
        name="decay",
    )(flog, bf, pq, pk, oq, ok)


def _attn_pipe_kernel(*refs, fox, t, nq, pairs):
    if fox:
        q_ref, cq_ref, k_ref, ck_ref, v_ref, o_ref, s_even, s_odd = refs
        group = FOX_DECAY_LANES
    else:
        q_ref, k_ref, v_ref, o_ref, s_even, s_odd = refs
        group = MLA_ROPE_DIM
    g = pl.program_id(0)
    i = g % nq
    pair = (jnp.minimum(g, pl.num_programs(0) - 2) // nq) % pairs
    base = 2 * pair if fox else 0
    lane = lax.broadcasted_iota(I32, (1, LANES), 1)

    @pl.when(g == 0)
    def _():
        s_odd[...] = jnp.zeros_like(s_odd)

    def body(iv):
        s_new, s_old = (s_even, s_odd) if iv % 2 == 0 else (s_odd, s_even)
        half = t // 2
        row = lax.broadcasted_iota(I32, (half, half), 0)
        col = lax.broadcasted_iota(I32, (half, half), 1)
        allowed = (col <= row) if fox else ((col // CHUNK) <= (row // CHUNK))
        past = iv * t
        if fox:
            q_main, q_extra = q_ref[0], cq_ref[0]
        else:
            q_main, q_extra = q_ref[0, :, :LANES], q_ref[0, :, LANES:]
        zero = jnp.zeros_like(q_main)

        def keys(lo, hi):
            if fox:
                return jnp.concatenate([k_ref[0, lo:hi, :], ck_ref[0, lo:hi, :]], axis=1)
            return k_ref[0, lo:hi, :]

        qa = [jnp.concatenate([jnp.where((lane // (LANES // 2)) == hh, q_main, zero),
                               jnp.where((lane // group) == base + hh, q_extra, zero)], axis=1) for hh in range(2)]
        for hh in range(2):
            if iv > 0:
                s_new[hh, :, 0:past] = _dot_nt(qa[hh], keys(0, past))
            upper = _dot_nt(qa[hh][:half], keys(past, past + half))
            s_new[hh, :half, past:past + half] = jnp.where(allowed, upper, NEG)
            lower = _dot_nt(qa[hh][half:], keys(past, past + t))
            s_new[hh, half:, past:past + half] = lower[:, :half]
            s_new[hh, half:, past + half:past + t] = jnp.where(allowed, lower[:, half:], NEG)

        seen = ((iv - 1) % nq + 1) * t
        outs = []
        for hh in range(2):
            sc = s_old[hh, :, 0:seen - half]
            sc_tail = s_old[hh, half:, seen - half:seen]
            m_main = jnp.max(sc, axis=-1, keepdims=True)
            m_low = jnp.maximum(m_main[half:], jnp.max(sc_tail, axis=-1, keepdims=True))
            p = jnp.exp2(sc - jnp.concatenate([m_main[:half], m_low], axis=0))
            p_tail = jnp.exp2(sc_tail - m_low)
            l_main = jnp.sum(p, axis=-1, keepdims=True)
            l = jnp.concatenate([l_main[:half], l_main[half:] + jnp.sum(p_tail, axis=-1, keepdims=True)], axis=0)
            acc = _dot(p.astype(BF16), v_ref[0, 0:seen - half, :])
            acc_tail = _dot(p_tail.astype(BF16), v_ref[0, seen - half:seen, :])
            acc = jnp.concatenate([acc[:half], acc[half:] + acc_tail], axis=0)
            outs.append(acc * (1.0 / l))
        o_ref[0] = jnp.where(lane < LANES // 2, outs[0], outs[1]).astype(o_ref.dtype)

    for iv in range(nq):
        pl.when(i == iv)(functools.partial(body, iv))


def _attention_pipelined(fox, q, k, v, cq=None, ck=None):
    b, s, _ = v.shape
    t = min(T_ATT, s)
    nq = s // t
    assert nq % 2 == 0
    pairs = v.shape[2] // LANES
    qw = q.shape[2] // pairs
    kw = k.shape[2] // pairs
    items = b * pairs * nq

    def item(g):
        return g // (pairs * nq), (g // nq) % pairs, g % nq

    def cur(g):
        return item(jnp.minimum(g, items - 1))

    def prev(g):
        return item(jnp.maximum(g - 1, 0))

    qspec = pl.BlockSpec((1, t, qw), lambda g: (cur(g)[0], cur(g)[2], cur(g)[1]))
    kspec = pl.BlockSpec((1, s, kw), lambda g: (cur(g)[0], 0, cur(g)[1]))
    vspec = pl.BlockSpec((1, s, LANES), lambda g: (prev(g)[0], 0, prev(g)[1]))
    ospec = pl.BlockSpec((1, t, LANES), lambda g: (prev(g)[0], prev(g)[2], prev(g)[1]))
    if fox:
        cqspec = pl.BlockSpec((1, t, LANES), lambda g: (cur(g)[0], cur(g)[2], 0))
        ckspec = pl.BlockSpec((1, s, LANES), lambda g: (cur(g)[0], 0, 0))
        in_specs = [qspec, cqspec, kspec, ckspec, vspec]
        args = (q, cq, k, ck, v)
    else:
        in_specs = [qspec, kspec, vspec]
        args = (q, k, v)
    return pl.pallas_call(
        functools.partial(_attn_pipe_kernel, fox=fox, t=t, nq=nq, pairs=pairs),
        grid=(items + 1,),
        in_specs=in_specs,
        out_specs=ospec,
        out_shape=jax.ShapeDtypeStruct((b, s, pairs * LANES), BF16),
        scratch_shapes=[pltpu.VMEM((2, t, s), F32), pltpu.VMEM((2, t, s), F32)],
        compiler_params=_cparams(1),
        name="fox_attn" if fox else "mla_attn",
    )(*args)


def _memkv_kernel(mem_ref, g_ref, w_ref, k_ref, v_ref):
    a = _rms(mem_ref[0], g_ref[...]).astype(BF16)
    k_ref[0] = _dot(a, w_ref[:, 0:MEM_WIDTH]).astype(BF16)
    v_ref[0] = _dot(a, w_ref[:, MEM_WIDTH:2 * MEM_WIDTH]).astype(BF16)


def _memkv(mem, g, w):
    b, m, d = mem.shape
    kv = pl.BlockSpec((1, m, MEM_WIDTH), lambda bi: (bi, 0, 0))
    return pl.pallas_call(
        _memkv_kernel,
        grid=(b,),
        in_specs=[pl.BlockSpec((1, m, d), lambda bi: (bi, 0, 0)), pl.BlockSpec(g.shape, lambda bi: (0, 0)),
                  pl.BlockSpec(w.shape, lambda bi: (0, 0))],
        out_specs=(kv, kv),
        out_shape=(jax.ShapeDtypeStruct((b, m, MEM_WIDTH), BF16), jax.ShapeDtypeStruct((b, m, MEM_WIDTH), BF16)),
        compiler_params=_cparams(1),
        name="memkv",
    )(mem, g, w)


def _store_token_tiles(ref, val):
    rows = val.shape[0]
    for j in range(SUBLANES):
        ref[pl.ds(j, rows, stride=SUBLANES), :] = val[:, j * LANES:(j + 1) * LANES]


def _load_token_tiles(ref, rows):
    return jnp.concatenate([ref[pl.ds(j, rows, stride=SUBLANES), :] for j in range(SUBLANES)], axis=1)


def _memrouter_kernel(of_ref, om_ref, x_ref, gf_ref, gm_ref, wmix_ref, gq_ref, wq_ref, k_ref, v_ref, wo_ref,
                      gffn_ref, wrh_ref, wrl_ref, br_ref, h2_ref, at_ref, meta_ref, metat_ref, cnt_ref, carry_ref,
                      *, ts):
    first = jnp.logical_and(pl.program_id(0) == 0, pl.program_id(1) == 0)

    @pl.when(first)
    def _():
        carry_ref[...] = jnp.zeros_like(carry_ref)

    nf = _rms(of_ref[0].astype(F32), gf_ref[...]).astype(BF16)
    nm = _rms(om_ref[0].astype(F32), gm_ref[...]).astype(BF16)
    h1 = x_ref[0] + _dot(nf, wmix_ref[0:FOX_WIDTH, :]) + _dot(nm, wmix_ref[FOX_WIDTH:FOX_WIDTH + MLA_WIDTH, :])

    q = (_dot(_rms(h1, gq_ref[...]).astype(BF16), wq_ref[...]) * MEM_QSCALE).astype(BF16)
    heads = []
    for h in range(MEM_HEADS):
        sl = slice(h * MEM_HEAD_DIM, (h + 1) * MEM_HEAD_DIM)
        sc = _dot_nt(q[:, sl], k_ref[0, :, sl])
        m = jnp.max(sc, axis=-1, keepdims=True)
        pm = jnp.exp2(sc - m)
        l = jnp.sum(pm, axis=-1, keepdims=True)
        heads.append((_dot(pm.astype(BF16), v_ref[0, :, sl]) / l).astype(BF16))
    h2 = h1 + _dot(jnp.concatenate(heads, axis=1), wo_ref[...])
    h2_ref[0] = h2

    a = _rms(h2, gffn_ref[...])
    _store_token_tiles(at_ref, a)
    a_hi = a.astype(BF16)
    a_lo = (a - a_hi.astype(F32)).astype(BF16)
    logits = _dot(a_hi, wrh_ref[...]) + _dot(a_lo, wrh_ref[...]) + _dot(a_hi, wrl_ref[...]) + br_ref[...]
    lane = lax.broadcasted_iota(I32, (ts, LANES), 1)
    work = jnp.where(lane < N_EXPERTS, logits, NEG)
    vals, idxs, sels = [], [], []
    for _ in range(TOP_K):
        mk = jnp.max(work, axis=-1, keepdims=True)
        ik = jnp.min(jnp.where(work == mk, lane, LANES), axis=-1, keepdims=True)
        sel = lane == ik
        work = jnp.where(sel, NEG, work)
        vals.append(mk)
        idxs.append(ik)
        sels.append(sel)
    exps = [jnp.exp(vk - vals[0]) for vk in vals]
    denom = exps[0] + exps[1] + exps[2] + exps[3]
    chosen = jnp.logical_or(jnp.logical_or(sels[0], sels[1]), jnp.logical_or(sels[2], sels[3]))
    onehot = jnp.where(chosen, 1.0, 0.0)
    row = lax.broadcasted_iota(I32, (ts, ts), 0)
    col = lax.broadcasted_iota(I32, (ts, ts), 1)
    tri = jnp.where(row > col, 1.0, 0.0).astype(BF16)
    carry = carry_ref[...]
    rank = _dot(tri, onehot.astype(BF16)) + carry
    carry_new = carry + jnp.sum(onehot, axis=0, keepdims=True)
    carry_ref[...] = carry_new
    cnt_ref[...] = carry_new
    meta = jnp.zeros((ts, LANES), F32)
    for kk in range(TOP_K):
        rk = jnp.sum(jnp.where(sels[kk], rank, 0.0), axis=-1, keepdims=True)
        meta = jnp.where(lane == kk, idxs[kk].astype(F32), meta)
        meta = jnp.where(lane == TOP_K + kk, rk, meta)
        meta = jnp.where(lane == 2 * TOP_K + kk, exps[kk] / denom, meta)
    meta_ref[0] = meta
    metat_ref[...] = jnp.transpose(meta)[0:META_ROWS, :]


def _memrouter(o_fox, o_mla, x, g_fox, g_mla, w_mix, g_mq, w_mq, kmem, vmem, w_mo, g_ffn, wr_hi, wr_lo, b_r):
    b, s, d = x.shape
    m = kmem.shape[1]
    ts = min(TS_MEM, s)

    def full(arr):
        return pl.BlockSpec(arr.shape, lambda bi, si: (0,) * arr.ndim)

    def tok(width):
        return pl.BlockSpec((1, ts, width), lambda bi, si: (bi, si, 0))

    kv = pl.BlockSpec((1, m, MEM_WIDTH), lambda bi, si: (bi, 0, 0))
    cnt = pl.BlockSpec((1, LANES), lambda bi, si: (0, 0))
    return pl.pallas_call(
        functools.partial(_memrouter_kernel, ts=ts),
        grid=(b, s // ts),
        in_specs=[tok(FOX_WIDTH), tok(MLA_WIDTH), tok(d), full(g_fox), full(g_mla), full(w_mix), full(g_mq),
                  full(w_mq), kv, kv, full(w_mo), full(g_ffn), full(wr_hi), full(wr_lo), full(b_r)],
        out_specs=(tok(d), pl.BlockSpec((ts * SUBLANES, LANES), lambda bi, si: (bi * (s // ts) + si, 0)),
                   tok(LANES), pl.BlockSpec((META_ROWS, ts), lambda bi, si: (0, bi * (s // ts) + si)), cnt),
        out_shape=(jax.ShapeDtypeStruct((b, s, d), F32), jax.ShapeDtypeStruct((b * s * SUBLANES, LANES), F32),
                   jax.ShapeDtypeStruct((b, s, LANES), F32), jax.ShapeDtypeStruct((META_ROWS, b * s), F32),
                   jax.ShapeDtypeStruct((1, LANES), F32)),
        scratch_shapes=[pltpu.VMEM((1, LANES), F32)],
        compiler_params=_cparams(2),
        name="memrouter",
    )(o_fox, o_mla, x, g_fox, g_mla, w_mix, g_mq, w_mq, kmem, vmem, w_mo, g_ffn, wr_hi, wr_lo, b_r)


def _tile_copy(src_ref, src_row8, dst_ref, dst_row8, sem):
    return pltpu.make_async_copy(src_ref.at[pl.ds(pl.multiple_of(src_row8, SUBLANES), SUBLANES), :],
                                 dst_ref.at[pl.ds(pl.multiple_of(dst_row8, SUBLANES), SUBLANES), :], sem)


def _dispatch_kernel(last_ref, cnt_ref, nv_ref, pos_ref, at_ref, xs_ref, zbuf, zsem, sem, *, td, tm, min_used,
                     n_tiles):
    @pl.when(pl.program_id(0) == 0)
    def _():
        zbuf[...] = jnp.zeros_like(zbuf)

        def fill(row):
            return pltpu.make_async_copy(
                zbuf, xs_ref.at[pl.ds(pl.multiple_of(row * SUBLANES, SUBLANES), tm * SUBLANES), :], zsem)

        fills = [(cnt_ref[e] > 0, last_ref[e]) for e in range(N_EXPERTS)]
        fills += [(t >= nv_ref[0], t * tm) for t in range(min_used, n_tiles)]
        for pred, row in fills:
            pl.when(pred)(lambda row=row: fill(row).start())
        for pred, row in fills:
            pl.when(pred)(lambda row=row: fill(row).wait())

    def issue(r, c):
        for kk in range(TOP_K):
            _tile_copy(at_ref, r * SUBLANES, xs_ref, pos_ref[kk, r], sem).start(priority=kk % 2)
        return c

    lax.fori_loop(0, td, issue, 0, unroll=4)
    for _ in range(TOP_K):
        pltpu.make_async_copy(at_ref, xs_ref.at[pl.ds(0, td * SUBLANES), :], sem).wait()


def _dispatch(last_tile_row, cnt, nv, pos8, at, n_tiles, tm):
    n = at.shape[0] // SUBLANES
    td = min(TD_ROWS, n)
    grid_spec = pltpu.PrefetchScalarGridSpec(
        num_scalar_prefetch=3,
        grid=(n // td,),
        in_specs=[pl.BlockSpec((TOP_K, td), lambda i, *_: (0, i), memory_space=pltpu.SMEM),
                  pl.BlockSpec((td * SUBLANES, LANES), lambda i, *_: (i, 0))],
        out_specs=pl.BlockSpec(memory_space=pl.ANY),
        scratch_shapes=[pltpu.VMEM((tm * SUBLANES, LANES), F32), pltpu.SemaphoreType.DMA, pltpu.SemaphoreType.DMA],
    )
    min_used = n * TOP_K // tm
    return pl.pallas_call(
        functools.partial(_dispatch_kernel, td=td, tm=tm, min_used=min_used, n_tiles=n_tiles),
        grid_spec=grid_spec,
        out_shape=jax.ShapeDtypeStruct((n_tiles * tm * SUBLANES, LANES), F32),
        compiler_params=_cparams(1),
        name="dispatch",
    )(last_tile_row, cnt, nv, pos8, at)


def _moe_kernel(te_ref, rows_ref, slot_ref, nxt_ref, xs_ref, wgu_hbm, bgu_ref, wd_hbm, bd_ref, y_ref,
                wgu_f32, wd_f32, wgu_bf, wd_bf, sem, *, ch, tm):
    i = pl.program_id(0)
    d_exp = wd_hbm.shape[1]
    rows_used = rows_ref[i]
    valid = rows_used > 0
    new_expert = jnp.logical_or(i == 0, te_ref[i] != te_ref[jnp.maximum(i - 1, 0)])

    def weight_copies(e, sl):
        return (pltpu.make_async_copy(wgu_hbm.at[e], wgu_f32.at[sl], sem.at[sl, 0]),
                pltpu.make_async_copy(wd_hbm.at[e], wd_f32.at[sl], sem.at[sl, 1]))

    @pl.when(i == 0)
    def _():
        for c in weight_copies(te_ref[0], slot_ref[0]):
            c.start()

    @pl.when(jnp.logical_and(new_expert, valid))
    def _():
        sl = slot_ref[i]
        for c in weight_copies(te_ref[i], sl):
            c.wait()
        wgu_bf[...] = wgu_f32[sl].astype(BF16)
        wd_bf[...] = wd_f32[sl].astype(BF16)

        @pl.when(nxt_ref[i] >= 0)
        def _():
            for c in weight_copies(nxt_ref[i], 1 - sl):
                c.start()

    def ffn(rows):
        x = _load_token_tiles(xs_ref, rows).astype(BF16)
        acc = jnp.zeros((rows, wd_hbm.shape[2]), F32)
        for c in range(d_exp // ch):
            def gu(lo):
                return _dot(x, wgu_bf[:, lo:lo + ch]) + bgu_ref[0, :, lo:lo + ch]

            gate = jnp.minimum(gu(c * ch), SWIGLU_LIMIT)
            up = jnp.clip(gu(d_exp + c * ch), -SWIGLU_LIMIT, SWIGLU_LIMIT)
            act = gate * (1.0 / (1.0 + jnp.exp(-SWIGLU_ALPHA * gate))) * (up + 1.0)
            acc = acc + _dot(act.astype(BF16), wd_bf[c * ch:(c + 1) * ch, :])
        return acc + bd_ref[0]

    half = tm // 2

    @pl.when(rows_used > half)
    def _():
        _store_token_tiles(y_ref, ffn(tm))

    @pl.when(jnp.logical_and(valid, rows_used <= half))
    def _():
        _store_token_tiles(y_ref, ffn(half))
        y_ref[half * SUBLANES:, :] = jnp.zeros((half * SUBLANES, LANES), F32)

    @pl.when(jnp.logical_not(valid))
    def _():
        y_ref[...] = jnp.zeros_like(y_ref)


def _moe(te, rows, slot, nxt, xs, wgu, bgu, wd, bd, tm):
    n_tiles = xs.shape[0] // (tm * SUBLANES)
    _, d, d_exp2 = wgu.shape
    d_exp = wd.shape[1]
    assert d == SUBLANES * LANES and wd.shape[2] == d
    tile = pl.BlockSpec((tm * SUBLANES, LANES), lambda i, te, *_: (i, 0))
    grid_spec = pltpu.PrefetchScalarGridSpec(
        num_scalar_prefetch=4,
        grid=(n_tiles,),
        in_specs=[
            tile,
            pl.BlockSpec(memory_space=pl.ANY),
            pl.BlockSpec((1, 1, d_exp2), lambda i, te, *_: (te[i], 0, 0)),
            pl.BlockSpec(memory_space=pl.ANY),
            pl.BlockSpec((1, 1, d), lambda i, te, *_: (te[i], 0, 0)),
        ],
        out_specs=tile,
        scratch_shapes=[pltpu.VMEM((2, d, d_exp2), F32), pltpu.VMEM((2, d_exp, d), F32),
                        pltpu.VMEM((d, d_exp2), BF16), pltpu.VMEM((d_exp, d), BF16),
                        pltpu.SemaphoreType.DMA((2, 2))],
    )
    return pl.pallas_call(
        functools.partial(_moe_kernel, ch=min(MOE_CH, d_exp), tm=tm),
        grid_spec=grid_spec,
        out_shape=jax.ShapeDtypeStruct(xs.shape, F32),
        compiler_params=_cparams(1),
        name="moe",
    )(te, rows, slot, nxt, xs, wgu, bgu, wd, bd)


def _combine_kernel(pos_ref, posn_ref, h2_ref, meta_ref, y_ref, gf_ref, o_ref, ybuf, sem, *, td, final_norm):
    i = pl.program_id(0)
    slot = i % 2

    def issue_block(p_ref, sl):
        def issue(r, c):
            for kk in range(TOP_K):
                _tile_copy(y_ref, p_ref[kk, r], ybuf.at[sl, kk], r * SUBLANES,
                           sem.at[sl]).start(priority=kk % 2)
            return c

        lax.fori_loop(0, td, issue, 0, unroll=4)

    @pl.when(i == 0)
    def _():
        issue_block(pos_ref, 0)

    @pl.when(i + 1 < pl.num_programs(0))
    def _():
        issue_block(posn_ref, 1 - slot)

    for kk in range(TOP_K):
        pltpu.make_async_copy(y_ref.at[pl.ds(0, td * SUBLANES), :], ybuf.at[slot, kk], sem.at[slot]).wait()

    meta = meta_ref[...]
    acc = h2_ref[...]
    for kk in range(TOP_K):
        gate = meta[:, 2 * TOP_K + kk:2 * TOP_K + kk + 1]
        acc = acc + gate * _load_token_tiles(ybuf.at[slot, kk], td)
    o_ref[...] = _rms(acc, gf_ref[...]) if final_norm else acc


def _combine(pos8, h2, meta, y, g_final, final_norm):
    n, d = h2.shape
    td = min(TD_ROWS, n)
    n_blocks = n // td
    grid_spec = pltpu.PrefetchScalarGridSpec(
        num_scalar_prefetch=0,
        grid=(n_blocks,),
        in_specs=[pl.BlockSpec((TOP_K, td), lambda i: (0, i), memory_space=pltpu.SMEM),
                  pl.BlockSpec((TOP_K, td), lambda i: (0, jnp.minimum(i + 1, n_blocks - 1)), memory_space=pltpu.SMEM),
                  pl.BlockSpec((td, d), lambda i: (i, 0)),
                  pl.BlockSpec((td, LANES), lambda i: (i, 0)),
                  pl.BlockSpec(memory_space=pl.ANY),
                  pl.BlockSpec(g_final.shape, lambda i: (0, 0))],
        out_specs=pl.BlockSpec((td, d), lambda i: (i, 0)),
        scratch_shapes=[pltpu.VMEM((2, TOP_K, td * SUBLANES, LANES), F32), pltpu.SemaphoreType.DMA((2,))],
    )
    return pl.pallas_call(
        functools.partial(_combine_kernel, td=td, final_norm=final_norm),
        grid_spec=grid_spec,
        out_shape=jax.ShapeDtypeStruct((n, d), F32),
        compiler_params=_cparams(1),
        name="combine",
    )(pos8, pos8, h2, meta, y, g_final)


def _rope_tables(seq_len):
    inv = 1.0 / (ROPE_BASE ** (jnp.arange(0, MLA_ROPE_DIM, 2, dtype=F32) / MLA_ROPE_DIM))
    ang = jnp.arange(seq_len, dtype=F32)[:, None] * inv[None, :]
    cos, sin = jnp.cos(ang), jnp.sin(ang)
    pad = jnp.zeros((seq_len, LANES - 2 * MLA_ROPE_DIM), F32)
    cos_t = jnp.concatenate([cos, cos, cos, cos, pad], axis=1)
    sin_t = jnp.concatenate([sin, sin, sin, sin, pad], axis=1)
    return cos_t, sin_t


def _rot_cols(w):
    half = MLA_ROPE_DIM // 2
    return jnp.concatenate([-w[:, half:], w[:, :half]], axis=1)


def _rope_pair_block(r0, r1):
    pad = jnp.zeros((r0.shape[0], LANES - 2 * MLA_ROPE_DIM), F32)
    return jnp.concatenate([r0, r1, pad], axis=1)


def _prep_inproj_weights(w_in, w_uq, w_ukv):
    d = w_in.shape[0]
    pts = np.cumsum((FOX_WIDTH, FOX_WIDTH, FOX_WIDTH, FOX_HEADS, MLA_Q_LORA, MLA_KV_LORA, MLA_ROPE_DIM))
    w_fq, w_fk, w_fv = w_in[:, :pts[0]], w_in[:, pts[0]:pts[1]], w_in[:, pts[1]:pts[2]]
    w_fl, w_ql = w_in[:, pts[2]:pts[3]], w_in[:, pts[3]:pts[4]]
    w_kvl, w_kr = w_in[:, pts[4]:pts[5]], w_in[:, pts[5]:pts[6]]
    misc = jnp.concatenate([w_fl, jnp.zeros((d, LANES - FOX_HEADS), F32)], axis=1)
    w1 = jnp.concatenate([w_fq, w_fk, w_fv, w_ql, w_kvl, misc, _rope_pair_block(w_kr, w_kr),
                          _rope_pair_block(_rot_cols(w_kr), _rot_cols(w_kr))], axis=1).astype(BF16)
    qd = MLA_NOPE_DIM + MLA_ROPE_DIM
    kvd = MLA_NOPE_DIM + MLA_V_DIM
    q_nope = [w_uq[:, h * qd:h * qd + MLA_NOPE_DIM] for h in range(MLA_HEADS)]
    q_rope = [w_uq[:, h * qd + MLA_NOPE_DIM:(h + 1) * qd] for h in range(MLA_HEADS)]
    wqa = [_rope_pair_block(q_rope[h], q_rope[h + 1]) for h in range(0, MLA_HEADS, 2)]
    wqb = [_rope_pair_block(_rot_cols(q_rope[h]), _rot_cols(q_rope[h + 1])) for h in range(0, MLA_HEADS, 2)]
    wk = [w_ukv[:, h * kvd:h * kvd + MLA_NOPE_DIM] for h in range(MLA_HEADS)]
    wv = [w_ukv[:, h * kvd + MLA_NOPE_DIM:(h + 1) * kvd] for h in range(MLA_HEADS)]
    cat = lambda xs: jnp.concatenate(xs, axis=1).astype(BF16)
    return w1, cat(q_nope), cat(wqa), cat(wqb), cat(wk), cat(wv)


def _routing_tables(counts, n_tiles, tm):
    tiles_e = (counts + tm - 1) // tm
    tile_end = jnp.cumsum(tiles_e)
    off = ((tile_end - tiles_e) * tm).astype(I32)
    total = tile_end[-1]
    ti = jnp.arange(n_tiles, dtype=I32)
    te = jnp.minimum(jnp.sum(ti[:, None] >= tile_end[None, :], axis=1), N_EXPERTS - 1).astype(I32)
    valid = ti < total
    last_e = jnp.sum(jnp.where(ti == total - 1, te, 0))
    te = jnp.where(valid, te, last_e).astype(I32)
    last_tile_row = (off + (tiles_e - 1) * tm).astype(I32)
    used = tiles_e > 0
    ids = jnp.arange(N_EXPERTS, dtype=I32)
    slot_e = (jnp.cumsum(used.astype(I32)) - 1) % 2
    later = jnp.where(used[None, :] & (ids[None, :] > ids[:, None]), ids[None, :], N_EXPERTS)
    nxt_e = jnp.min(later, axis=1)
    nxt_e = jnp.where(nxt_e == N_EXPERTS, -1, nxt_e).astype(I32)
    pick = te[:, None] == ids[None, :]
    slot = jnp.sum(jnp.where(pick, slot_e[None, :], 0), axis=1).astype(I32)
    nxt = jnp.sum(jnp.where(pick, nxt_e[None, :], 0), axis=1).astype(I32)
    end_row = jnp.sum(jnp.where(pick, (off + counts)[None, :], 0), axis=1)
    rows = jnp.where(valid, jnp.clip(end_row - ti * tm, 0, tm), 0).astype(I32)
    return off, last_tile_row, te, total.astype(I32).reshape(1), slot, nxt, rows


def kernel(x, mem, g_mix, w_in, b_f, g_q_lat, w_uq, g_kv_lat, w_ukv, g_fox_out, g_mla_out, w_o, g_mem_q, w_mem_q,
           g_mem_kv, w_mem_kv, w_mem_o, g_ffn, w_router, b_router, w_gate_up, b_gate_up, w_down, b_down, g_final):
    b, s, d = x.shape
    n = b * s
    depth = g_mix.shape[0]
    cos_t, sin_t = _rope_tables(s)
    row = lambda v: v.reshape(1, -1)
    h = x
    for l in range(depth):
        w1, wqn, wqa, wqb, wk, wv = _prep_inproj_weights(w_in[l], w_uq[l], w_ukv[l])
        fq, fk, fv, flog, qm, km, vm = _inproj(h, row(g_mix[l]), w1, row(g_q_lat[l]), wqn, wqa, wqb,
                                                row(g_kv_lat[l]), wk, wv, cos_t, sin_t)
        cq, ck = _decay(flog, b_f[l])
        o_fox = _attention_pipelined(True, fq, fk, fv, cq, ck)
        o_mla = _attention_pipelined(False, qm, km, vm)
        kmem, vmem = _memkv(mem, row(g_mem_kv[l]), w_mem_kv[l].astype(BF16))
        wr = jnp.zeros((d, LANES), F32).at[:, :N_EXPERTS].set(w_router[l])
        wr_hi = wr.astype(BF16)
        wr_lo = (wr - wr_hi.astype(F32)).astype(BF16)
        br = jnp.zeros((1, LANES), F32).at[0, :N_EXPERTS].set(b_router[l])
        h2, at, meta, meta_t, cnt = _memrouter(
            o_fox, o_mla, h, row(g_fox_out[l]), row(g_mla_out[l]), w_o[l].astype(BF16), row(g_mem_q[l]),
            w_mem_q[l].astype(BF16), kmem, vmem, w_mem_o[l].astype(BF16), row(g_ffn[l]), wr_hi, wr_lo, br)
        ek = meta_t[0:TOP_K].astype(I32)
        rk = meta_t[TOP_K:2 * TOP_K].astype(I32)
        n_tiles = n * TOP_K // TM_MOE + N_EXPERTS
        counts = cnt[0, :N_EXPERTS].astype(I32)
        off, last_tile_row, te, nv, slot, nxt, rows = _routing_tables(counts, n_tiles, TM_MOE)
        ids = jnp.arange(N_EXPERTS, dtype=I32)[:, None, None]
        off_of = jnp.sum(jnp.where(ek[None] == ids, off[:, None, None], 0), axis=0)
        pos8 = ((off_of + rk) * SUBLANES).astype(I32)
        xs = _dispatch(last_tile_row, counts, nv, pos8, at, n_tiles, TM_MOE)
        y = _moe(te, rows, slot, nxt, xs, w_gate_up[l], b_gate_up[l].reshape(N_EXPERTS, 1, -1), w_down[l],
                 b_down[l].reshape(N_EXPERTS, 1, -1), TM_MOE)
        h = _combine(pos8, h2.reshape(n, d), meta.reshape(n, LANES), y, row(g_final),
                     l == depth - 1).reshape(b, s, d)
    return h
```

```python
import functools

import numpy as np
import jax
import jax.numpy as jnp
from jax import lax
from jax.experimental import pallas as pl
from jax.experimental.pallas import tpu as pltpu

F32 = jnp.float32
BF16 = jnp.bfloat16
I32 = jnp.int32

LANES = 128
SUBLANES = 8
RMS_EPS = 1e-6
CHUNK = 64
FOX_HEADS = 8
FOX_HEAD_DIM = 64
FOX_WIDTH = FOX_HEADS * FOX_HEAD_DIM
MLA_HEADS = 8
MLA_Q_LORA = 384
MLA_KV_LORA = 256
MLA_NOPE_DIM = 64
MLA_ROPE_DIM = 32
MLA_V_DIM = 64
MLA_WIDTH = MLA_HEADS * MLA_V_DIM
ROPE_BASE = 10000.0
MEM_HEADS = 4
MEM_HEAD_DIM = 128
MEM_WIDTH = MEM_HEADS * MEM_HEAD_DIM
N_EXPERTS = 32
TOP_K = 4
SWIGLU_LIMIT = 7.0
SWIGLU_ALPHA = 1.702

LOG2E = 1.4426950408889634
NEG = -1e30
FOX_QSCALE = FOX_HEAD_DIM ** -0.5 * LOG2E
MLA_QSCALE = (MLA_NOPE_DIM + MLA_ROPE_DIM) ** -0.5 * LOG2E
MEM_QSCALE = MEM_HEAD_DIM ** -0.5 * LOG2E
FOX_DECAY_LANES = 16
META_ROWS = 16

VMEM_LIMIT = 56 * 1024 * 1024

TS_PROJ = 512
TS_MEM = 512
T_ATT = 512
DECAY_BLK = 256
TM_MOE = 512
TD_ROWS = 512
MOE_CH = 512


def _cparams(n_axes):
    return pltpu.CompilerParams(dimension_semantics=("arbitrary",) * n_axes, vmem_limit_bytes=VMEM_LIMIT)


def _dot(a, b):
    return jnp.dot(a, b, preferred_element_type=F32)


def _dot_nt(a, b):
    return lax.dot_general(a, b, (((1,), (1,)), ((), ())), preferred_element_type=F32)


def _rms(x, g):
    return x * lax.rsqrt(jnp.mean(x * x, axis=-1, keepdims=True) + RMS_EPS) * g


def _split3(x):
    hi = x.astype(BF16)
    r1 = x - hi.astype(F32)
    mid = r1.astype(BF16)
    lo = (r1 - mid.astype(F32)).astype(BF16)
    return hi, mid, lo


def _interleave_blocks(a, b):
    parts = []
    for p in range(a.shape[1] // LANES):
        parts += [a[:, p * LANES:(p + 1) * LANES], b[:, p * LANES:(p + 1) * LANES]]
    return jnp.concatenate(parts, axis=1)


def _inproj_kernel(x_ref, g_ref, w1_ref, gq_ref, wqn_ref, wqa_ref, wqb_ref, gkv_ref, wk_ref, wv_ref, cos_ref, sin_ref,
                   fq_ref, fk_ref, fv_ref, fl_ref, qm_ref, km_ref, vm_ref):
    a = _rms(x_ref[0], g_ref[...]).astype(BF16)

    def proj(lo, hi):
        return _dot(a, w1_ref[:, lo:hi])

    c0 = 3 * FOX_WIDTH
    c1 = c0 + MLA_Q_LORA
    c2 = c1 + MLA_KV_LORA
    fq_ref[0] = (proj(0, FOX_WIDTH) * FOX_QSCALE).astype(BF16)
    fk_ref[0] = proj(FOX_WIDTH, 2 * FOX_WIDTH).astype(BF16)
    fv_ref[0] = proj(2 * FOX_WIDTH, c0).astype(BF16)
    qn = _rms(proj(c0, c1), gq_ref[...]).astype(BF16)
    kvn = _rms(proj(c1, c2), gkv_ref[...]).astype(BF16)
    fl_ref[0] = proj(c2, c2 + LANES)
    cos = cos_ref[...]
    sin = sin_ref[...]
    pairs = MLA_HEADS // 2
    kpe = (proj(c2 + LANES, c2 + 2 * LANES) * cos + proj(c2 + 2 * LANES, c2 + 3 * LANES) * sin).astype(BF16)
    cos4 = jnp.concatenate([cos] * pairs, axis=1)
    sin4 = jnp.concatenate([sin] * pairs, axis=1)
    q_nope = (_dot(qn, wqn_ref[...]) * MLA_QSCALE).astype(BF16)
    q_rope = ((_dot(qn, wqa_ref[...]) * cos4 + _dot(qn, wqb_ref[...]) * sin4) * MLA_QSCALE).astype(BF16)
    qm_ref[0] = _interleave_blocks(q_nope, q_rope)
    k_nope = _dot(kvn, wk_ref[...]).astype(BF16)
    km_ref[0] = _interleave_blocks(k_nope, jnp.concatenate([kpe] * pairs, axis=1))
    vm_ref[0] = _dot(kvn, wv_ref[...]).astype(BF16)


def _inproj(x, g_mix, w1, g_q, wqn, wqa, wqb, g_kv, wk, wv, cos_t, sin_t):
    b, s, d = x.shape
    ts = min(TS_PROJ, s)

    def full(arr):
        return pl.BlockSpec(arr.shape, lambda bi, si: (0,) * arr.ndim)

    def tok(width):
        return pl.BlockSpec((1, ts, width), lambda bi, si: (bi, si, 0))

    tab = pl.BlockSpec((ts, LANES), lambda bi, si: (si, 0))
    pair_w = 2 * LANES * (MLA_HEADS // 2)
    out_shapes = (
        jax.ShapeDtypeStruct((b, s, FOX_WIDTH), BF16),
        jax.ShapeDtypeStruct((b, s, FOX_WIDTH), BF16),
        jax.ShapeDtypeStruct((b, s, FOX_WIDTH), BF16),
        jax.ShapeDtypeStruct((b, s, LANES), F32),
        jax.ShapeDtypeStruct((b, s, pair_w), BF16),
        jax.ShapeDtypeStruct((b, s, pair_w), BF16),
        jax.ShapeDtypeStruct((b, s, MLA_WIDTH), BF16),
    )
    return pl.pallas_call(
        _inproj_kernel,
        grid=(b, s // ts),
        in_specs=[tok(d), full(g_mix), full(w1), full(g_q), full(wqn), full(wqa), full(wqb), full(g_kv), full(wk),
                  full(wv), tab, tab],
        out_specs=(tok(FOX_WIDTH), tok(FOX_WIDTH), tok(FOX_WIDTH), tok(LANES), tok(pair_w), tok(pair_w),
                   tok(MLA_WIDTH)),
        out_shape=out_shapes,
        compiler_params=_cparams(2),
        name="inproj",
    )(x, g_mix, w1, g_q, wqn, wqa, wqb, g_kv, wk, wv, cos_t, sin_t)


def _decay_kernel(fl_ref, bf_ref, pq_ref, pk_ref, oq_ref, ok_ref, cq_ref, ck_ref, *, blk):
    s = fl_ref.shape[1]
    row = lax.broadcasted_iota(I32, (blk, blk), 0)
    col = lax.broadcasted_iota(I32, (blk, blk), 1)
    tri = jnp.where(row >= col, 1.0, 0.0).astype(BF16)
    carry = jnp.zeros((1, LANES), F32)
    for i in range(s // blk):
        sl = slice(i * blk, (i + 1) * blk)
        z = fl_ref[0, sl, :] + bf_ref[...]
        lf = (jnp.minimum(z, 0.0) - jnp.log1p(jnp.exp(-jnp.abs(z)))) * LOG2E
        h, m, l = _split3(lf)
        cs = _dot(tri, h) + _dot(tri, m) + _dot(tri, l) + carry
        carry = cs[blk - 1:blk, :]
        h, m, l = _split3(cs)
        cq = _dot(h, pq_ref[0]) + _dot(m, pq_ref[1]) + _dot(l, pq_ref[2]) + oq_ref[...]
        ck = _dot(h, pk_ref[0]) + _dot(m, pk_ref[1]) + _dot(l, pk_ref[2]) + ok_ref[...]
        cq_ref[0, sl, :] = cq.astype(BF16)
        ck_ref[0, sl, :] = ck.astype(BF16)


def _decay_tables():
    pq = np.zeros((3, LANES, LANES), np.float32)
    pk = np.zeros((3, LANES, LANES), np.float32)
    oq = np.zeros((1, LANES), np.float32)
    ok = np.zeros((1, LANES), np.float32)
    for h in range(FOX_HEADS):
        for part in range(3):
            pq[part, h, FOX_DECAY_LANES * h + part] = 1.0
            pk[part, h, FOX_DECAY_LANES * h + 3 + part] = -1.0
            oq[0, FOX_DECAY_LANES * h + 3 + part] = 1.0
            ok[0, FOX_DECAY_LANES * h + part] = 1.0
    return jnp.asarray(pq, BF16), jnp.asarray(pk, BF16), jnp.asarray(oq), jnp.asarray(ok)


def _decay(flog, b_f):
    b, s, _ = flog.shape
    blk = min(DECAY_BLK, s)
    pq, pk, oq, ok = _decay_tables()
    bf = jnp.zeros((1, LANES), F32).at[0, :FOX_HEADS].set(b_f)

    def full(arr):
        return pl.BlockSpec(arr.shape, lambda bi: (0,) * arr.ndim)

    seq = pl.BlockSpec((1, s, LANES), lambda bi: (bi, 0, 0))
    return pl.pallas_call(
        functools.partial(_decay_kernel, blk=blk),
        grid=(b,),
        in_specs=[seq, full(bf), full(pq), full(pk), full(oq), full(ok)],
        out_specs=(seq, seq),
        out_shape=(jax.ShapeDtypeStruct((b, s, LANES), BF16), jax.ShapeDtypeStruct((b, s, LANES), BF16)),
        compiler_params=_cparams(1),
        name="decay",
    )(flog, bf, pq, pk, oq, ok)


def _attn_pipe_kernel(*refs, fox, t, nq, pairs):
    if fox:
        q_ref, cq_ref, k_ref, ck_ref, v_ref, o_ref, s_even, s_odd = refs
        group = FOX_DECAY_LANES
    else:
        q_ref, k_ref, v_ref, o_ref, s_even, s_odd = refs
        group = MLA_ROPE_DIM
    g = pl.program_id(0)
    i = g % nq
    pair = (jnp.minimum(g, pl.num_programs(0) - 2) // nq) % pairs
    base = 2 * pair if fox else 0
    lane = lax.broadcasted_iota(I32, (1, LANES), 1)

    @pl.when(g == 0)
    def _():
        s_odd[...] = jnp.zeros_like(s_odd)

    def body(iv):
        s_new, s_old = (s_even, s_odd) if iv % 2 == 0 else (s_odd, s_even)
        half = t // 2
        row = lax.broadcasted_iota(I32, (half, half), 0)
        col = lax.broadcasted_iota(I32, (half, half), 1)
        allowed = (col <= row) if fox else ((col // CHUNK) <= (row // CHUNK))
        past = iv * t
        if fox:
            q_main, q_extra = q_ref[0], cq_ref[0]
        else:
            q_main, q_extra = q_ref[0, :, :LANES], q_ref[0, :, LANES:]
        zero = jnp.zeros_like(q_main)

        def keys(lo, hi):
            if fox:
                return jnp.concatenate([k_ref[0, lo:hi, :], ck_ref[0, lo:hi, :]], axis=1)
            return k_ref[0, lo:hi, :]

        qa = [jnp.concatenate([jnp.where((lane // (LANES // 2)) == hh, q_main, zero),
                               jnp.where((lane // group) == base + hh, q_extra, zero)], axis=1) for hh in range(2)]
        for hh in range(2):
            if iv > 0:
                s_new[hh, :, 0:past] = _dot_nt(qa[hh], keys(0, past))
            upper = _dot_nt(qa[hh][:half], keys(past, past + half))
            s_new[hh, :half, past:past + half] = jnp.where(allowed, upper, NEG)
            lower = _dot_nt(qa[hh][half:], keys(past, past + t))
            s_new[hh, half:, past:past + half] = lower[:, :half]
            s_new[hh, half:, past + half:past + t] = jnp.where(allowed, lower[:, half:], NEG)

        seen = ((iv - 1) % nq + 1) * t
        outs = []
        for hh in range(2):
            sc = s_old[hh, :, 0:seen - half]
            sc_tail = s_old[hh, half:, seen - half:seen]
            m_main = jnp.max(sc, axis=-1, keepdims=True)
            m_low = jnp.maximum(m_main[half:], jnp.max(sc_tail, axis=-1, keepdims=True))
            p = jnp.exp2(sc - jnp.concatenate([m_main[:half], m_low], axis=0))
            p_tail = jnp.exp2(sc_tail - m_low)
            l_main = jnp.sum(p, axis=-1, keepdims=True)
            l = jnp.concatenate([l_main[:half], l_main[half:] + jnp.sum(p_tail, axis=-1, keepdims=True)], axis=0)
            acc = _dot(p.astype(BF16), v_ref[0, 0:seen - half, :])
            acc_tail = _dot(p_tail.astype(BF16), v_ref[0, seen - half:seen, :])
            acc = jnp.concatenate([acc[:half], acc[half:] + acc_tail], axis=0)
            outs.append(acc * (1.0 / l))
        o_ref[0] = jnp.where(lane < LANES // 2, outs[0], outs[1]).astype(o_ref.dtype)

    for iv in range(nq):
        pl.when(i == iv)(functools.partial(body, iv))


def _attention_pipelined(fox, q, k, v, cq=None, ck=None):
    b, s, _ = v.shape
    t = min(T_ATT, s)
    nq = s // t
    assert nq % 2 == 0
    pairs = v.shape[2] // LANES
    qw = q.shape[2] // pairs
    kw = k.shape[2] // pairs
    items = b * pairs * nq

    def item(g):
        return g // (pairs * nq), (g // nq) % pairs, g % nq

    def cur(g):
        return item(jnp.minimum(g, items - 1))

    def prev(g):
        return item(jnp.maximum(g - 1, 0))

    qspec = pl.BlockSpec((1, t, qw), lambda g: (cur(g)[0], cur(g)[2], cur(g)[1]))
    kspec = pl.BlockSpec((1, s, kw), lambda g: (cur(g)[0], 0, cur(g)[1]))
    vspec = pl.BlockSpec((1, s, LANES), lambda g: (prev(g)[0], 0, prev(g)[1]))
    ospec = pl.BlockSpec((1, t, LANES), lambda g: (prev(g)[0], prev(g)[2], prev(g)[1]))
    if fox:
        cqspec = pl.BlockSpec((1, t, LANES), lambda g: (cur(g)[0], cur(g)[2], 0))
        ckspec = pl.BlockSpec((1, s, LANES), lambda g: (cur(g)[0], 0, 0))
        in_specs = [qspec, cqspec, kspec, ckspec, vspec]
        args = (q, cq, k, ck, v)
    else:
        in_specs = [qspec, kspec, vspec]
        args = (q, k, v)
    return pl.pallas_call(
        functools.partial(_attn_pipe_kernel, fox=fox, t=t, nq=nq, pairs=pairs),
        grid=(items + 1,),
        in_specs=in_specs,
        out_specs=ospec,
        out_shape=jax.ShapeDtypeStruct((b, s, pairs * LANES), BF16),
        scratch_shapes=[pltpu.VMEM((2, t, s), F32), pltpu.VMEM((2, t, s), F32)],
        compiler_params=_cparams(1),
        name="fox_attn" if fox else "mla_attn",
    )(*args)


def _memkv_kernel(mem_ref, g_ref, w_ref, k_ref, v_ref):
    a = _rms(mem_ref[0], g_ref[...]).astype(BF16)
    k_ref[0] = _dot(a, w_ref[:, 0:MEM_WIDTH]).astype(BF16)
    v_ref[0] = _dot(a, w_ref[:, MEM_WIDTH:2 * MEM_WIDTH]).astype(BF16)


def _memkv(mem, g, w):
    b, m, d = mem.shape
    kv = pl.BlockSpec((1, m, MEM_WIDTH), lambda bi: (bi, 0, 0))
    return pl.pallas_call(
        _memkv_kernel,
        grid=(b,),
        in_specs=[pl.BlockSpec((1, m, d), lambda bi: (bi, 0, 0)), pl.BlockSpec(g.shape, lambda bi: (0, 0)),
                  pl.BlockSpec(w.shape, lambda bi: (0, 0))],
        out_specs=(kv, kv),
        out_shape=(jax.ShapeDtypeStruct((b, m, MEM_WIDTH), BF16), jax.ShapeDtypeStruct((b, m, MEM_WIDTH), BF16)),
        compiler_params=_cparams(1),
        name="memkv",
    )(mem, g, w)


def _store_token_tiles(ref, val):
    rows = val.shape[0]
    for j in range(SUBLANES):
        ref[pl.ds(j, rows, stride=SUBLANES), :] = val[:, j * LANES:(j + 1) * LANES]


def _load_token_tiles(ref, rows):
    return jnp.concatenate([ref[pl.ds(j, rows, stride=SUBLANES), :] for j in range(SUBLANES)], axis=1)


def _memrouter_kernel(of_ref, om_ref, x_ref, gf_ref, gm_ref, wmix_ref, gq_ref, wq_ref, k_ref, v_ref, wo_ref,
                      gffn_ref, wrh_ref, wrl_ref, br_ref, tri_ref, h2_ref, at_ref, meta_ref, metat_ref, cnt_ref, carry_ref,
                      *, ts):
    first = jnp.logical_and(pl.program_id(0) == 0, pl.program_id(1) == 0)

    @pl.when(first)
    def _():
        carry_ref[...] = jnp.zeros_like(carry_ref)

    nf = _rms(of_ref[0].astype(F32), gf_ref[...]).astype(BF16)
    nm = _rms(om_ref[0].astype(F32), gm_ref[...]).astype(BF16)
    h1 = x_ref[0] + _dot(nf, wmix_ref[0:FOX_WIDTH, :]) + _dot(nm, wmix_ref[FOX_WIDTH:FOX_WIDTH + MLA_WIDTH, :])

    q = (_dot(_rms(h1, gq_ref[...]).astype(BF16), wq_ref[...]) * MEM_QSCALE).astype(BF16)
    heads = []
    for h in range(MEM_HEADS):
        sl = slice(h * MEM_HEAD_DIM, (h + 1) * MEM_HEAD_DIM)
        sc = _dot_nt(q[:, sl], k_ref[0, :, sl])
        m = jnp.max(sc, axis=-1, keepdims=True)
        pm = jnp.exp2(sc - m)
        l = jnp.sum(pm, axis=-1, keepdims=True)
        heads.append((_dot(pm.astype(BF16), v_ref[0, :, sl]) / l).astype(BF16))
    h2 = h1 + _dot(jnp.concatenate(heads, axis=1), wo_ref[...])
    h2_ref[0] = h2

    a = _rms(h2, gffn_ref[...])
    _store_token_tiles(at_ref, a)
    a_hi = a.astype(BF16)
    a_lo = (a - a_hi.astype(F32)).astype(BF16)
    logits = _dot(a_hi, wrh_ref[...]) + _dot(a_lo, wrh_ref[...]) + _dot(a_hi, wrl_ref[...]) + br_ref[...]
    lane = lax.broadcasted_iota(I32, (ts, LANES), 1)
    lane_f = lane.astype(F32)
    work = jnp.where(lane < N_EXPERTS, logits, NEG)
    vals, idxs, sels = [], [], []
    for _ in range(TOP_K):
        mk = jnp.max(work, axis=-1, keepdims=True)
        ik = jnp.min(jnp.where(work == mk, lane_f, float(LANES)), axis=-1, keepdims=True)
        sel = lane_f == ik
        work = jnp.where(sel, NEG, work)
        vals.append(mk)
        idxs.append(ik)
        sels.append(sel)
    exps = [jnp.exp(vk - vals[0]) for vk in vals]
    denom = exps[0] + exps[1] + exps[2] + exps[3]
    chosen = jnp.logical_or(jnp.logical_or(sels[0], sels[1]), jnp.logical_or(sels[2], sels[3]))
    onehot = jnp.where(chosen, 1.0, 0.0)
    carry = carry_ref[...]
    rank = _dot(tri_ref[...], onehot.astype(BF16)) + carry
    carry_new = carry + jnp.sum(onehot, axis=0, keepdims=True)
    carry_ref[...] = carry_new
    cnt_ref[...] = carry_new
    meta = jnp.zeros((ts, LANES), F32)
    for kk in range(TOP_K):
        rk = jnp.sum(jnp.where(sels[kk], rank, 0.0), axis=-1, keepdims=True)
        meta = jnp.where(lane == kk, idxs[kk], meta)
        meta = jnp.where(lane == TOP_K + kk, rk, meta)
        meta = jnp.where(lane == 2 * TOP_K + kk, exps[kk] / denom, meta)
    meta_ref[0] = meta
    metat_ref[...] = jnp.transpose(meta)[0:META_ROWS, :]


def _memrouter(o_fox, o_mla, x, g_fox, g_mla, w_mix, g_mq, w_mq, kmem, vmem, w_mo, g_ffn, wr_hi, wr_lo, b_r):
    b, s, d = x.shape
    m = kmem.shape[1]
    ts = min(TS_MEM, s)

    def full(arr):
        return pl.BlockSpec(arr.shape, lambda bi, si: (0,) * arr.ndim)

    def tok(width):
        return pl.BlockSpec((1, ts, width), lambda bi, si: (bi, si, 0))

    kv = pl.BlockSpec((1, m, MEM_WIDTH), lambda bi, si: (bi, 0, 0))
    cnt = pl.BlockSpec((1, LANES), lambda bi, si: (0, 0))
    tri = jnp.asarray(np.tril(np.ones((ts, ts), np.float32), k=-1), BF16)
    return pl.pallas_call(
        functools.partial(_memrouter_kernel, ts=ts),
        grid=(b, s // ts),
        in_specs=[tok(FOX_WIDTH), tok(MLA_WIDTH), tok(d), full(g_fox), full(g_mla), full(w_mix), full(g_mq),
                  full(w_mq), kv, kv, full(w_mo), full(g_ffn), full(wr_hi), full(wr_lo), full(b_r), full(tri)],
        out_specs=(tok(d), pl.BlockSpec((ts * SUBLANES, LANES), lambda bi, si: (bi * (s // ts) + si, 0)),
                   tok(LANES), pl.BlockSpec((META_ROWS, ts), lambda bi, si: (0, bi * (s // ts) + si)), cnt),
        out_shape=(jax.ShapeDtypeStruct((b, s, d), F32), jax.ShapeDtypeStruct((b * s * SUBLANES, LANES), F32),
                   jax.ShapeDtypeStruct((b, s, LANES), F32), jax.ShapeDtypeStruct((META_ROWS, b * s), F32),
                   jax.ShapeDtypeStruct((1, LANES), F32)),
        scratch_shapes=[pltpu.VMEM((1, LANES), F32)],
        compiler_params=_cparams(2),
        name="memrouter",
    )(o_fox, o_mla, x, g_fox, g_mla, w_mix, g_mq, w_mq, kmem, vmem, w_mo, g_ffn, wr_hi, wr_lo, b_r, tri)


def _tile_copy(src_ref, src_row8, dst_ref, dst_row8, sem):
    return pltpu.make_async_copy(src_ref.at[pl.ds(pl.multiple_of(src_row8, SUBLANES), SUBLANES), :],
                                 dst_ref.at[pl.ds(pl.multiple_of(dst_row8, SUBLANES), SUBLANES), :], sem)


def _dispatch_kernel(last_ref, cnt_ref, nv_ref, pos_ref, at_ref, xs_ref, zbuf, zsem, sem, *, td, tm, min_used,
                     n_tiles):
    @pl.when(pl.program_id(0) == 0)
    def _():
        zbuf[...] = jnp.zeros_like(zbuf)

        def fill(row):
            return pltpu.make_async_copy(
                zbuf, xs_ref.at[pl.ds(pl.multiple_of(row * SUBLANES, SUBLANES), tm * SUBLANES), :], zsem)

        fills = [(cnt_ref[e] > 0, last_ref[e]) for e in range(N_EXPERTS)]
        fills += [(t >= nv_ref[0], t * tm) for t in range(min_used, n_tiles)]
        for pred, row in fills:
            pl.when(pred)(lambda row=row: fill(row).start())
        for pred, row in fills:
            pl.when(pred)(lambda row=row: fill(row).wait())

    def issue(r, c):
        for kk in range(TOP_K):
            _tile_copy(at_ref, r * SUBLANES, xs_ref, pos_ref[kk * td + r], sem).start(priority=kk % 2)
        return c

    lax.fori_loop(0, td, issue, 0, unroll=8)
    for _ in range(TOP_K):
        pltpu.make_async_copy(at_ref, xs_ref.at[pl.ds(0, td * SUBLANES), :], sem).wait()


def _dispatch(last_tile_row, cnt, nv, pos8, at, n_tiles, tm):
    n = at.shape[0] // SUBLANES
    td = min(TD_ROWS, n)
    grid_spec = pltpu.PrefetchScalarGridSpec(
        num_scalar_prefetch=3,
        grid=(n // td,),
        in_specs=[pl.BlockSpec((TOP_K * td,), lambda i, *_: (i,), memory_space=pltpu.SMEM),
                  pl.BlockSpec((td * SUBLANES, LANES), lambda i, *_: (i, 0))],
        out_specs=pl.BlockSpec(memory_space=pl.ANY),
        scratch_shapes=[pltpu.VMEM((tm * SUBLANES, LANES), F32), pltpu.SemaphoreType.DMA, pltpu.SemaphoreType.DMA],
    )
    min_used = n * TOP_K // tm
    return pl.pallas_call(
        functools.partial(_dispatch_kernel, td=td, tm=tm, min_used=min_used, n_tiles=n_tiles),
        grid_spec=grid_spec,
        out_shape=jax.ShapeDtypeStruct((n_tiles * tm * SUBLANES, LANES), F32),
        compiler_params=_cparams(1),
        name="dispatch",
    )(last_tile_row, cnt, nv, pos8, at)


def _moe_kernel(te_ref, rows_ref, slot_ref, nxt_ref, xs_ref, wgu_hbm, bgu_ref, wd_hbm, bd_ref, y_ref,
                wgu_f32, wd_f32, wgu_bf, wd_bf, sem, *, ch, tm):
    i = pl.program_id(0)
    d_exp = wd_hbm.shape[1]
    rows_used = rows_ref[i]
    valid = rows_used > 0
    new_expert = jnp.logical_or(i == 0, te_ref[i] != te_ref[jnp.maximum(i - 1, 0)])

    def weight_copies(e, sl):
        return (pltpu.make_async_copy(wgu_hbm.at[e], wgu_f32.at[sl], sem.at[sl, 0]),
                pltpu.make_async_copy(wd_hbm.at[e], wd_f32.at[sl], sem.at[sl, 1]))

    @pl.when(i == 0)
    def _():
        for c in weight_copies(te_ref[0], slot_ref[0]):
            c.start()

    @pl.when(jnp.logical_and(new_expert, valid))
    def _():
        sl = slot_ref[i]
        for c in weight_copies(te_ref[i], sl):
            c.wait()
        wgu_bf[...] = wgu_f32[sl].astype(BF16)
        wd_bf[...] = wd_f32[sl].astype(BF16)

        @pl.when(nxt_ref[i] >= 0)
        def _():
            for c in weight_copies(nxt_ref[i], 1 - sl):
                c.start()

    def ffn(rows):
        x = _load_token_tiles(xs_ref, rows).astype(BF16)
        acc = jnp.zeros((rows, wd_hbm.shape[2]), F32)
        for c in range(d_exp // ch):
            def gu(lo):
                return _dot(x, wgu_bf[:, lo:lo + ch]) + bgu_ref[0, :, lo:lo + ch]

            gate = jnp.minimum(gu(c * ch), SWIGLU_LIMIT)
            up = jnp.clip(gu(d_exp + c * ch), -SWIGLU_LIMIT, SWIGLU_LIMIT)
            act = gate * (1.0 / (1.0 + jnp.exp(-SWIGLU_ALPHA * gate))) * (up + 1.0)
            acc = acc + _dot(act.astype(BF16), wd_bf[c * ch:(c + 1) * ch, :])
        return acc + bd_ref[0]

    half = tm // 2

    @pl.when(rows_used > half)
    def _():
        _store_token_tiles(y_ref, ffn(tm))

    @pl.when(jnp.logical_and(valid, rows_used <= half))
    def _():
        _store_token_tiles(y_ref, ffn(half))
        y_ref[half * SUBLANES:, :] = jnp.zeros((half * SUBLANES, LANES), F32)

    @pl.when(jnp.logical_not(valid))
    def _():
        y_ref[...] = jnp.zeros_like(y_ref)


def _moe(te, rows, slot, nxt, xs, wgu, bgu, wd, bd, tm):
    n_tiles = xs.shape[0] // (tm * SUBLANES)
    _, d, d_exp2 = wgu.shape
    d_exp = wd.shape[1]
    assert d == SUBLANES * LANES and wd.shape[2] == d
    tile = pl.BlockSpec((tm * SUBLANES, LANES), lambda i, te, *_: (i, 0))
    grid_spec = pltpu.PrefetchScalarGridSpec(
        num_scalar_prefetch=4,
        grid=(n_tiles,),
        in_specs=[
            tile,
            pl.BlockSpec(memory_space=pl.ANY),
            pl.BlockSpec((1, 1, d_exp2), lambda i, te, *_: (te[i], 0, 0)),
            pl.BlockSpec(memory_space=pl.ANY),
            pl.BlockSpec((1, 1, d), lambda i, te, *_: (te[i], 0, 0)),
        ],
        out_specs=tile,
        scratch_shapes=[pltpu.VMEM((2, d, d_exp2), F32), pltpu.VMEM((2, d_exp, d), F32),
                        pltpu.VMEM((d, d_exp2), BF16), pltpu.VMEM((d_exp, d), BF16),
                        pltpu.SemaphoreType.DMA((2, 2))],
    )
    return pl.pallas_call(
        functools.partial(_moe_kernel, ch=min(MOE_CH, d_exp), tm=tm),
        grid_spec=grid_spec,
        out_shape=jax.ShapeDtypeStruct(xs.shape, F32),
        compiler_params=_cparams(1),
        name="moe",
    )(te, rows, slot, nxt, xs, wgu, bgu, wd, bd)


def _combine_kernel(pos_ref, posn_ref, h2_ref, meta_ref, y_ref, gf_ref, o_ref, ybuf, sem, *, td, final_norm):
    i = pl.program_id(0)
    slot = i % 2

    def issue_block(p_ref, sl):
        def issue(r, c):
            for kk in range(TOP_K):
                _tile_copy(y_ref, p_ref[kk * td + r], ybuf.at[sl, kk], r * SUBLANES,
                           sem.at[sl]).start(priority=kk % 2)
            return c

        lax.fori_loop(0, td, issue, 0, unroll=8)

    @pl.when(i == 0)
    def _():
        issue_block(pos_ref, 0)

    @pl.when(i + 1 < pl.num_programs(0))
    def _():
        issue_block(posn_ref, 1 - slot)

    for kk in range(TOP_K):
        pltpu.make_async_copy(y_ref.at[pl.ds(0, td * SUBLANES), :], ybuf.at[slot, kk], sem.at[slot]).wait()

    meta = meta_ref[...]
    acc = h2_ref[...]
    for kk in range(TOP_K):
        gate = meta[:, 2 * TOP_K + kk:2 * TOP_K + kk + 1]
        acc = acc + gate * _load_token_tiles(ybuf.at[slot, kk], td)
    o_ref[...] = _rms(acc, gf_ref[...]) if final_norm else acc


def _combine(pos8, h2, meta, y, g_final, final_norm):
    n, d = h2.shape
    td = min(TD_ROWS, n)
    n_blocks = n // td
    grid_spec = pltpu.PrefetchScalarGridSpec(
        num_scalar_prefetch=0,
        grid=(n_blocks,),
        in_specs=[pl.BlockSpec((TOP_K * td,), lambda i: (i,), memory_space=pltpu.SMEM),
                  pl.BlockSpec((TOP_K * td,), lambda i: (jnp.minimum(i + 1, n_blocks - 1),), memory_space=pltpu.SMEM),
                  pl.BlockSpec((td, d), lambda i: (i, 0)),
                  pl.BlockSpec((td, LANES), lambda i: (i, 0)),
                  pl.BlockSpec(memory_space=pl.ANY),
                  pl.BlockSpec(g_final.shape, lambda i: (0, 0))],
        out_specs=pl.BlockSpec((td, d), lambda i: (i, 0)),
        scratch_shapes=[pltpu.VMEM((2, TOP_K, td * SUBLANES, LANES), F32), pltpu.SemaphoreType.DMA((2,))],
    )
    return pl.pallas_call(
        functools.partial(_combine_kernel, td=td, final_norm=final_norm),
        grid_spec=grid_spec,
        out_shape=jax.ShapeDtypeStruct((n, d), F32),
        compiler_params=_cparams(1),
        name="combine",
    )(pos8, pos8, h2, meta, y, g_final)


def _rope_tables(seq_len):
    inv = 1.0 / (ROPE_BASE ** (jnp.arange(0, MLA_ROPE_DIM, 2, dtype=F32) / MLA_ROPE_DIM))
    ang = jnp.arange(seq_len, dtype=F32)[:, None] * inv[None, :]
    cos, sin = jnp.cos(ang), jnp.sin(ang)
    pad = jnp.zeros((seq_len, LANES - 2 * MLA_ROPE_DIM), F32)
    cos_t = jnp.concatenate([cos, cos, cos, cos, pad], axis=1)
    sin_t = jnp.concatenate([sin, sin, sin, sin, pad], axis=1)
    return cos_t, sin_t


def _rot_cols(w):
    half = MLA_ROPE_DIM // 2
    return jnp.concatenate([-w[:, half:], w[:, :half]], axis=1)


def _rope_pair_block(r0, r1):
    pad = jnp.zeros((r0.shape[0], LANES - 2 * MLA_ROPE_DIM), F32)
    return jnp.concatenate([r0, r1, pad], axis=1)


def _prep_inproj_weights(w_in, w_uq, w_ukv):
    d = w_in.shape[0]
    pts = np.cumsum((FOX_WIDTH, FOX_WIDTH, FOX_WIDTH, FOX_HEADS, MLA_Q_LORA, MLA_KV_LORA, MLA_ROPE_DIM))
    w_fq, w_fk, w_fv = w_in[:, :pts[0]], w_in[:, pts[0]:pts[1]], w_in[:, pts[1]:pts[2]]
    w_fl, w_ql = w_in[:, pts[2]:pts[3]], w_in[:, pts[3]:pts[4]]
    w_kvl, w_kr = w_in[:, pts[4]:pts[5]], w_in[:, pts[5]:pts[6]]
    misc = jnp.concatenate([w_fl, jnp.zeros((d, LANES - FOX_HEADS), F32)], axis=1)
    w1 = jnp.concatenate([w_fq, w_fk, w_fv, w_ql, w_kvl, misc, _rope_pair_block(w_kr, w_kr),
                          _rope_pair_block(_rot_cols(w_kr), _rot_cols(w_kr))], axis=1).astype(BF16)
    qd = MLA_NOPE_DIM + MLA_ROPE_DIM
    kvd = MLA_NOPE_DIM + MLA_V_DIM
    q_nope = [w_uq[:, h * qd:h * qd + MLA_NOPE_DIM] for h in range(MLA_HEADS)]
    q_rope = [w_uq[:, h * qd + MLA_NOPE_DIM:(h + 1) * qd] for h in range(MLA_HEADS)]
    wqa = [_rope_pair_block(q_rope[h], q_rope[h + 1]) for h in range(0, MLA_HEADS, 2)]
    wqb = [_rope_pair_block(_rot_cols(q_rope[h]), _rot_cols(q_rope[h + 1])) for h in range(0, MLA_HEADS, 2)]
    wk = [w_ukv[:, h * kvd:h * kvd + MLA_NOPE_DIM] for h in range(MLA_HEADS)]
    wv = [w_ukv[:, h * kvd + MLA_NOPE_DIM:(h + 1) * kvd] for h in range(MLA_HEADS)]
    cat = lambda xs: jnp.concatenate(xs, axis=1).astype(BF16)
    return w1, cat(q_nope), cat(wqa), cat(wqb), cat(wk), cat(wv)


def _routing_tables(counts, n_tiles, tm):
    tiles_e = (counts + tm - 1) // tm
    tile_end = jnp.cumsum(tiles_e)
    off = ((tile_end - tiles_e) * tm).astype(I32)
    total = tile_end[-1]
    ti = jnp.arange(n_tiles, dtype=I32)
    te = jnp.minimum(jnp.sum(ti[:, None] >= tile_end[None, :], axis=1), N_EXPERTS - 1).astype(I32)
    valid = ti < total
    last_e = jnp.sum(jnp.where(ti == total - 1, te, 0))
    te = jnp.where(valid, te, last_e).astype(I32)
    last_tile_row = (off + (tiles_e - 1) * tm).astype(I32)
    used = tiles_e > 0
    ids = jnp.arange(N_EXPERTS, dtype=I32)
    slot_e = (jnp.cumsum(used.astype(I32)) - 1) % 2
    later = jnp.where(used[None, :] & (ids[None, :] > ids[:, None]), ids[None, :], N_EXPERTS)
    nxt_e = jnp.min(later, axis=1)
    nxt_e = jnp.where(nxt_e == N_EXPERTS, -1, nxt_e).astype(I32)
    pick = te[:, None] == ids[None, :]
    slot = jnp.sum(jnp.where(pick, slot_e[None, :], 0), axis=1).astype(I32)
    nxt = jnp.sum(jnp.where(pick, nxt_e[None, :], 0), axis=1).astype(I32)
    end_row = jnp.sum(jnp.where(pick, (off + counts)[None, :], 0), axis=1)
    rows = jnp.where(valid, jnp.clip(end_row - ti * tm, 0, tm), 0).astype(I32)
    return off, last_tile_row, te, total.astype(I32).reshape(1), slot, nxt, rows


def kernel(x, mem, g_mix, w_in, b_f, g_q_lat, w_uq, g_kv_lat, w_ukv, g_fox_out, g_mla_out, w_o, g_mem_q, w_mem_q,
           g_mem_kv, w_mem_kv, w_mem_o, g_ffn, w_router, b_router, w_gate_up, b_gate_up, w_down, b_down, g_final):
    b, s, d = x.shape
    n = b * s
    depth = g_mix.shape[0]
    cos_t, sin_t = _rope_tables(s)
    row = lambda v: v.reshape(1, -1)
    h = x
    for l in range(depth):
        w1, wqn, wqa, wqb, wk, wv = _prep_inproj_weights(w_in[l], w_uq[l], w_ukv[l])
        fq, fk, fv, flog, qm, km, vm = _inproj(h, row(g_mix[l]), w1, row(g_q_lat[l]), wqn, wqa, wqb,
                                                row(g_kv_lat[l]), wk, wv, cos_t, sin_t)
        cq, ck = _decay(flog, b_f[l])
        o_fox = _attention_pipelined(True, fq, fk, fv, cq, ck)
        o_mla = _attention_pipelined(False, qm, km, vm)
        kmem, vmem = _memkv(mem, row(g_mem_kv[l]), w_mem_kv[l].astype(BF16))
        wr = jnp.zeros((d, LANES), F32).at[:, :N_EXPERTS].set(w_router[l])
        wr_hi = wr.astype(BF16)
        wr_lo = (wr - wr_hi.astype(F32)).astype(BF16)
        br = jnp.zeros((1, LANES), F32).at[0, :N_EXPERTS].set(b_router[l])
        h2, at, meta, meta_t, cnt = _memrouter(
            o_fox, o_mla, h, row(g_fox_out[l]), row(g_mla_out[l]), w_o[l].astype(BF16), row(g_mem_q[l]),
            w_mem_q[l].astype(BF16), kmem, vmem, w_mem_o[l].astype(BF16), row(g_ffn[l]), wr_hi, wr_lo, br)
        ek = meta_t[0:TOP_K].astype(I32)
        rk = meta_t[TOP_K:2 * TOP_K].astype(I32)
        n_tiles = n * TOP_K // TM_MOE + N_EXPERTS
        counts = cnt[0, :N_EXPERTS].astype(I32)
        off, last_tile_row, te, nv, slot, nxt, rows = _routing_tables(counts, n_tiles, TM_MOE)
        ids = jnp.arange(N_EXPERTS, dtype=I32)[:, None, None]
        off_of = jnp.sum(jnp.where(ek[None] == ids, off[:, None, None], 0), axis=0)
        pos8 = ((off_of + rk) * SUBLANES).astype(I32)
        td = min(TD_ROWS, n)
        pos8 = pos8.reshape(TOP_K, n // td, td).transpose(1, 0, 2).reshape(-1)
        xs = _dispatch(last_tile_row, counts, nv, pos8, at, n_tiles, TM_MOE)
        y = _moe(te, rows, slot, nxt, xs, w_gate_up[l], b_gate_up[l].reshape(N_EXPERTS, 1, -1), w_down[l],
                 b_down[l].reshape(N_EXPERTS, 1, -1), TM_MOE)
        h = _combine(pos8, h2.reshape(n, d), meta.reshape(n, LANES), y, row(g_final),
                     l == depth - 1).reshape(b, s, d)
    return h
```

```python
import functools

import numpy as np
import jax
import jax.numpy as jnp
from jax import lax
from jax.experimental import pallas as pl
from jax.experimental.pallas import tpu as pltpu

F32 = jnp.float32
BF16 = jnp.bfloat16
I32 = jnp.int32

LANES = 128
SUBLANES = 8
RMS_EPS = 1e-6
CHUNK = 64
FOX_HEADS = 8
FOX_HEAD_DIM = 64
FOX_WIDTH = FOX_HEADS * FOX_HEAD_DIM
MLA_HEADS = 8
MLA_Q_LORA = 384
MLA_KV_LORA = 256
MLA_NOPE_DIM = 64
MLA_ROPE_DIM = 32
MLA_V_DIM = 64
MLA_WIDTH = MLA_HEADS * MLA_V_DIM
ROPE_BASE = 10000.0
MEM_HEADS = 4
MEM_HEAD_DIM = 128
MEM_WIDTH = MEM_HEADS * MEM_HEAD_DIM
N_EXPERTS = 32
TOP_K = 4
SWIGLU_LIMIT = 7.0
SWIGLU_ALPHA = 1.702

LOG2E = 1.4426950408889634
NEG = -1e30
FOX_QSCALE = FOX_HEAD_DIM ** -0.5 * LOG2E
MLA_QSCALE = (MLA_NOPE_DIM + MLA_ROPE_DIM) ** -0.5 * LOG2E
MEM_QSCALE = MEM_HEAD_DIM ** -0.5 * LOG2E
FOX_DECAY_LANES = 16
META_ROWS = 16

VMEM_LIMIT = 56 * 1024 * 1024

TS_PROJ = 512
TS_MEM = 512
T_ATT = 512
DECAY_BLK = 256
TM_MOE = 512
MOE_ROW_GROUPS = 4
TD_ROWS = 512
MOE_CH = 512


def _cparams(n_axes):
    return pltpu.CompilerParams(dimension_semantics=("arbitrary",) * n_axes, vmem_limit_bytes=VMEM_LIMIT)


def _dot(a, b):
    return jnp.dot(a, b, preferred_element_type=F32)


def _dot_nt(a, b):
    return lax.dot_general(a, b, (((1,), (1,)), ((), ())), preferred_element_type=F32)


def _rms(x, g):
    return x * lax.rsqrt(jnp.mean(x * x, axis=-1, keepdims=True) + RMS_EPS) * g


def _split3(x):
    hi = x.astype(BF16)
    r1 = x - hi.astype(F32)
    mid = r1.astype(BF16)
    lo = (r1 - mid.astype(F32)).astype(BF16)
    return hi, mid, lo


def _interleave_blocks(a, b):
    parts = []
    for p in range(a.shape[1] // LANES):
        parts += [a[:, p * LANES:(p + 1) * LANES], b[:, p * LANES:(p + 1) * LANES]]
    return jnp.concatenate(parts, axis=1)


def _inproj_kernel(x_ref, g_ref, win_ref, wtail_ref, gq_ref, wqn_ref, wqa_ref, wqb_ref, gkv_ref, wk_ref, wv_ref,
                   cos_ref, sin_ref, fq_ref, fk_ref, fv_ref, fl_ref, qm_ref, km_ref, vm_ref):
    a = _rms(x_ref[0], g_ref[...]).astype(BF16)
    c0 = 3 * FOX_WIDTH
    c1 = c0 + MLA_Q_LORA
    c2 = c1 + MLA_KV_LORA

    def proj(lo, hi):
        if hi <= c0:
            return _dot(a, win_ref[:, lo:hi])
        return _dot(a, wtail_ref[:, lo - c0:hi - c0])

    fq_ref[0] = (proj(0, FOX_WIDTH) * FOX_QSCALE).astype(BF16)
    fk_ref[0] = proj(FOX_WIDTH, 2 * FOX_WIDTH).astype(BF16)
    fv_ref[0] = proj(2 * FOX_WIDTH, c0).astype(BF16)
    qn = _rms(proj(c0, c1), gq_ref[...]).astype(BF16)
    kvn = _rms(proj(c1, c2), gkv_ref[...]).astype(BF16)
    fl_ref[0] = proj(c2, c2 + LANES)
    cos = cos_ref[...]
    sin = sin_ref[...]
    pairs = MLA_HEADS // 2
    kpe = (proj(c2 + LANES, c2 + 2 * LANES) * cos + proj(c2 + 2 * LANES, c2 + 3 * LANES) * sin).astype(BF16)
    cos4 = jnp.concatenate([cos] * pairs, axis=1)
    sin4 = jnp.concatenate([sin] * pairs, axis=1)
    q_nope = (_dot(qn, wqn_ref[...]) * MLA_QSCALE).astype(BF16)
    q_rope = ((_dot(qn, wqa_ref[...]) * cos4 + _dot(qn, wqb_ref[...]) * sin4) * MLA_QSCALE).astype(BF16)
    qm_ref[0] = _interleave_blocks(q_nope, q_rope)
    k_nope = _dot(kvn, wk_ref[...]).astype(BF16)
    km_ref[0] = _interleave_blocks(k_nope, jnp.concatenate([kpe] * pairs, axis=1))
    vm_ref[0] = _dot(kvn, wv_ref[...]).astype(BF16)


def _inproj(x, g_mix, w_in_bf, w_tail, g_q, wqn, wqa, wqb, g_kv, wk, wv, cos_t, sin_t):
    b, s, d = x.shape
    ts = min(TS_PROJ, s)

    def full(arr):
        return pl.BlockSpec(arr.shape, lambda bi, si: (0,) * arr.ndim)

    def tok(width):
        return pl.BlockSpec((1, ts, width), lambda bi, si: (bi, si, 0))

    tab = pl.BlockSpec((ts, LANES), lambda bi, si: (si, 0))
    pair_w = 2 * LANES * (MLA_HEADS // 2)
    out_shapes = (
        jax.ShapeDtypeStruct((b, s, FOX_WIDTH), BF16),
        jax.ShapeDtypeStruct((b, s, FOX_WIDTH), BF16),
        jax.ShapeDtypeStruct((b, s, FOX_WIDTH), BF16),
        jax.ShapeDtypeStruct((b, s, LANES), F32),
        jax.ShapeDtypeStruct((b, s, pair_w), BF16),
        jax.ShapeDtypeStruct((b, s, pair_w), BF16),
        jax.ShapeDtypeStruct((b, s, MLA_WIDTH), BF16),
    )
    return pl.pallas_call(
        _inproj_kernel,
        grid=(b, s // ts),
        in_specs=[tok(d), full(g_mix), full(w_in_bf), full(w_tail), full(g_q), full(wqn), full(wqa), full(wqb),
                  full(g_kv), full(wk), full(wv), tab, tab],
        out_specs=(tok(FOX_WIDTH), tok(FOX_WIDTH), tok(FOX_WIDTH), tok(LANES), tok(pair_w), tok(pair_w),
                   tok(MLA_WIDTH)),
        out_shape=out_shapes,
        compiler_params=_cparams(2),
        name="inproj",
    )(x, g_mix, w_in_bf, w_tail, g_q, wqn, wqa, wqb, g_kv, wk, wv, cos_t, sin_t)


def _decay_kernel(fl_ref, bf_ref, place_ref, ones_ref, cq_ref, ck_ref, *, blk):
    s = fl_ref.shape[1]
    row = lax.broadcasted_iota(I32, (blk, blk), 0)
    col = lax.broadcasted_iota(I32, (blk, blk), 1)
    tri = jnp.where(row >= col, 1.0, 0.0).astype(BF16)
    carry = jnp.zeros((1, LANES), F32)
    for i in range(s // blk):
        sl = slice(i * blk, (i + 1) * blk)
        z = fl_ref[0, sl, :] + bf_ref[...]
        lf = (jnp.minimum(z, 0.0) - jnp.log1p(jnp.exp(-jnp.abs(z)))) * LOG2E
        part = _dot(tri, jnp.concatenate(_split3(lf), axis=1))
        cs = part[:, :LANES] + part[:, LANES:2 * LANES] + part[:, 2 * LANES:] + carry
        carry = cs[blk - 1:blk, :]
        placed = _dot(jnp.concatenate(_split3(cs), axis=1), place_ref[...]) + ones_ref[...]
        cq_ref[0, sl, :] = placed[:, :LANES].astype(BF16)
        ck_ref[0, sl, :] = placed[:, LANES:].astype(BF16)


def _decay_tables():
    place = np.zeros((3 * LANES, 2 * LANES), np.float32)
    ones = np.zeros((1, 2 * LANES), np.float32)
    for h in range(FOX_HEADS):
        for part in range(3):
            place[part * LANES + h, FOX_DECAY_LANES * h + part] = 1.0
            place[part * LANES + h, LANES + FOX_DECAY_LANES * h + 3 + part] = -1.0
            ones[0, FOX_DECAY_LANES * h + 3 + part] = 1.0
            ones[0, LANES + FOX_DECAY_LANES * h + part] = 1.0
    return jnp.asarray(place, BF16), jnp.asarray(ones)


def _decay(flog, b_f):
    b, s, _ = flog.shape
    blk = min(DECAY_BLK, s)
    place, ones = _decay_tables()
    bf = jnp.zeros((1, LANES), F32).at[0, :FOX_HEADS].set(b_f)

    def full(arr):
        return pl.BlockSpec(arr.shape, lambda bi: (0,) * arr.ndim)

    seq = pl.BlockSpec((1, s, LANES), lambda bi: (bi, 0, 0))
    return pl.pallas_call(
        functools.partial(_decay_kernel, blk=blk),
        grid=(b,),
        in_specs=[seq, full(bf), full(place), full(ones)],
        out_specs=(seq, seq),
        out_shape=(jax.ShapeDtypeStruct((b, s, LANES), BF16), jax.ShapeDtypeStruct((b, s, LANES), BF16)),
        compiler_params=_cparams(1),
        name="decay",
    )(flog, bf, place, ones)


def _attn_pipe_kernel(*refs, fox, t, nq, pairs):
    if fox:
        q_ref, cq_ref, k_ref, ck_ref, v_ref, o_ref, s_even, s_odd = refs
        group = FOX_DECAY_LANES
    else:
        q_ref, k_ref, v_ref, o_ref, s_even, s_odd = refs
        group = MLA_ROPE_DIM
    g = pl.program_id(0)
    i = g % nq
    pair = (jnp.minimum(g, pl.num_programs(0) - 2) // nq) % pairs
    base = 2 * pair if fox else 0
    lane = lax.broadcasted_iota(I32, (1, LANES), 1)

    @pl.when(g == 0)
    def _():
        s_odd[...] = jnp.zeros_like(s_odd)

    def body(iv):
        s_new, s_old = (s_even, s_odd) if iv % 2 == 0 else (s_odd, s_even)
        half = t // 2
        row = lax.broadcasted_iota(I32, (half, half), 0)
        col = lax.broadcasted_iota(I32, (half, half), 1)
        allowed = (col <= row) if fox else ((col // CHUNK) <= (row // CHUNK))
        past = iv * t
        if fox:
            q_main, q_extra = q_ref[0], cq_ref[0]
        else:
            q_main, q_extra = q_ref[0, :, :LANES], q_ref[0, :, LANES:]
        zero = jnp.zeros_like(q_main)

        def keys(lo, hi):
            if fox:
                return jnp.concatenate([k_ref[0, lo:hi, :], ck_ref[0, lo:hi, :]], axis=1)
            return k_ref[0, lo:hi, :]

        qa = [jnp.concatenate([jnp.where((lane // (LANES // 2)) == hh, q_main, zero),
                               jnp.where((lane // group) == base + hh, q_extra, zero)], axis=1) for hh in range(2)]
        for hh in range(2):
            if iv > 0:
                s_new[hh, :, 0:past] = _dot_nt(qa[hh], keys(0, past))
            upper = _dot_nt(qa[hh][:half], keys(past, past + half))
            s_new[hh, :half, past:past + half] = jnp.where(allowed, upper, NEG)
            lower = _dot_nt(qa[hh][half:], keys(past, past + t))
            s_new[hh, half:, past:past + half] = lower[:, :half]
            s_new[hh, half:, past + half:past + t] = jnp.where(allowed, lower[:, half:], NEG)

        seen = ((iv - 1) % nq + 1) * t
        outs = []
        for hh in range(2):
            sc = s_old[hh, :, 0:seen - half]
            sc_tail = s_old[hh, half:, seen - half:seen]
            m_main = jnp.max(sc, axis=-1, keepdims=True)
            m_low = jnp.maximum(m_main[half:], jnp.max(sc_tail, axis=-1, keepdims=True))
            p = jnp.exp2(sc - jnp.concatenate([m_main[:half], m_low], axis=0))
            p_tail = jnp.exp2(sc_tail - m_low)
            l_main = jnp.sum(p, axis=-1, keepdims=True)
            l = jnp.concatenate([l_main[:half], l_main[half:] + jnp.sum(p_tail, axis=-1, keepdims=True)], axis=0)
            acc = _dot(p.astype(BF16), v_ref[0, 0:seen - half, :])
            acc_tail = _dot(p_tail.astype(BF16), v_ref[0, seen - half:seen, :])
            acc = jnp.concatenate([acc[:half], acc[half:] + acc_tail], axis=0)
            outs.append(acc * (1.0 / l))
        o_ref[0] = jnp.where(lane < LANES // 2, outs[0], outs[1]).astype(o_ref.dtype)

    for iv in range(nq):
        pl.when(i == iv)(functools.partial(body, iv))


def _attention_pipelined(fox, q, k, v, cq=None, ck=None):
    b, s, _ = v.shape
    t = min(T_ATT, s)
    nq = s // t
    assert nq % 2 == 0
    pairs = v.shape[2] // LANES
    qw = q.shape[2] // pairs
    kw = k.shape[2] // pairs
    items = b * pairs * nq

    def item(g):
        return g // (pairs * nq), (g // nq) % pairs, g % nq

    def cur(g):
        return item(jnp.minimum(g, items - 1))

    def prev(g):
        return item(jnp.maximum(g - 1, 0))

    qspec = pl.BlockSpec((1, t, qw), lambda g: (cur(g)[0], cur(g)[2], cur(g)[1]))
    kspec = pl.BlockSpec((1, s, kw), lambda g: (cur(g)[0], 0, cur(g)[1]))
    vspec = pl.BlockSpec((1, s, LANES), lambda g: (prev(g)[0], 0, prev(g)[1]))
    ospec = pl.BlockSpec((1, t, LANES), lambda g: (prev(g)[0], prev(g)[2], prev(g)[1]))
    if fox:
        cqspec = pl.BlockSpec((1, t, LANES), lambda g: (cur(g)[0], cur(g)[2], 0))
        ckspec = pl.BlockSpec((1, s, LANES), lambda g: (cur(g)[0], 0, 0))
        in_specs = [qspec, cqspec, kspec, ckspec, vspec]
        args = (q, cq, k, ck, v)
    else:
        in_specs = [qspec, kspec, vspec]
        args = (q, k, v)
    return pl.pallas_call(
        functools.partial(_attn_pipe_kernel, fox=fox, t=t, nq=nq, pairs=pairs),
        grid=(items + 1,),
        in_specs=in_specs,
        out_specs=ospec,
        out_shape=jax.ShapeDtypeStruct((b, s, pairs * LANES), BF16),
        scratch_shapes=[pltpu.VMEM((2, t, s), F32), pltpu.VMEM((2, t, s), F32)],
        compiler_params=_cparams(1),
        name="fox_attn" if fox else "mla_attn",
    )(*args)


def _memkv_kernel(mem_ref, g_ref, w_ref, k_ref, v_ref):
    a = _rms(mem_ref[0], g_ref[...]).astype(BF16)
    k_ref[0] = _dot(a, w_ref[:, 0:MEM_WIDTH]).astype(BF16)
    v_ref[0] = _dot(a, w_ref[:, MEM_WIDTH:2 * MEM_WIDTH]).astype(BF16)


def _memkv(mem, g, w):
    b, m, d = mem.shape
    kv = pl.BlockSpec((1, m, MEM_WIDTH), lambda bi: (bi, 0, 0))
    return pl.pallas_call(
        _memkv_kernel,
        grid=(b,),
        in_specs=[pl.BlockSpec((1, m, d), lambda bi: (bi, 0, 0)), pl.BlockSpec(g.shape, lambda bi: (0, 0)),
                  pl.BlockSpec(w.shape, lambda bi: (0, 0))],
        out_specs=(kv, kv),
        out_shape=(jax.ShapeDtypeStruct((b, m, MEM_WIDTH), BF16), jax.ShapeDtypeStruct((b, m, MEM_WIDTH), BF16)),
        compiler_params=_cparams(1),
        name="memkv",
    )(mem, g, w)


def _store_token_tiles(ref, val):
    rows = val.shape[0]
    for j in range(SUBLANES):
        ref[pl.ds(j, rows, stride=SUBLANES), :] = val[:, j * LANES:(j + 1) * LANES]


def _load_token_tiles(ref, rows):
    return jnp.concatenate([ref[pl.ds(j, rows, stride=SUBLANES), :] for j in range(SUBLANES)], axis=1)


def _memrouter_kernel(of_ref, om_ref, x_ref, gf_ref, gm_ref, wmix_ref, gq_ref, wq_ref, k_ref, v_ref, wo_ref,
                      gffn_ref, wrh_ref, wrl_ref, br_ref, tri_ref, h2_ref, at_ref, meta_ref, metat_ref, cnt_ref, carry_ref,
                      *, ts):
    first = jnp.logical_and(pl.program_id(0) == 0, pl.program_id(1) == 0)

    @pl.when(first)
    def _():
        carry_ref[...] = jnp.zeros_like(carry_ref)

    nf = _rms(of_ref[0].astype(F32), gf_ref[...]).astype(BF16)
    nm = _rms(om_ref[0].astype(F32), gm_ref[...]).astype(BF16)
    h1 = x_ref[0] + _dot(nf, wmix_ref[0:FOX_WIDTH, :]) + _dot(nm, wmix_ref[FOX_WIDTH:FOX_WIDTH + MLA_WIDTH, :])

    q = (_dot(_rms(h1, gq_ref[...]).astype(BF16), wq_ref[...]) * MEM_QSCALE).astype(BF16)
    heads = []
    for h in range(MEM_HEADS):
        sl = slice(h * MEM_HEAD_DIM, (h + 1) * MEM_HEAD_DIM)
        sc = _dot_nt(q[:, sl], k_ref[0, :, sl])
        m = jnp.max(sc, axis=-1, keepdims=True)
        pm = jnp.exp2(sc - m)
        l = jnp.sum(pm, axis=-1, keepdims=True)
        heads.append((_dot(pm.astype(BF16), v_ref[0, :, sl]) / l).astype(BF16))
    h2 = h1 + _dot(jnp.concatenate(heads, axis=1), wo_ref[...])
    h2_ref[0] = h2

    a = _rms(h2, gffn_ref[...])
    _store_token_tiles(at_ref, a)
    a_hi = a.astype(BF16)
    a_lo = (a - a_hi.astype(F32)).astype(BF16)
    logits = _dot(a_hi, wrh_ref[...]) + _dot(a_lo, wrh_ref[...]) + _dot(a_hi, wrl_ref[...]) + br_ref[...]
    lane = lax.broadcasted_iota(I32, (ts, LANES), 1)
    lane_f = lane.astype(F32)
    work = jnp.where(lane < N_EXPERTS, logits, NEG)
    vals, idxs, sels = [], [], []
    for _ in range(TOP_K):
        mk = jnp.max(work, axis=-1, keepdims=True)
        ik = jnp.min(jnp.where(work == mk, lane_f, float(LANES)), axis=-1, keepdims=True)
        sel = lane_f == ik
        work = jnp.where(sel, NEG, work)
        vals.append(mk)
        idxs.append(ik)
        sels.append(sel)
    exps = [jnp.exp(vk - vals[0]) for vk in vals]
    denom = exps[0] + exps[1] + exps[2] + exps[3]
    chosen = jnp.logical_or(jnp.logical_or(sels[0], sels[1]), jnp.logical_or(sels[2], sels[3]))
    onehot = jnp.where(chosen, 1.0, 0.0)
    carry = carry_ref[...]
    rank = _dot(tri_ref[...], onehot.astype(BF16)) + carry
    carry_new = carry + jnp.sum(onehot, axis=0, keepdims=True)
    carry_ref[...] = carry_new
    cnt_ref[...] = carry_new
    meta = jnp.zeros((ts, LANES), F32)
    for kk in range(TOP_K):
        rk = jnp.sum(jnp.where(sels[kk], rank, 0.0), axis=-1, keepdims=True)
        meta = jnp.where(lane == kk, idxs[kk], meta)
        meta = jnp.where(lane == TOP_K + kk, rk, meta)
        meta = jnp.where(lane == 2 * TOP_K + kk, exps[kk] / denom, meta)
    meta_ref[0] = meta
    metat_ref[...] = jnp.transpose(meta)[0:META_ROWS, :]


def _memrouter(o_fox, o_mla, x, g_fox, g_mla, w_mix, g_mq, w_mq, kmem, vmem, w_mo, g_ffn, wr_hi, wr_lo, b_r):
    b, s, d = x.shape
    m = kmem.shape[1]
    ts = min(TS_MEM, s)

    def full(arr):
        return pl.BlockSpec(arr.shape, lambda bi, si: (0,) * arr.ndim)

    def tok(width):
        return pl.BlockSpec((1, ts, width), lambda bi, si: (bi, si, 0))

    kv = pl.BlockSpec((1, m, MEM_WIDTH), lambda bi, si: (bi, 0, 0))
    cnt = pl.BlockSpec((1, LANES), lambda bi, si: (0, 0))
    tri = jnp.asarray(np.tril(np.ones((ts, ts), np.float32), k=-1), BF16)
    return pl.pallas_call(
        functools.partial(_memrouter_kernel, ts=ts),
        grid=(b, s // ts),
        in_specs=[tok(FOX_WIDTH), tok(MLA_WIDTH), tok(d), full(g_fox), full(g_mla), full(w_mix), full(g_mq),
                  full(w_mq), kv, kv, full(w_mo), full(g_ffn), full(wr_hi), full(wr_lo), full(b_r), full(tri)],
        out_specs=(tok(d), pl.BlockSpec((ts * SUBLANES, LANES), lambda bi, si: (bi * (s // ts) + si, 0)),
                   tok(LANES), pl.BlockSpec((META_ROWS, ts), lambda bi, si: (0, bi * (s // ts) + si)), cnt),
        out_shape=(jax.ShapeDtypeStruct((b, s, d), F32), jax.ShapeDtypeStruct((b * s * SUBLANES, LANES), F32),
                   jax.ShapeDtypeStruct((b, s, LANES), F32), jax.ShapeDtypeStruct((META_ROWS, b * s), F32),
                   jax.ShapeDtypeStruct((1, LANES), F32)),
        scratch_shapes=[pltpu.VMEM((1, LANES), F32)],
        compiler_params=_cparams(2),
        name="memrouter",
    )(o_fox, o_mla, x, g_fox, g_mla, w_mix, g_mq, w_mq, kmem, vmem, w_mo, g_ffn, wr_hi, wr_lo, b_r, tri)


def _tile_copy(src_ref, src_row8, dst_ref, dst_row8, sem):
    return pltpu.make_async_copy(src_ref.at[pl.ds(pl.multiple_of(src_row8, SUBLANES), SUBLANES), :],
                                 dst_ref.at[pl.ds(pl.multiple_of(dst_row8, SUBLANES), SUBLANES), :], sem)


def _dispatch_kernel(last_ref, cnt_ref, nv_ref, pos_ref, at_ref, xs_ref, zbuf, zsem, sem, *, td, tm, min_used,
                     n_tiles):
    @pl.when(pl.program_id(0) == 0)
    def _():
        zbuf[...] = jnp.zeros_like(zbuf)

        def fill(row):
            return pltpu.make_async_copy(
                zbuf, xs_ref.at[pl.ds(pl.multiple_of(row * SUBLANES, SUBLANES), tm * SUBLANES), :], zsem)

        fills = [(cnt_ref[e] > 0, last_ref[e]) for e in range(N_EXPERTS)]
        fills += [(t >= nv_ref[0], t * tm) for t in range(min_used, n_tiles)]
        for pred, row in fills:
            pl.when(pred)(lambda row=row: fill(row).start())
        for pred, row in fills:
            pl.when(pred)(lambda row=row: fill(row).wait())

    def issue(r, c):
        for kk in range(TOP_K):
            _tile_copy(at_ref, r * SUBLANES, xs_ref, pos_ref[kk * td + r], sem).start(priority=kk % 2)
        return c

    lax.fori_loop(0, td, issue, 0, unroll=8)
    for _ in range(TOP_K):
        pltpu.make_async_copy(at_ref, xs_ref.at[pl.ds(0, td * SUBLANES), :], sem).wait()


def _dispatch(last_tile_row, cnt, nv, pos8, at, n_tiles, tm):
    n = at.shape[0] // SUBLANES
    td = min(TD_ROWS, n)
    grid_spec = pltpu.PrefetchScalarGridSpec(
        num_scalar_prefetch=3,
        grid=(n // td,),
        in_specs=[pl.BlockSpec((TOP_K * td,), lambda i, *_: (i,), memory_space=pltpu.SMEM),
                  pl.BlockSpec((td * SUBLANES, LANES), lambda i, *_: (i, 0))],
        out_specs=pl.BlockSpec(memory_space=pl.ANY),
        scratch_shapes=[pltpu.VMEM((tm * SUBLANES, LANES), F32), pltpu.SemaphoreType.DMA, pltpu.SemaphoreType.DMA],
    )
    min_used = n * TOP_K // tm
    return pl.pallas_call(
        functools.partial(_dispatch_kernel, td=td, tm=tm, min_used=min_used, n_tiles=n_tiles),
        grid_spec=grid_spec,
        out_shape=jax.ShapeDtypeStruct((n_tiles * tm * SUBLANES, LANES), F32),
        compiler_params=_cparams(1),
        name="dispatch",
    )(last_tile_row, cnt, nv, pos8, at)


def _moe_kernel(te_ref, rows_ref, slot_ref, nxt_ref, xs_ref, wgu_hbm, bgu_ref, wd_hbm, bd_ref, y_ref,
                wgu_f32, wd_f32, wgu_bf, wd_bf, sem, *, ch, tm):
    i = pl.program_id(0)
    d_exp = wd_hbm.shape[1]
    rows_used = rows_ref[i]
    valid = rows_used > 0
    new_expert = jnp.logical_or(i == 0, te_ref[i] != te_ref[jnp.maximum(i - 1, 0)])

    def weight_copies(e, sl):
        return (pltpu.make_async_copy(wgu_hbm.at[e], wgu_f32.at[sl], sem.at[sl, 0]),
                pltpu.make_async_copy(wd_hbm.at[e], wd_f32.at[sl], sem.at[sl, 1]))

    @pl.when(i == 0)
    def _():
        for c in weight_copies(te_ref[0], slot_ref[0]):
            c.start()

    @pl.when(jnp.logical_and(new_expert, valid))
    def _():
        sl = slot_ref[i]
        for c in weight_copies(te_ref[i], sl):
            c.wait()
        wgu_bf[...] = wgu_f32[sl].astype(BF16)
        wd_bf[...] = wd_f32[sl].astype(BF16)

        @pl.when(nxt_ref[i] >= 0)
        def _():
            for c in weight_copies(nxt_ref[i], 1 - sl):
                c.start()

    def ffn(rows):
        x = _load_token_tiles(xs_ref, rows).astype(BF16)
        acc = jnp.zeros((rows, wd_hbm.shape[2]), F32)
        for c in range(d_exp // ch):
            def gu(lo):
                return _dot(x, wgu_bf[:, lo:lo + ch]) + bgu_ref[0, :, lo:lo + ch]

            gate = jnp.minimum(gu(c * ch), SWIGLU_LIMIT)
            up = jnp.clip(gu(d_exp + c * ch), -SWIGLU_LIMIT, SWIGLU_LIMIT)
            act = gate * (1.0 / (1.0 + jnp.exp(-SWIGLU_ALPHA * gate))) * (up + 1.0)
            acc = acc + _dot(act.astype(BF16), wd_bf[c * ch:(c + 1) * ch, :])
        return acc + bd_ref[0]

    grp = tm // MOE_ROW_GROUPS
    for used in range(1, MOE_ROW_GROUPS + 1):
        rows = used * grp

        def partial_tile(rows=rows):
            _store_token_tiles(y_ref, ffn(rows))
            if rows < tm:
                y_ref[rows * SUBLANES:, :] = jnp.zeros(((tm - rows) * SUBLANES, LANES), F32)

        pl.when(jnp.logical_and(rows_used > rows - grp, rows_used <= rows))(partial_tile)

    @pl.when(jnp.logical_not(valid))
    def _():
        y_ref[...] = jnp.zeros_like(y_ref)


def _moe(te, rows, slot, nxt, xs, wgu, bgu, wd, bd, tm):
    n_tiles = xs.shape[0] // (tm * SUBLANES)
    _, d, d_exp2 = wgu.shape
    d_exp = wd.shape[1]
    assert d == SUBLANES * LANES and wd.shape[2] == d
    tile = pl.BlockSpec((tm * SUBLANES, LANES), lambda i, te, *_: (i, 0))
    grid_spec = pltpu.PrefetchScalarGridSpec(
        num_scalar_prefetch=4,
        grid=(n_tiles,),
        in_specs=[
            tile,
            pl.BlockSpec(memory_space=pl.ANY),
            pl.BlockSpec((1, 1, d_exp2), lambda i, te, *_: (te[i], 0, 0)),
            pl.BlockSpec(memory_space=pl.ANY),
            pl.BlockSpec((1, 1, d), lambda i, te, *_: (te[i], 0, 0)),
        ],
        out_specs=tile,
        scratch_shapes=[pltpu.VMEM((2, d, d_exp2), F32), pltpu.VMEM((2, d_exp, d), F32),
                        pltpu.VMEM((d, d_exp2), BF16), pltpu.VMEM((d_exp, d), BF16),
                        pltpu.SemaphoreType.DMA((2, 2))],
    )
    return pl.pallas_call(
        functools.partial(_moe_kernel, ch=min(MOE_CH, d_exp), tm=tm),
        grid_spec=grid_spec,
        out_shape=jax.ShapeDtypeStruct(xs.shape, F32),
        compiler_params=_cparams(1),
        name="moe",
    )(te, rows, slot, nxt, xs, wgu, bgu, wd, bd)


def _combine_kernel(pos_ref, posn_ref, h2_ref, meta_ref, y_ref, gf_ref, o_ref, ybuf, sem, *, td, final_norm):
    i = pl.program_id(0)
    slot = i % 2

    def issue_block(p_ref, sl):
        def issue(r, c):
            for kk in range(TOP_K):
                _tile_copy(y_ref, p_ref[kk * td + r], ybuf.at[sl, kk], r * SUBLANES,
                           sem.at[sl]).start(priority=kk % 2)
            return c

        lax.fori_loop(0, td, issue, 0, unroll=8)

    @pl.when(i == 0)
    def _():
        issue_block(pos_ref, 0)

    @pl.when(i + 1 < pl.num_programs(0))
    def _():
        issue_block(posn_ref, 1 - slot)

    for kk in range(TOP_K):
        pltpu.make_async_copy(y_ref.at[pl.ds(0, td * SUBLANES), :], ybuf.at[slot, kk], sem.at[slot]).wait()

    meta = meta_ref[...]
    acc = h2_ref[...]
    for kk in range(TOP_K):
        gate = meta[:, 2 * TOP_K + kk:2 * TOP_K + kk + 1]
        acc = acc + gate * _load_token_tiles(ybuf.at[slot, kk], td)
    o_ref[...] = _rms(acc, gf_ref[...]) if final_norm else acc


def _combine(pos8, h2, meta, y, g_final, final_norm):
    n, d = h2.shape
    td = min(TD_ROWS, n)
    n_blocks = n // td
    grid_spec = pltpu.PrefetchScalarGridSpec(
        num_scalar_prefetch=0,
        grid=(n_blocks,),
        in_specs=[pl.BlockSpec((TOP_K * td,), lambda i: (i,), memory_space=pltpu.SMEM),
                  pl.BlockSpec((TOP_K * td,), lambda i: (jnp.minimum(i + 1, n_blocks - 1),), memory_space=pltpu.SMEM),
                  pl.BlockSpec((td, d), lambda i: (i, 0)),
                  pl.BlockSpec((td, LANES), lambda i: (i, 0)),
                  pl.BlockSpec(memory_space=pl.ANY),
                  pl.BlockSpec(g_final.shape, lambda i: (0, 0))],
        out_specs=pl.BlockSpec((td, d), lambda i: (i, 0)),
        scratch_shapes=[pltpu.VMEM((2, TOP_K, td * SUBLANES, LANES), F32), pltpu.SemaphoreType.DMA((2,))],
    )
    return pl.pallas_call(
        functools.partial(_combine_kernel, td=td, final_norm=final_norm),
        grid_spec=grid_spec,
        out_shape=jax.ShapeDtypeStruct((n, d), F32),
        compiler_params=_cparams(1),
        name="combine",
    )(pos8, pos8, h2, meta, y, g_final)


def _rope_tables(seq_len):
    inv = 1.0 / (ROPE_BASE ** (jnp.arange(0, MLA_ROPE_DIM, 2, dtype=F32) / MLA_ROPE_DIM))
    ang = jnp.arange(seq_len, dtype=F32)[:, None] * inv[None, :]
    cos, sin = jnp.cos(ang), jnp.sin(ang)
    pad = jnp.zeros((seq_len, LANES - 2 * MLA_ROPE_DIM), F32)
    cos_t = jnp.concatenate([cos, cos, cos, cos, pad], axis=1)
    sin_t = jnp.concatenate([sin, sin, sin, sin, pad], axis=1)
    return cos_t, sin_t


def _rot_cols(w):
    half = MLA_ROPE_DIM // 2
    return jnp.concatenate([-w[:, half:], w[:, :half]], axis=1)


def _rope_pair_block(r0, r1):
    pad = jnp.zeros((r0.shape[0], LANES - 2 * MLA_ROPE_DIM), F32)
    return jnp.concatenate([r0, r1, pad], axis=1)


def _prep_inproj_weights(w_in, w_uq, w_ukv):
    d = w_in.shape[0]
    pts = np.cumsum((FOX_WIDTH, FOX_WIDTH, FOX_WIDTH, FOX_HEADS, MLA_Q_LORA, MLA_KV_LORA, MLA_ROPE_DIM))
    w_fl, w_ql = w_in[:, pts[2]:pts[3]], w_in[:, pts[3]:pts[4]]
    w_kvl, w_kr = w_in[:, pts[4]:pts[5]], w_in[:, pts[5]:pts[6]]
    misc = jnp.concatenate([w_fl, jnp.zeros((d, LANES - FOX_HEADS), F32)], axis=1)
    w_tail = jnp.concatenate([w_ql, w_kvl, misc, _rope_pair_block(w_kr, w_kr),
                              _rope_pair_block(_rot_cols(w_kr), _rot_cols(w_kr))], axis=1).astype(BF16)
    qd = MLA_NOPE_DIM + MLA_ROPE_DIM
    kvd = MLA_NOPE_DIM + MLA_V_DIM
    q_nope = [w_uq[:, h * qd:h * qd + MLA_NOPE_DIM] for h in range(MLA_HEADS)]
    q_rope = [w_uq[:, h * qd + MLA_NOPE_DIM:(h + 1) * qd] for h in range(MLA_HEADS)]
    wqa = [_rope_pair_block(q_rope[h], q_rope[h + 1]) for h in range(0, MLA_HEADS, 2)]
    wqb = [_rope_pair_block(_rot_cols(q_rope[h]), _rot_cols(q_rope[h + 1])) for h in range(0, MLA_HEADS, 2)]
    wk = [w_ukv[:, h * kvd:h * kvd + MLA_NOPE_DIM] for h in range(MLA_HEADS)]
    wv = [w_ukv[:, h * kvd + MLA_NOPE_DIM:(h + 1) * kvd] for h in range(MLA_HEADS)]
    cat = lambda xs: jnp.concatenate(xs, axis=1).astype(BF16)
    return w_in.astype(BF16), w_tail, cat(q_nope), cat(wqa), cat(wqb), cat(wk), cat(wv)


def _routing_tables(counts, n_tiles, tm):
    tiles_e = (counts + tm - 1) // tm
    tile_end = jnp.cumsum(tiles_e)
    off = ((tile_end - tiles_e) * tm).astype(I32)
    total = tile_end[-1]
    ti = jnp.arange(n_tiles, dtype=I32)
    te = jnp.minimum(jnp.sum(ti[:, None] >= tile_end[None, :], axis=1), N_EXPERTS - 1).astype(I32)
    valid = ti < total
    last_e = jnp.sum(jnp.where(ti == total - 1, te, 0))
    te = jnp.where(valid, te, last_e).astype(I32)
    last_tile_row = (off + (tiles_e - 1) * tm).astype(I32)
    used = tiles_e > 0
    ids = jnp.arange(N_EXPERTS, dtype=I32)
    slot_e = (jnp.cumsum(used.astype(I32)) - 1) % 2
    later = jnp.where(used[None, :] & (ids[None, :] > ids[:, None]), ids[None, :], N_EXPERTS)
    nxt_e = jnp.min(later, axis=1)
    nxt_e = jnp.where(nxt_e == N_EXPERTS, -1, nxt_e).astype(I32)
    pick = te[:, None] == ids[None, :]
    slot = jnp.sum(jnp.where(pick, slot_e[None, :], 0), axis=1).astype(I32)
    nxt = jnp.sum(jnp.where(pick, nxt_e[None, :], 0), axis=1).astype(I32)
    end_row = jnp.sum(jnp.where(pick, (off + counts)[None, :], 0), axis=1)
    rows = jnp.where(valid, jnp.clip(end_row - ti * tm, 0, tm), 0).astype(I32)
    return off, last_tile_row, te, total.astype(I32).reshape(1), slot, nxt, rows


def kernel(x, mem, g_mix, w_in, b_f, g_q_lat, w_uq, g_kv_lat, w_ukv, g_fox_out, g_mla_out, w_o, g_mem_q, w_mem_q,
           g_mem_kv, w_mem_kv, w_mem_o, g_ffn, w_router, b_router, w_gate_up, b_gate_up, w_down, b_down, g_final):
    b, s, d = x.shape
    n = b * s
    depth = g_mix.shape[0]
    cos_t, sin_t = _rope_tables(s)
    row = lambda v: v.reshape(1, -1)
    h = x
    for l in range(depth):
        w_in_bf, w_tail, wqn, wqa, wqb, wk, wv = _prep_inproj_weights(w_in[l], w_uq[l], w_ukv[l])
        fq, fk, fv, flog, qm, km, vm = _inproj(h, row(g_mix[l]), w_in_bf, w_tail, row(g_q_lat[l]), wqn, wqa, wqb,
                                                row(g_kv_lat[l]), wk, wv, cos_t, sin_t)
        cq, ck = _decay(flog, b_f[l])
        o_fox = _attention_pipelined(True, fq, fk, fv, cq, ck)
        o_mla = _attention_pipelined(False, qm, km, vm)
        kmem, vmem = _memkv(mem, row(g_mem_kv[l]), w_mem_kv[l].astype(BF16))
        wr = jnp.zeros((d, LANES), F32).at[:, :N_EXPERTS].set(w_router[l])
        wr_hi = wr.astype(BF16)
        wr_lo = (wr - wr_hi.astype(F32)).astype(BF16)
        br = jnp.zeros((1, LANES), F32).at[0, :N_EXPERTS].set(b_router[l])
        h2, at, meta, meta_t, cnt = _memrouter(
            o_fox, o_mla, h, row(g_fox_out[l]), row(g_mla_out[l]), w_o[l].astype(BF16), row(g_mem_q[l]),
            w_mem_q[l].astype(BF16), kmem, vmem, w_mem_o[l].astype(BF16), row(g_ffn[l]), wr_hi, wr_lo, br)
        ek = meta_t[0:TOP_K].astype(I32)
        rk = meta_t[TOP_K:2 * TOP_K].astype(I32)
        n_tiles = n * TOP_K // TM_MOE + N_EXPERTS
        counts = cnt[0, :N_EXPERTS].astype(I32)
        off, last_tile_row, te, nv, slot, nxt, rows = _routing_tables(counts, n_tiles, TM_MOE)
        ids = jnp.arange(N_EXPERTS, dtype=I32)[:, None, None]
        off_of = jnp.sum(jnp.where(ek[None] == ids, off[:, None, None], 0), axis=0)
        pos8 = ((off_of + rk) * SUBLANES).astype(I32)
        td = min(TD_ROWS, n)
        pos8 = pos8.reshape(TOP_K, n // td, td).transpose(1, 0, 2).reshape(-1)
        xs = _dispatch(last_tile_row, counts, nv, pos8, at, n_tiles, TM_MOE)
        y = _moe(te, rows, slot, nxt, xs, w_gate_up[l], b_gate_up[l].reshape(N_EXPERTS, 1, -1), w_down[l],
                 b_down[l].reshape(N_EXPERTS, 1, -1), TM_MOE)
        h = _combine(pos8, h2.reshape(n, d), meta.reshape(n, LANES), y, row(g_final),
                     l == depth - 1).reshape(b, s, d)
    return h
```

```python
import functools

import numpy as np
import jax
import jax.numpy as jnp
from jax import lax
from jax.experimental import pallas as pl
from jax.experimental.pallas import tpu as pltpu

F32 = jnp.float32
BF16 = jnp.bfloat16
I32 = jnp.int32

LANES = 128
SUBLANES = 8
RMS_EPS = 1e-6
CHUNK = 64
FOX_HEADS = 8
FOX_HEAD_DIM = 64
FOX_WIDTH = FOX_HEADS * FOX_HEAD_DIM
MLA_HEADS = 8
MLA_Q_LORA = 384
MLA_KV_LORA = 256
MLA_NOPE_DIM = 64
MLA_ROPE_DIM = 32
MLA_V_DIM = 64
MLA_WIDTH = MLA_HEADS * MLA_V_DIM
ROPE_BASE = 10000.0
MEM_HEADS = 4
MEM_HEAD_DIM = 128
MEM_WIDTH = MEM_HEADS * MEM_HEAD_DIM
N_EXPERTS = 32
TOP_K = 4
SWIGLU_LIMIT = 7.0
SWIGLU_ALPHA = 1.702

LOG2E = 1.4426950408889634
NEG = -1e30
FOX_QSCALE = FOX_HEAD_DIM ** -0.5 * LOG2E
MLA_QSCALE = (MLA_NOPE_DIM + MLA_ROPE_DIM) ** -0.5 * LOG2E
MEM_QSCALE = MEM_HEAD_DIM ** -0.5 * LOG2E
FOX_DECAY_LANES = 16
META_ROWS = 16

VMEM_LIMIT = 56 * 1024 * 1024

TS_PROJ = 512
TS_MEM = 512
T_ATT = 512
DECAY_BLK = 256
TM_MOE = 512
MOE_ROW_GROUPS = 4
TD_ROWS = 512
MOE_CH = 512


def _cparams(n_axes):
    return pltpu.CompilerParams(dimension_semantics=("arbitrary",) * n_axes, vmem_limit_bytes=VMEM_LIMIT)


def _dot(a, b):
    return jnp.dot(a, b, preferred_element_type=F32)


def _dot_nt(a, b):
    return lax.dot_general(a, b, (((1,), (1,)), ((), ())), preferred_element_type=F32)


def _rms(x, g):
    return x * lax.rsqrt(jnp.mean(x * x, axis=-1, keepdims=True) + RMS_EPS) * g


def _split3(x):
    hi = x.astype(BF16)
    r1 = x - hi.astype(F32)
    mid = r1.astype(BF16)
    lo = (r1 - mid.astype(F32)).astype(BF16)
    return hi, mid, lo


def _interleave_blocks(a, b):
    parts = []
    for p in range(a.shape[1] // LANES):
        parts += [a[:, p * LANES:(p + 1) * LANES], b[:, p * LANES:(p + 1) * LANES]]
    return jnp.concatenate(parts, axis=1)


def _inproj_kernel(x_ref, g_ref, win_ref, wtail_ref, gq_ref, wqn_ref, wqa_ref, wqb_ref, gkv_ref, wk_ref, wv_ref,
                   cos_ref, sin_ref, fq_ref, fk_ref, fv_ref, fl_ref, qm_ref, km_ref, vm_ref):
    a = _rms(x_ref[0], g_ref[...]).astype(BF16)
    c0 = 3 * FOX_WIDTH
    c1 = c0 + MLA_Q_LORA
    c2 = c1 + MLA_KV_LORA

    def proj(lo, hi):
        if hi <= c0:
            return _dot(a, win_ref[:, lo:hi])
        return _dot(a, wtail_ref[:, lo - c0:hi - c0])

    fq_ref[0] = (proj(0, FOX_WIDTH) * FOX_QSCALE).astype(BF16)
    fk_ref[0] = proj(FOX_WIDTH, 2 * FOX_WIDTH).astype(BF16)
    fv_ref[0] = proj(2 * FOX_WIDTH, c0).astype(BF16)
    qn = _rms(proj(c0, c1), gq_ref[...]).astype(BF16)
    kvn = _rms(proj(c1, c2), gkv_ref[...]).astype(BF16)
    fl_ref[0] = proj(c2, c2 + LANES)
    cos = cos_ref[...]
    sin = sin_ref[...]
    pairs = MLA_HEADS // 2
    kpe = (proj(c2 + LANES, c2 + 2 * LANES) * cos + proj(c2 + 2 * LANES, c2 + 3 * LANES) * sin).astype(BF16)
    cos4 = jnp.concatenate([cos] * pairs, axis=1)
    sin4 = jnp.concatenate([sin] * pairs, axis=1)
    q_nope = (_dot(qn, wqn_ref[...]) * MLA_QSCALE).astype(BF16)
    q_rope = ((_dot(qn, wqa_ref[...]) * cos4 + _dot(qn, wqb_ref[...]) * sin4) * MLA_QSCALE).astype(BF16)
    qm_ref[0] = _interleave_blocks(q_nope, q_rope)
    k_nope = _dot(kvn, wk_ref[...]).astype(BF16)
    km_ref[0] = _interleave_blocks(k_nope, jnp.concatenate([kpe] * pairs, axis=1))
    vm_ref[0] = _dot(kvn, wv_ref[...]).astype(BF16)


def _inproj(x, g_mix, w_in_bf, w_tail, g_q, wqn, wqa, wqb, g_kv, wk, wv, cos_t, sin_t):
    b, s, d = x.shape
    ts = min(TS_PROJ, s)

    def full(arr):
        return pl.BlockSpec(arr.shape, lambda bi, si: (0,) * arr.ndim)

    def tok(width):
        return pl.BlockSpec((1, ts, width), lambda bi, si: (bi, si, 0))

    tab = pl.BlockSpec((ts, LANES), lambda bi, si: (si, 0))
    pair_w = 2 * LANES * (MLA_HEADS // 2)
    out_shapes = (
        jax.ShapeDtypeStruct((b, s, FOX_WIDTH), BF16),
        jax.ShapeDtypeStruct((b, s, FOX_WIDTH), BF16),
        jax.ShapeDtypeStruct((b, s, FOX_WIDTH), BF16),
        jax.ShapeDtypeStruct((b, s, LANES), F32),
        jax.ShapeDtypeStruct((b, s, pair_w), BF16),
        jax.ShapeDtypeStruct((b, s, pair_w), BF16),
        jax.ShapeDtypeStruct((b, s, MLA_WIDTH), BF16),
    )
    return pl.pallas_call(
        _inproj_kernel,
        grid=(b, s // ts),
        in_specs=[tok(d), full(g_mix), full(w_in_bf), full(w_tail), full(g_q), full(wqn), full(wqa), full(wqb),
                  full(g_kv), full(wk), full(wv), tab, tab],
        out_specs=(tok(FOX_WIDTH), tok(FOX_WIDTH), tok(FOX_WIDTH), tok(LANES), tok(pair_w), tok(pair_w),
                   tok(MLA_WIDTH)),
        out_shape=out_shapes,
        compiler_params=_cparams(2),
        name="inproj",
    )(x, g_mix, w_in_bf, w_tail, g_q, wqn, wqa, wqb, g_kv, wk, wv, cos_t, sin_t)


def _decay_kernel(fl_ref, bf_ref, place_ref, ones_ref, cq_ref, ck_ref, *, blk):
    s = fl_ref.shape[1]
    row = lax.broadcasted_iota(I32, (blk, blk), 0)
    col = lax.broadcasted_iota(I32, (blk, blk), 1)
    tri = jnp.where(row >= col, 1.0, 0.0).astype(BF16)
    carry = jnp.zeros((1, LANES), F32)
    for i in range(s // blk):
        sl = slice(i * blk, (i + 1) * blk)
        z = fl_ref[0, sl, :] + bf_ref[...]
        lf = (jnp.minimum(z, 0.0) - jnp.log1p(jnp.exp(-jnp.abs(z)))) * LOG2E
        part = _dot(tri, jnp.concatenate(_split3(lf), axis=1))
        cs = part[:, :LANES] + part[:, LANES:2 * LANES] + part[:, 2 * LANES:] + carry
        carry = cs[blk - 1:blk, :]
        placed = _dot(jnp.concatenate(_split3(cs), axis=1), place_ref[...]) + ones_ref[...]
        cq_ref[0, sl, :] = placed[:, :LANES].astype(BF16)
        ck_ref[0, sl, :] = placed[:, LANES:].astype(BF16)


def _decay_tables():
    place = np.zeros((3 * LANES, 2 * LANES), np.float32)
    ones = np.zeros((1, 2 * LANES), np.float32)
    for h in range(FOX_HEADS):
        for part in range(3):
            place[part * LANES + h, FOX_DECAY_LANES * h + part] = 1.0
            place[part * LANES + h, LANES + FOX_DECAY_LANES * h + 3 + part] = -1.0
            ones[0, FOX_DECAY_LANES * h + 3 + part] = 1.0
            ones[0, LANES + FOX_DECAY_LANES * h + part] = 1.0
    return jnp.asarray(place, BF16), jnp.asarray(ones)


def _decay(flog, b_f):
    b, s, _ = flog.shape
    blk = min(DECAY_BLK, s)
    place, ones = _decay_tables()
    bf = jnp.zeros((1, LANES), F32).at[0, :FOX_HEADS].set(b_f)

    def full(arr):
        return pl.BlockSpec(arr.shape, lambda bi: (0,) * arr.ndim)

    seq = pl.BlockSpec((1, s, LANES), lambda bi: (bi, 0, 0))
    return pl.pallas_call(
        functools.partial(_decay_kernel, blk=blk),
        grid=(b,),
        in_specs=[seq, full(bf), full(place), full(ones)],
        out_specs=(seq, seq),
        out_shape=(jax.ShapeDtypeStruct((b, s, LANES), BF16), jax.ShapeDtypeStruct((b, s, LANES), BF16)),
        compiler_params=_cparams(1),
        name="decay",
    )(flog, bf, place, ones)


def _attn_pipe_kernel(*refs, fox, t, nq, pairs):
    if fox:
        q_ref, cq_ref, k_ref, ck_ref, v_ref, o_ref, s_even, s_odd = refs
        group = FOX_DECAY_LANES
    else:
        q_ref, k_ref, v_ref, o_ref, s_even, s_odd = refs
        group = MLA_ROPE_DIM
    g = pl.program_id(0)
    i = g % nq
    pair = (jnp.minimum(g, pl.num_programs(0) - 2) // nq) % pairs
    base = 2 * pair if fox else 0
    lane = lax.broadcasted_iota(I32, (1, LANES), 1)

    @pl.when(g == 0)
    def _():
        s_odd[...] = jnp.zeros_like(s_odd)

    def body(iv):
        s_new, s_old = (s_even, s_odd) if iv % 2 == 0 else (s_odd, s_even)
        half = t // 2
        row = lax.broadcasted_iota(I32, (half, half), 0)
        col = lax.broadcasted_iota(I32, (half, half), 1)
        allowed = (col <= row) if fox else ((col // CHUNK) <= (row // CHUNK))
        past = iv * t
        if fox:
            q_main, q_extra = q_ref[0], cq_ref[0]
        else:
            q_main, q_extra = q_ref[0, :, :LANES], q_ref[0, :, LANES:]
        zero = jnp.zeros_like(q_main)

        def keys(lo, hi):
            if fox:
                return jnp.concatenate([k_ref[0, lo:hi, :], ck_ref[0, lo:hi, :]], axis=1)
            return k_ref[0, lo:hi, :]

        qa = [jnp.concatenate([jnp.where((lane // (LANES // 2)) == hh, q_main, zero),
                               jnp.where((lane // group) == base + hh, q_extra, zero)], axis=1) for hh in range(2)]
        for hh in range(2):
            if iv > 0:
                s_new[hh, :, 0:past] = _dot_nt(qa[hh], keys(0, past))
            upper = _dot_nt(qa[hh][:half], keys(past, past + half))
            s_new[hh, :half, past:past + half] = jnp.where(allowed, upper, NEG)
            lower = _dot_nt(qa[hh][half:], keys(past, past + t))
            s_new[hh, half:, past:past + half] = lower[:, :half]
            s_new[hh, half:, past + half:past + t] = jnp.where(allowed, lower[:, half:], NEG)

        seen = ((iv - 1) % nq + 1) * t
        outs = []
        for hh in range(2):
            sc = s_old[hh, :, 0:seen - half]
            sc_tail = s_old[hh, half:, seen - half:seen]
            m_main = jnp.max(sc, axis=-1, keepdims=True)
            m_low = jnp.maximum(m_main[half:], jnp.max(sc_tail, axis=-1, keepdims=True))
            p = jnp.exp2(sc - jnp.concatenate([m_main[:half], m_low], axis=0))
            p_tail = jnp.exp2(sc_tail - m_low)
            l_main = jnp.sum(p, axis=-1, keepdims=True)
            l = jnp.concatenate([l_main[:half], l_main[half:] + jnp.sum(p_tail, axis=-1, keepdims=True)], axis=0)
            acc = _dot(p.astype(BF16), v_ref[0, 0:seen - half, :])
            acc_tail = _dot(p_tail.astype(BF16), v_ref[0, seen - half:seen, :])
            acc = jnp.concatenate([acc[:half], acc[half:] + acc_tail], axis=0)
            outs.append(acc * (1.0 / l))
        o_ref[0] = jnp.where(lane < LANES // 2, outs[0], outs[1]).astype(o_ref.dtype)

    for iv in range(nq):
        pl.when(i == iv)(functools.partial(body, iv))


def _attention_pipelined(fox, q, k, v, cq=None, ck=None):
    b, s, _ = v.shape
    t = min(T_ATT, s)
    nq = s // t
    assert nq % 2 == 0
    pairs = v.shape[2] // LANES
    qw = q.shape[2] // pairs
    kw = k.shape[2] // pairs
    items = b * pairs * nq

    def item(g):
        return g // (pairs * nq), (g // nq) % pairs, g % nq

    def cur(g):
        return item(jnp.minimum(g, items - 1))

    def prev(g):
        return item(jnp.maximum(g - 1, 0))

    qspec = pl.BlockSpec((1, t, qw), lambda g: (cur(g)[0], cur(g)[2], cur(g)[1]))
    kspec = pl.BlockSpec((1, s, kw), lambda g: (cur(g)[0], 0, cur(g)[1]))
    vspec = pl.BlockSpec((1, s, LANES), lambda g: (prev(g)[0], 0, prev(g)[1]))
    ospec = pl.BlockSpec((1, t, LANES), lambda g: (prev(g)[0], prev(g)[2], prev(g)[1]))
    if fox:
        cqspec = pl.BlockSpec((1, t, LANES), lambda g: (cur(g)[0], cur(g)[2], 0))
        ckspec = pl.BlockSpec((1, s, LANES), lambda g: (cur(g)[0], 0, 0))
        in_specs = [qspec, cqspec, kspec, ckspec, vspec]
        args = (q, cq, k, ck, v)
    else:
        in_specs = [qspec, kspec, vspec]
        args = (q, k, v)
    return pl.pallas_call(
        functools.partial(_attn_pipe_kernel, fox=fox, t=t, nq=nq, pairs=pairs),
        grid=(items + 1,),
        in_specs=in_specs,
        out_specs=ospec,
        out_shape=jax.ShapeDtypeStruct((b, s, pairs * LANES), BF16),
        scratch_shapes=[pltpu.VMEM((2, t, s), F32), pltpu.VMEM((2, t, s), F32)],
        compiler_params=_cparams(1),
        name="fox_attn" if fox else "mla_attn",
    )(*args)


def _memkv_kernel(mem_ref, g_ref, w_ref, k_ref, v_ref):
    a = _rms(mem_ref[0], g_ref[...]).astype(BF16)
    k_ref[0] = _dot(a, w_ref[:, 0:MEM_WIDTH]).astype(BF16)
    v_ref[0] = _dot(a, w_ref[:, MEM_WIDTH:2 * MEM_WIDTH]).astype(BF16)


def _memkv(mem, g, w):
    b, m, d = mem.shape
    kv = pl.BlockSpec((1, m, MEM_WIDTH), lambda bi: (bi, 0, 0))
    return pl.pallas_call(
        _memkv_kernel,
        grid=(b,),
        in_specs=[pl.BlockSpec((1, m, d), lambda bi: (bi, 0, 0)), pl.BlockSpec(g.shape, lambda bi: (0, 0)),
                  pl.BlockSpec(w.shape, lambda bi: (0, 0))],
        out_specs=(kv, kv),
        out_shape=(jax.ShapeDtypeStruct((b, m, MEM_WIDTH), BF16), jax.ShapeDtypeStruct((b, m, MEM_WIDTH), BF16)),
        compiler_params=_cparams(1),
        name="memkv",
    )(mem, g, w)


def _store_token_tiles(ref, val):
    rows = val.shape[0]
    for j in range(SUBLANES):
        ref[pl.ds(j, rows, stride=SUBLANES), :] = val[:, j * LANES:(j + 1) * LANES]


def _load_token_tiles(ref, rows):
    return jnp.concatenate([ref[pl.ds(j, rows, stride=SUBLANES), :] for j in range(SUBLANES)], axis=1)


def _memrouter_kernel(of_ref, om_ref, x_ref, gf_ref, gm_ref, wmix_ref, gq_ref, wq_ref, k_ref, v_ref, wo_ref,
                      gffn_ref, wrh_ref, wrl_ref, br_ref, tri_ref, h2_ref, at_ref, meta_ref, metat_ref, cnt_ref, carry_ref,
                      *, ts):
    first = jnp.logical_and(pl.program_id(0) == 0, pl.program_id(1) == 0)

    @pl.when(first)
    def _():
        carry_ref[...] = jnp.zeros_like(carry_ref)

    nf = _rms(of_ref[0].astype(F32), gf_ref[...]).astype(BF16)
    nm = _rms(om_ref[0].astype(F32), gm_ref[...]).astype(BF16)
    h1 = x_ref[0] + _dot(nf, wmix_ref[0:FOX_WIDTH, :]) + _dot(nm, wmix_ref[FOX_WIDTH:FOX_WIDTH + MLA_WIDTH, :])

    q = (_dot(_rms(h1, gq_ref[...]).astype(BF16), wq_ref[...]) * MEM_QSCALE).astype(BF16)
    heads = []
    for h in range(MEM_HEADS):
        sl = slice(h * MEM_HEAD_DIM, (h + 1) * MEM_HEAD_DIM)
        sc = _dot_nt(q[:, sl], k_ref[0, :, sl])
        m = jnp.max(sc, axis=-1, keepdims=True)
        pm = jnp.exp2(sc - m)
        l = jnp.sum(pm, axis=-1, keepdims=True)
        heads.append((_dot(pm.astype(BF16), v_ref[0, :, sl]) / l).astype(BF16))
    h2 = h1 + _dot(jnp.concatenate(heads, axis=1), wo_ref[...])
    h2_ref[0] = h2

    a = _rms(h2, gffn_ref[...])
    _store_token_tiles(at_ref, a)
    a_hi = a.astype(BF16)
    a_lo = (a - a_hi.astype(F32)).astype(BF16)
    logits = _dot(a_hi, wrh_ref[...]) + _dot(a_lo, wrh_ref[...]) + _dot(a_hi, wrl_ref[...]) + br_ref[...]
    lane = lax.broadcasted_iota(I32, (ts, LANES), 1)
    lane_f = lane.astype(F32)
    work = jnp.where(lane < N_EXPERTS, logits, NEG)
    vals, idxs, sels = [], [], []
    for _ in range(TOP_K):
        mk = jnp.max(work, axis=-1, keepdims=True)
        ik = jnp.min(jnp.where(work == mk, lane_f, float(LANES)), axis=-1, keepdims=True)
        sel = lane_f == ik
        work = jnp.where(sel, NEG, work)
        vals.append(mk)
        idxs.append(ik)
        sels.append(sel)
    exps = [jnp.exp(vk - vals[0]) for vk in vals]
    denom = exps[0] + exps[1] + exps[2] + exps[3]
    chosen = jnp.logical_or(jnp.logical_or(sels[0], sels[1]), jnp.logical_or(sels[2], sels[3]))
    onehot = jnp.where(chosen, 1.0, 0.0)
    carry = carry_ref[...]
    rank = _dot(tri_ref[...], onehot.astype(BF16)) + carry
    carry_new = carry + jnp.sum(onehot, axis=0, keepdims=True)
    carry_ref[...] = carry_new
    cnt_ref[...] = carry_new
    meta = jnp.zeros((ts, LANES), F32)
    for kk in range(TOP_K):
        rk = jnp.sum(jnp.where(sels[kk], rank, 0.0), axis=-1, keepdims=True)
        meta = jnp.where(lane == kk, idxs[kk], meta)
        meta = jnp.where(lane == TOP_K + kk, rk, meta)
        meta = jnp.where(lane == 2 * TOP_K + kk, exps[kk] / denom, meta)
    meta_ref[0] = meta
    metat_ref[...] = jnp.transpose(meta)[0:META_ROWS, :]


def _memrouter(o_fox, o_mla, x, g_fox, g_mla, w_mix, g_mq, w_mq, kmem, vmem, w_mo, g_ffn, wr_hi, wr_lo, b_r):
    b, s, d = x.shape
    m = kmem.shape[1]
    ts = min(TS_MEM, s)

    def full(arr):
        return pl.BlockSpec(arr.shape, lambda bi, si: (0,) * arr.ndim)

    def tok(width):
        return pl.BlockSpec((1, ts, width), lambda bi, si: (bi, si, 0))

    kv = pl.BlockSpec((1, m, MEM_WIDTH), lambda bi, si: (bi, 0, 0))
    cnt = pl.BlockSpec((1, LANES), lambda bi, si: (0, 0))
    tri = jnp.asarray(np.tril(np.ones((ts, ts), np.float32), k=-1), BF16)
    return pl.pallas_call(
        functools.partial(_memrouter_kernel, ts=ts),
        grid=(b, s // ts),
        in_specs=[tok(FOX_WIDTH), tok(MLA_WIDTH), tok(d), full(g_fox), full(g_mla), full(w_mix), full(g_mq),
                  full(w_mq), kv, kv, full(w_mo), full(g_ffn), full(wr_hi), full(wr_lo), full(b_r), full(tri)],
        out_specs=(tok(d), pl.BlockSpec((ts * SUBLANES, LANES), lambda bi, si: (bi * (s // ts) + si, 0)),
                   tok(LANES), pl.BlockSpec((META_ROWS, ts), lambda bi, si: (0, bi * (s // ts) + si)), cnt),
        out_shape=(jax.ShapeDtypeStruct((b, s, d), F32), jax.ShapeDtypeStruct((b * s * SUBLANES, LANES), F32),
                   jax.ShapeDtypeStruct((b, s, LANES), F32), jax.ShapeDtypeStruct((META_ROWS, b * s), F32),
                   jax.ShapeDtypeStruct((1, LANES), F32)),
        scratch_shapes=[pltpu.VMEM((1, LANES), F32)],
        compiler_params=_cparams(2),
        name="memrouter",
    )(o_fox, o_mla, x, g_fox, g_mla, w_mix, g_mq, w_mq, kmem, vmem, w_mo, g_ffn, wr_hi, wr_lo, b_r, tri)


def _tile_copy(src_ref, src_row8, dst_ref, dst_row8, sem):
    return pltpu.make_async_copy(src_ref.at[pl.ds(pl.multiple_of(src_row8, SUBLANES), SUBLANES), :],
                                 dst_ref.at[pl.ds(pl.multiple_of(dst_row8, SUBLANES), SUBLANES), :], sem)


def _dispatch_kernel(last_ref, cnt_ref, nv_ref, pos_ref, at_ref, xs_ref, zbuf, zsem, sem, *, td, tm, min_used,
                     n_tiles):
    @pl.when(pl.program_id(0) == 0)
    def _():
        zbuf[...] = jnp.zeros_like(zbuf)

        def fill(row):
            return pltpu.make_async_copy(
                zbuf, xs_ref.at[pl.ds(pl.multiple_of(row * SUBLANES, SUBLANES), tm * SUBLANES), :], zsem)

        fills = [(cnt_ref[e] > 0, last_ref[e]) for e in range(N_EXPERTS)]
        fills += [(t >= nv_ref[0], t * tm) for t in range(min_used, n_tiles)]
        for pred, row in fills:
            pl.when(pred)(lambda row=row: fill(row).start())
        for pred, row in fills:
            pl.when(pred)(lambda row=row: fill(row).wait())

    def issue(r, c):
        for kk in range(TOP_K):
            _tile_copy(at_ref, r * SUBLANES, xs_ref, pos_ref[kk * td + r], sem).start(priority=kk % 2)
        return c

    lax.fori_loop(0, td, issue, 0, unroll=8)
    for _ in range(TOP_K):
        pltpu.make_async_copy(at_ref, xs_ref.at[pl.ds(0, td * SUBLANES), :], sem).wait()


def _dispatch(last_tile_row, cnt, nv, pos8, at, n_tiles, tm):
    n = at.shape[0] // SUBLANES
    td = min(TD_ROWS, n)
    grid_spec = pltpu.PrefetchScalarGridSpec(
        num_scalar_prefetch=3,
        grid=(n // td,),
        in_specs=[pl.BlockSpec((TOP_K * td,), lambda i, *_: (i,), memory_space=pltpu.SMEM),
                  pl.BlockSpec((td * SUBLANES, LANES), lambda i, *_: (i, 0))],
        out_specs=pl.BlockSpec(memory_space=pl.ANY),
        scratch_shapes=[pltpu.VMEM((tm * SUBLANES, LANES), F32), pltpu.SemaphoreType.DMA, pltpu.SemaphoreType.DMA],
    )
    min_used = n * TOP_K // tm
    return pl.pallas_call(
        functools.partial(_dispatch_kernel, td=td, tm=tm, min_used=min_used, n_tiles=n_tiles),
        grid_spec=grid_spec,
        out_shape=jax.ShapeDtypeStruct((n_tiles * tm * SUBLANES, LANES), F32),
        compiler_params=_cparams(1),
        name="dispatch",
    )(last_tile_row, cnt, nv, pos8, at)


def _moe_kernel(te_ref, rows_ref, slot_ref, nxt_ref, xs_ref, wgu_hbm, bgu_ref, wd_hbm, bd_ref, y_ref,
                wgu_f32, wd_f32, wgu_bf, wd_bf, sem, *, ch, tm):
    i = pl.program_id(0)
    d_exp = wd_hbm.shape[1]
    rows_used = rows_ref[i]
    valid = rows_used > 0
    new_expert = jnp.logical_or(i == 0, te_ref[i] != te_ref[jnp.maximum(i - 1, 0)])

    def weight_copies(e, sl):
        return (pltpu.make_async_copy(wgu_hbm.at[e], wgu_f32.at[sl], sem.at[sl, 0]),
                pltpu.make_async_copy(wd_hbm.at[e], wd_f32.at[sl], sem.at[sl, 1]))

    @pl.when(i == 0)
    def _():
        for c in weight_copies(te_ref[0], slot_ref[0]):
            c.start()

    @pl.when(jnp.logical_and(new_expert, valid))
    def _():
        sl = slot_ref[i]
        for c in weight_copies(te_ref[i], sl):
            c.wait()
        wgu_bf[...] = wgu_f32[sl].astype(BF16)
        wd_bf[...] = wd_f32[sl].astype(BF16)

        @pl.when(nxt_ref[i] >= 0)
        def _():
            for c in weight_copies(nxt_ref[i], 1 - sl):
                c.start()

    def ffn(rows):
        x = _load_token_tiles(xs_ref, rows).astype(BF16)
        acc = jnp.zeros((rows, wd_hbm.shape[2]), F32)
        for c in range(d_exp // ch):
            def gu(lo):
                return _dot(x, wgu_bf[:, lo:lo + ch]) + bgu_ref[0, :, lo:lo + ch]

            gate = jnp.minimum(gu(c * ch), SWIGLU_LIMIT)
            up = jnp.clip(gu(d_exp + c * ch), -SWIGLU_LIMIT, SWIGLU_LIMIT)
            act = gate * (1.0 / (1.0 + jnp.exp(-SWIGLU_ALPHA * gate))) * (up + 1.0)
            acc = acc + _dot(act.astype(BF16), wd_bf[c * ch:(c + 1) * ch, :])
        return acc + bd_ref[0]

    grp = tm // MOE_ROW_GROUPS
    for used in range(1, MOE_ROW_GROUPS + 1):
        rows = used * grp

        def partial_tile(rows=rows):
            _store_token_tiles(y_ref, ffn(rows))
            if rows < tm:
                y_ref[rows * SUBLANES:, :] = jnp.zeros(((tm - rows) * SUBLANES, LANES), F32)

        pl.when(jnp.logical_and(rows_used > rows - grp, rows_used <= rows))(partial_tile)

    @pl.when(jnp.logical_not(valid))
    def _():
        y_ref[...] = jnp.zeros_like(y_ref)


def _moe(te, rows, slot, nxt, xs, wgu, bgu, wd, bd, tm):
    n_tiles = xs.shape[0] // (tm * SUBLANES)
    _, d, d_exp2 = wgu.shape
    d_exp = wd.shape[1]
    assert d == SUBLANES * LANES and wd.shape[2] == d
    tile = pl.BlockSpec((tm * SUBLANES, LANES), lambda i, te, *_: (i, 0))
    grid_spec = pltpu.PrefetchScalarGridSpec(
        num_scalar_prefetch=4,
        grid=(n_tiles,),
        in_specs=[
            tile,
            pl.BlockSpec(memory_space=pl.ANY),
            pl.BlockSpec((1, 1, d_exp2), lambda i, te, *_: (te[i], 0, 0)),
            pl.BlockSpec(memory_space=pl.ANY),
            pl.BlockSpec((1, 1, d), lambda i, te, *_: (te[i], 0, 0)),
        ],
        out_specs=tile,
        scratch_shapes=[pltpu.VMEM((2, d, d_exp2), F32), pltpu.VMEM((2, d_exp, d), F32),
                        pltpu.VMEM((d, d_exp2), BF16), pltpu.VMEM((d_exp, d), BF16),
                        pltpu.SemaphoreType.DMA((2, 2))],
    )
    return pl.pallas_call(
        functools.partial(_moe_kernel, ch=min(MOE_CH, d_exp), tm=tm),
        grid_spec=grid_spec,
        out_shape=jax.ShapeDtypeStruct(xs.shape, F32),
        compiler_params=_cparams(1),
        name="moe",
    )(te, rows, slot, nxt, xs, wgu, bgu, wd, bd)


def _combine_kernel(pos_ref, posn_ref, h2_ref, meta_ref, y_ref, gf_ref, o_ref, ybuf, sem, *, td, final_norm):
    i = pl.program_id(0)
    slot = i % 2

    def issue_block(p_ref, sl):
        def issue(r, c):
            for kk in range(TOP_K):
                _tile_copy(y_ref, p_ref[kk * td + r], ybuf.at[sl, kk], r * SUBLANES,
                           sem.at[sl]).start(priority=kk % 2)
            return c

        lax.fori_loop(0, td, issue, 0, unroll=8)

    @pl.when(i == 0)
    def _():
        issue_block(pos_ref, 0)

    @pl.when(i + 1 < pl.num_programs(0))
    def _():
        issue_block(posn_ref, 1 - slot)

    for kk in range(TOP_K):
        pltpu.make_async_copy(y_ref.at[pl.ds(0, td * SUBLANES), :], ybuf.at[slot, kk], sem.at[slot]).wait()

    meta = meta_ref[...]
    acc = h2_ref[...]
    for kk in range(TOP_K):
        gate = meta[:, 2 * TOP_K + kk:2 * TOP_K + kk + 1]
        acc = acc + gate * _load_token_tiles(ybuf.at[slot, kk], td)
    o_ref[...] = _rms(acc, gf_ref[...]) if final_norm else acc


def _combine(pos8, h2, meta, y, g_final, final_norm):
    n, d = h2.shape
    td = min(TD_ROWS, n)
    n_blocks = n // td
    grid_spec = pltpu.PrefetchScalarGridSpec(
        num_scalar_prefetch=0,
        grid=(n_blocks,),
        in_specs=[pl.BlockSpec((TOP_K * td,), lambda i: (i,), memory_space=pltpu.SMEM),
                  pl.BlockSpec((TOP_K * td,), lambda i: (jnp.minimum(i + 1, n_blocks - 1),), memory_space=pltpu.SMEM),
                  pl.BlockSpec((td, d), lambda i: (i, 0)),
                  pl.BlockSpec((td, LANES), lambda i: (i, 0)),
                  pl.BlockSpec(memory_space=pl.ANY),
                  pl.BlockSpec(g_final.shape, lambda i: (0, 0))],
        out_specs=pl.BlockSpec((td, d), lambda i: (i, 0)),
        scratch_shapes=[pltpu.VMEM((2, TOP_K, td * SUBLANES, LANES), F32), pltpu.SemaphoreType.DMA((2,))],
    )
    return pl.pallas_call(
        functools.partial(_combine_kernel, td=td, final_norm=final_norm),
        grid_spec=grid_spec,
        out_shape=jax.ShapeDtypeStruct((n, d), F32),
        compiler_params=_cparams(1),
        name="combine",
    )(pos8, pos8, h2, meta, y, g_final)


def _rope_tables(seq_len):
    inv = 1.0 / (ROPE_BASE ** (jnp.arange(0, MLA_ROPE_DIM, 2, dtype=F32) / MLA_ROPE_DIM))
    ang = jnp.arange(seq_len, dtype=F32)[:, None] * inv[None, :]
    cos, sin = jnp.cos(ang), jnp.sin(ang)
    pad = jnp.zeros((seq_len, LANES - 2 * MLA_ROPE_DIM), F32)
    cos_t = jnp.concatenate([cos, cos, cos, cos, pad], axis=1)
    sin_t = jnp.concatenate([sin, sin, sin, sin, pad], axis=1)
    return cos_t, sin_t


def _rot_cols(w):
    half = MLA_ROPE_DIM // 2
    return jnp.concatenate([-w[:, half:], w[:, :half]], axis=1)


def _rope_pair_block(r0, r1):
    pad = jnp.zeros((r0.shape[0], LANES - 2 * MLA_ROPE_DIM), F32)
    return jnp.concatenate([r0, r1, pad], axis=1)


def _prep_inproj_weights(w_in, w_uq, w_ukv):
    d = w_in.shape[0]
    pts = np.cumsum((FOX_WIDTH, FOX_WIDTH, FOX_WIDTH, FOX_HEADS, MLA_Q_LORA, MLA_KV_LORA, MLA_ROPE_DIM))
    w_fl, w_ql = w_in[:, pts[2]:pts[3]], w_in[:, pts[3]:pts[4]]
    w_kvl, w_kr = w_in[:, pts[4]:pts[5]], w_in[:, pts[5]:pts[6]]
    misc = jnp.concatenate([w_fl, jnp.zeros((d, LANES - FOX_HEADS), F32)], axis=1)
    w_tail = jnp.concatenate([w_ql, w_kvl, misc, _rope_pair_block(w_kr, w_kr),
                              _rope_pair_block(_rot_cols(w_kr), _rot_cols(w_kr))], axis=1).astype(BF16)
    qd = MLA_NOPE_DIM + MLA_ROPE_DIM
    kvd = MLA_NOPE_DIM + MLA_V_DIM
    q_nope = [w_uq[:, h * qd:h * qd + MLA_NOPE_DIM] for h in range(MLA_HEADS)]
    q_rope = [w_uq[:, h * qd + MLA_NOPE_DIM:(h + 1) * qd] for h in range(MLA_HEADS)]
    wqa = [_rope_pair_block(q_rope[h], q_rope[h + 1]) for h in range(0, MLA_HEADS, 2)]
    wqb = [_rope_pair_block(_rot_cols(q_rope[h]), _rot_cols(q_rope[h + 1])) for h in range(0, MLA_HEADS, 2)]
    wk = [w_ukv[:, h * kvd:h * kvd + MLA_NOPE_DIM] for h in range(MLA_HEADS)]
    wv = [w_ukv[:, h * kvd + MLA_NOPE_DIM:(h + 1) * kvd] for h in range(MLA_HEADS)]
    cat = lambda xs: jnp.concatenate(xs, axis=1).astype(BF16)
    return w_in[:, :pts[2]].astype(BF16), w_tail, cat(q_nope), cat(wqa), cat(wqb), cat(wk), cat(wv)


def _routing_tables(counts, n_tiles, tm):
    tiles_e = (counts + tm - 1) // tm
    tile_end = jnp.cumsum(tiles_e)
    off = ((tile_end - tiles_e) * tm).astype(I32)
    total = tile_end[-1]
    ti = jnp.arange(n_tiles, dtype=I32)
    te = jnp.minimum(jnp.sum(ti[:, None] >= tile_end[None, :], axis=1), N_EXPERTS - 1).astype(I32)
    valid = ti < total
    last_e = jnp.sum(jnp.where(ti == total - 1, te, 0))
    te = jnp.where(valid, te, last_e).astype(I32)
    last_tile_row = (off + (tiles_e - 1) * tm).astype(I32)
    used = tiles_e > 0
    ids = jnp.arange(N_EXPERTS, dtype=I32)
    slot_e = (jnp.cumsum(used.astype(I32)) - 1) % 2
    later = jnp.where(used[None, :] & (ids[None, :] > ids[:, None]), ids[None, :], N_EXPERTS)
    nxt_e = jnp.min(later, axis=1)
    nxt_e = jnp.where(nxt_e == N_EXPERTS, -1, nxt_e).astype(I32)
    pick = te[:, None] == ids[None, :]
    slot = jnp.sum(jnp.where(pick, slot_e[None, :], 0), axis=1).astype(I32)
    nxt = jnp.sum(jnp.where(pick, nxt_e[None, :], 0), axis=1).astype(I32)
    end_row = jnp.sum(jnp.where(pick, (off + counts)[None, :], 0), axis=1)
    rows = jnp.where(valid, jnp.clip(end_row - ti * tm, 0, tm), 0).astype(I32)
    return off, last_tile_row, te, total.astype(I32).reshape(1), slot, nxt, rows


def kernel(x, mem, g_mix, w_in, b_f, g_q_lat, w_uq, g_kv_lat, w_ukv, g_fox_out, g_mla_out, w_o, g_mem_q, w_mem_q,
           g_mem_kv, w_mem_kv, w_mem_o, g_ffn, w_router, b_router, w_gate_up, b_gate_up, w_down, b_down, g_final):
    b, s, d = x.shape
    n = b * s
    depth = g_mix.shape[0]
    cos_t, sin_t = _rope_tables(s)
    row = lambda v: v.reshape(1, -1)
    h = x
    for l in range(depth):
        w_in_bf, w_tail, wqn, wqa, wqb, wk, wv = _prep_inproj_weights(w_in[l], w_uq[l], w_ukv[l])
        fq, fk, fv, flog, qm, km, vm = _inproj(h, row(g_mix[l]), w_in_bf, w_tail, row(g_q_lat[l]), wqn, wqa, wqb,
                                                row(g_kv_lat[l]), wk, wv, cos_t, sin_t)
        cq, ck = _decay(flog, b_f[l])
        o_fox = _attention_pipelined(True, fq, fk, fv, cq, ck)
        o_mla = _attention_pipelined(False, qm, km, vm)
        kmem, vmem = _memkv(mem, row(g_mem_kv[l]), w_mem_kv[l].astype(BF16))
        wr = jnp.zeros((d, LANES), F32).at[:, :N_EXPERTS].set(w_router[l])
        wr_hi = wr.astype(BF16)
        wr_lo = (wr - wr_hi.astype(F32)).astype(BF16)
        br = jnp.zeros((1, LANES), F32).at[0, :N_EXPERTS].set(b_router[l])
        h2, at, meta, meta_t, cnt = _memrouter(
            o_fox, o_mla, h, row(g_fox_out[l]), row(g_mla_out[l]), w_o[l].astype(BF16), row(g_mem_q[l]),
            w_mem_q[l].astype(BF16), kmem, vmem, w_mem_o[l].astype(BF16), row(g_ffn[l]), wr_hi, wr_lo, br)
        ek = meta_t[0:TOP_K].astype(I32)
        rk = meta_t[TOP_K:2 * TOP_K].astype(I32)
        n_tiles = n * TOP_K // TM_MOE + N_EXPERTS
        counts = cnt[0, :N_EXPERTS].astype(I32)
        off, last_tile_row, te, nv, slot, nxt, rows = _routing_tables(counts, n_tiles, TM_MOE)
        ids = jnp.arange(N_EXPERTS, dtype=I32)[:, None, None]
        off_of = jnp.sum(jnp.where(ek[None] == ids, off[:, None, None], 0), axis=0)
        pos8 = ((off_of + rk) * SUBLANES).astype(I32)
        td = min(TD_ROWS, n)
        pos8 = pos8.reshape(TOP_K, n // td, td).transpose(1, 0, 2).reshape(-1)
        xs = _dispatch(last_tile_row, counts, nv, pos8, at, n_tiles, TM_MOE)
        y = _moe(te, rows, slot, nxt, xs, w_gate_up[l], b_gate_up[l].reshape(N_EXPERTS, 1, -1), w_down[l],
                 b_down[l].reshape(N_EXPERTS, 1, -1), TM_MOE)
        h = _combine(pos8, h2.reshape(n, d), meta.reshape(n, LANES), y, row(g_final),
                     l == depth - 1).reshape(b, s, d)
    return h
```

```python
import functools

import numpy as np
import jax
import jax.numpy as jnp
from jax import lax
from jax.experimental import pallas as pl
from jax.experimental.pallas import tpu as pltpu

F32 = jnp.float32
BF16 = jnp.bfloat16
I32 = jnp.int32

LANES = 128
SUBLANES = 8
RMS_EPS = 1e-6
CHUNK = 64
FOX_HEADS = 8
FOX_HEAD_DIM = 64
FOX_WIDTH = FOX_HEADS * FOX_HEAD_DIM
MLA_HEADS = 8
MLA_Q_LORA = 384
MLA_KV_LORA = 256
MLA_NOPE_DIM = 64
MLA_ROPE_DIM = 32
MLA_V_DIM = 64
MLA_WIDTH = MLA_HEADS * MLA_V_DIM
ROPE_BASE = 10000.0
MEM_HEADS = 4
MEM_HEAD_DIM = 128
MEM_WIDTH = MEM_HEADS * MEM_HEAD_DIM
N_EXPERTS = 32
TOP_K = 4
SWIGLU_LIMIT = 7.0
SWIGLU_ALPHA = 1.702

LOG2E = 1.4426950408889634
NEG = -1e30
FOX_QSCALE = FOX_HEAD_DIM ** -0.5 * LOG2E
MLA_QSCALE = (MLA_NOPE_DIM + MLA_ROPE_DIM) ** -0.5 * LOG2E
MEM_QSCALE = MEM_HEAD_DIM ** -0.5 * LOG2E
FOX_DECAY_LANES = 16
META_ROWS = 16

VMEM_LIMIT = 56 * 1024 * 1024

TS_PROJ = 512
TS_MEM = 512
T_ATT = 512
DECAY_BLK = 256
TM_MOE = 512
MOE_ROW_GROUPS = 4
TD_ROWS = 512
MOE_CH = 512


def _cparams(n_axes):
    return pltpu.CompilerParams(dimension_semantics=("arbitrary",) * n_axes, vmem_limit_bytes=VMEM_LIMIT)


def _dot(a, b):
    return jnp.dot(a, b, preferred_element_type=F32)


def _dot_nt(a, b):
    return lax.dot_general(a, b, (((1,), (1,)), ((), ())), preferred_element_type=F32)


def _rms(x, g):
    return x * lax.rsqrt(jnp.mean(x * x, axis=-1, keepdims=True) + RMS_EPS) * g


def _split3(x):
    hi = x.astype(BF16)
    r1 = x - hi.astype(F32)
    mid = r1.astype(BF16)
    lo = (r1 - mid.astype(F32)).astype(BF16)
    return hi, mid, lo


def _interleave_blocks(a, b):
    parts = []
    for p in range(a.shape[1] // LANES):
        parts += [a[:, p * LANES:(p + 1) * LANES], b[:, p * LANES:(p + 1) * LANES]]
    return jnp.concatenate(parts, axis=1)


def _inproj_kernel(x_ref, g_ref, win_ref, wtail_ref, gq_ref, wqn_ref, wqa_ref, wqb_ref, gkv_ref, wk_ref, wv_ref,
                   cos_ref, sin_ref, fq_ref, fk_ref, fv_ref, fl_ref, qm_ref, km_ref, vm_ref):
    a = _rms(x_ref[0], g_ref[...]).astype(BF16)
    c0 = 3 * FOX_WIDTH
    c1 = c0 + MLA_Q_LORA
    c2 = c1 + MLA_KV_LORA

    def proj(lo, hi):
        if hi <= c0:
            return _dot(a, win_ref[:, lo:hi])
        return _dot(a, wtail_ref[:, lo - c0:hi - c0])

    fq_ref[0] = (proj(0, FOX_WIDTH) * FOX_QSCALE).astype(BF16)
    fk_ref[0] = proj(FOX_WIDTH, 2 * FOX_WIDTH).astype(BF16)
    fv_ref[0] = proj(2 * FOX_WIDTH, c0).astype(BF16)
    qn = _rms(proj(c0, c1), gq_ref[...]).astype(BF16)
    kvn = _rms(proj(c1, c2), gkv_ref[...]).astype(BF16)
    fl_ref[0] = proj(c2, c2 + LANES)
    cos = cos_ref[...]
    sin = sin_ref[...]
    pairs = MLA_HEADS // 2
    kpe = (proj(c2 + LANES, c2 + 2 * LANES) * cos + proj(c2 + 2 * LANES, c2 + 3 * LANES) * sin).astype(BF16)
    cos4 = jnp.concatenate([cos] * pairs, axis=1)
    sin4 = jnp.concatenate([sin] * pairs, axis=1)
    q_nope = (_dot(qn, wqn_ref[...]) * MLA_QSCALE).astype(BF16)
    q_rope = ((_dot(qn, wqa_ref[...]) * cos4 + _dot(qn, wqb_ref[...]) * sin4) * MLA_QSCALE).astype(BF16)
    qm_ref[0] = _interleave_blocks(q_nope, q_rope)
    k_nope = _dot(kvn, wk_ref[...]).astype(BF16)
    km_ref[0] = _interleave_blocks(k_nope, jnp.concatenate([kpe] * pairs, axis=1))
    vm_ref[0] = _dot(kvn, wv_ref[...]).astype(BF16)


def _inproj(x, g_mix, w_in_bf, w_tail, g_q, wqn, wqa, wqb, g_kv, wk, wv, cos_t, sin_t):
    b, s, d = x.shape
    ts = min(TS_PROJ, s)

    def full(arr):
        return pl.BlockSpec(arr.shape, lambda bi, si: (0,) * arr.ndim)

    def tok(width):
        return pl.BlockSpec((1, ts, width), lambda bi, si: (bi, si, 0))

    tab = pl.BlockSpec((ts, LANES), lambda bi, si: (si, 0))
    pair_w = 2 * LANES * (MLA_HEADS // 2)
    out_shapes = (
        jax.ShapeDtypeStruct((b, s, FOX_WIDTH), BF16),
        jax.ShapeDtypeStruct((b, s, FOX_WIDTH), BF16),
        jax.ShapeDtypeStruct((b, s, FOX_WIDTH), BF16),
        jax.ShapeDtypeStruct((b, s, LANES), F32),
        jax.ShapeDtypeStruct((b, s, pair_w), BF16),
        jax.ShapeDtypeStruct((b, s, pair_w), BF16),
        jax.ShapeDtypeStruct((b, s, MLA_WIDTH), BF16),
    )
    return pl.pallas_call(
        _inproj_kernel,
        grid=(b, s // ts),
        in_specs=[tok(d), full(g_mix), full(w_in_bf), full(w_tail), full(g_q), full(wqn), full(wqa), full(wqb),
                  full(g_kv), full(wk), full(wv), tab, tab],
        out_specs=(tok(FOX_WIDTH), tok(FOX_WIDTH), tok(FOX_WIDTH), tok(LANES), tok(pair_w), tok(pair_w),
                   tok(MLA_WIDTH)),
        out_shape=out_shapes,
        compiler_params=_cparams(2),
        name="inproj",
    )(x, g_mix, w_in_bf, w_tail, g_q, wqn, wqa, wqb, g_kv, wk, wv, cos_t, sin_t)


def _decay_kernel(fl_ref, bf_ref, place_ref, ones_ref, cq_ref, ck_ref, *, blk):
    s = fl_ref.shape[1]
    row = lax.broadcasted_iota(I32, (blk, blk), 0)
    col = lax.broadcasted_iota(I32, (blk, blk), 1)
    tri = jnp.where(row >= col, 1.0, 0.0).astype(BF16)
    carry = jnp.zeros((1, LANES), F32)
    for i in range(s // blk):
        sl = slice(i * blk, (i + 1) * blk)
        z = fl_ref[0, sl, :] + bf_ref[...]
        lf = (jnp.minimum(z, 0.0) - jnp.log1p(jnp.exp(-jnp.abs(z)))) * LOG2E
        part = _dot(tri, jnp.concatenate(_split3(lf), axis=1))
        cs = part[:, :LANES] + part[:, LANES:2 * LANES] + part[:, 2 * LANES:] + carry
        carry = cs[blk - 1:blk, :]
        placed = _dot(jnp.concatenate(_split3(cs), axis=1), place_ref[...]) + ones_ref[...]
        cq_ref[0, sl, :] = placed[:, :LANES].astype(BF16)
        ck_ref[0, sl, :] = placed[:, LANES:].astype(BF16)


def _decay_tables():
    place = np.zeros((3 * LANES, 2 * LANES), np.float32)
    ones = np.zeros((1, 2 * LANES), np.float32)
    for h in range(FOX_HEADS):
        for part in range(3):
            place[part * LANES + h, FOX_DECAY_LANES * h + part] = 1.0
            place[part * LANES + h, LANES + FOX_DECAY_LANES * h + 3 + part] = -1.0
            ones[0, FOX_DECAY_LANES * h + 3 + part] = 1.0
            ones[0, LANES + FOX_DECAY_LANES * h + part] = 1.0
    return jnp.asarray(place, BF16), jnp.asarray(ones)


def _decay(flog, b_f):
    b, s, _ = flog.shape
    blk = min(DECAY_BLK, s)
    place, ones = _decay_tables()
    bf = jnp.zeros((1, LANES), F32).at[0, :FOX_HEADS].set(b_f)

    def full(arr):
        return pl.BlockSpec(arr.shape, lambda bi: (0,) * arr.ndim)

    seq = pl.BlockSpec((1, s, LANES), lambda bi: (bi, 0, 0))
    return pl.pallas_call(
        functools.partial(_decay_kernel, blk=blk),
        grid=(b,),
        in_specs=[seq, full(bf), full(place), full(ones)],
        out_specs=(seq, seq),
        out_shape=(jax.ShapeDtypeStruct((b, s, LANES), BF16), jax.ShapeDtypeStruct((b, s, LANES), BF16)),
        compiler_params=_cparams(1),
        name="decay",
    )(flog, bf, place, ones)


def _attn_pipe_kernel(*refs, fox, t, nq, pairs):
    if fox:
        q_ref, cq_ref, k_ref, ck_ref, v_ref, o_ref, s_even, s_odd = refs
        group = FOX_DECAY_LANES
    else:
        q_ref, k_ref, v_ref, o_ref, s_even, s_odd = refs
        group = MLA_ROPE_DIM
    g = pl.program_id(0)
    i = g % nq
    pair = (jnp.minimum(g, pl.num_programs(0) - 2) // nq) % pairs
    base = 2 * pair if fox else 0
    lane = lax.broadcasted_iota(I32, (1, LANES), 1)

    @pl.when(g == 0)
    def _():
        s_odd[...] = jnp.zeros_like(s_odd)

    def body(iv):
        s_new, s_old = (s_even, s_odd) if iv % 2 == 0 else (s_odd, s_even)
        half = t // 2
        row = lax.broadcasted_iota(I32, (half, half), 0)
        col = lax.broadcasted_iota(I32, (half, half), 1)
        allowed = (col <= row) if fox else ((col // CHUNK) <= (row // CHUNK))
        past = iv * t
        if fox:
            q_main, q_extra = q_ref[0], cq_ref[0]
        else:
            q_main, q_extra = q_ref[0, :, :LANES], q_ref[0, :, LANES:]
        zero = jnp.zeros_like(q_main)

        def keys(lo, hi):
            if fox:
                return jnp.concatenate([k_ref[0, lo:hi, :], ck_ref[0, lo:hi, :]], axis=1)
            return k_ref[0, lo:hi, :]

        qa = [jnp.concatenate([jnp.where((lane // (LANES // 2)) == hh, q_main, zero),
                               jnp.where((lane // group) == base + hh, q_extra, zero)], axis=1) for hh in range(2)]
        for hh in range(2):
            if iv > 0:
                s_new[hh, :, 0:past] = _dot_nt(qa[hh], keys(0, past))
            upper = _dot_nt(qa[hh][:half], keys(past, past + half))
            s_new[hh, :half, past:past + half] = jnp.where(allowed, upper, NEG)
            lower = _dot_nt(qa[hh][half:], keys(past, past + t))
            s_new[hh, half:, past:past + half] = lower[:, :half]
            s_new[hh, half:, past + half:past + t] = jnp.where(allowed, lower[:, half:], NEG)

        seen = ((iv - 1) % nq + 1) * t
        outs = []
        for hh in range(2):
            sc = s_old[hh, :, 0:seen - half]
            sc_tail = s_old[hh, half:, seen - half:seen]
            m_main = jnp.max(sc, axis=-1, keepdims=True)
            m_low = jnp.maximum(m_main[half:], jnp.max(sc_tail, axis=-1, keepdims=True))
            p = jnp.exp2((sc - jnp.concatenate([m_main[:half], m_low], axis=0)).astype(BF16))
            p_tail = jnp.exp2((sc_tail - m_low).astype(BF16))
            own = (lane // (LANES // 2)) == hh
            one = jnp.ones((1, LANES), BF16)
            acc = _dot(p, jnp.where(own, v_ref[0, 0:seen - half, :], one))
            acc_tail = _dot(p_tail, jnp.where(own, v_ref[0, seen - half:seen, :], one))
            acc = jnp.concatenate([acc[:half], acc[half:] + acc_tail], axis=0)
            other = (1 - hh) * (LANES // 2)
            outs.append(acc * (1.0 / acc[:, other:other + 1]))
        o_ref[0] = jnp.where(lane < LANES // 2, outs[0], outs[1]).astype(o_ref.dtype)

    for iv in range(nq):
        pl.when(i == iv)(functools.partial(body, iv))


def _attention_pipelined(fox, q, k, v, cq=None, ck=None):
    b, s, _ = v.shape
    t = min(T_ATT, s)
    nq = s // t
    assert nq % 2 == 0
    pairs = v.shape[2] // LANES
    qw = q.shape[2] // pairs
    kw = k.shape[2] // pairs
    items = b * pairs * nq

    def item(g):
        return g // (pairs * nq), (g // nq) % pairs, g % nq

    def cur(g):
        return item(jnp.minimum(g, items - 1))

    def prev(g):
        return item(jnp.maximum(g - 1, 0))

    qspec = pl.BlockSpec((1, t, qw), lambda g: (cur(g)[0], cur(g)[2], cur(g)[1]))
    kspec = pl.BlockSpec((1, s, kw), lambda g: (cur(g)[0], 0, cur(g)[1]))
    vspec = pl.BlockSpec((1, s, LANES), lambda g: (prev(g)[0], 0, prev(g)[1]))
    ospec = pl.BlockSpec((1, t, LANES), lambda g: (prev(g)[0], prev(g)[2], prev(g)[1]))
    if fox:
        cqspec = pl.BlockSpec((1, t, LANES), lambda g: (cur(g)[0], cur(g)[2], 0))
        ckspec = pl.BlockSpec((1, s, LANES), lambda g: (cur(g)[0], 0, 0))
        in_specs = [qspec, cqspec, kspec, ckspec, vspec]
        args = (q, cq, k, ck, v)
    else:
        in_specs = [qspec, kspec, vspec]
        args = (q, k, v)
    return pl.pallas_call(
        functools.partial(_attn_pipe_kernel, fox=fox, t=t, nq=nq, pairs=pairs),
        grid=(items + 1,),
        in_specs=in_specs,
        out_specs=ospec,
        out_shape=jax.ShapeDtypeStruct((b, s, pairs * LANES), BF16),
        scratch_shapes=[pltpu.VMEM((2, t, s), F32), pltpu.VMEM((2, t, s), F32)],
        compiler_params=_cparams(1),
        name="fox_attn" if fox else "mla_attn",
    )(*args)


def _memkv_kernel(mem_ref, g_ref, w_ref, k_ref, v_ref):
    a = _rms(mem_ref[0], g_ref[...]).astype(BF16)
    k_ref[0] = _dot(a, w_ref[:, 0:MEM_WIDTH]).astype(BF16)
    v_ref[0] = _dot(a, w_ref[:, MEM_WIDTH:2 * MEM_WIDTH]).astype(BF16)


def _memkv(mem, g, w):
    b, m, d = mem.shape
    kv = pl.BlockSpec((1, m, MEM_WIDTH), lambda bi: (bi, 0, 0))
    return pl.pallas_call(
        _memkv_kernel,
        grid=(b,),
        in_specs=[pl.BlockSpec((1, m, d), lambda bi: (bi, 0, 0)), pl.BlockSpec(g.shape, lambda bi: (0, 0)),
                  pl.BlockSpec(w.shape, lambda bi: (0, 0))],
        out_specs=(kv, kv),
        out_shape=(jax.ShapeDtypeStruct((b, m, MEM_WIDTH), BF16), jax.ShapeDtypeStruct((b, m, MEM_WIDTH), BF16)),
        compiler_params=_cparams(1),
        name="memkv",
    )(mem, g, w)


def _store_token_tiles(ref, val):
    rows = val.shape[0]
    for j in range(SUBLANES):
        ref[pl.ds(j, rows, stride=SUBLANES), :] = val[:, j * LANES:(j + 1) * LANES]


def _load_token_tiles(ref, rows):
    return jnp.concatenate([ref[pl.ds(j, rows, stride=SUBLANES), :] for j in range(SUBLANES)], axis=1)


def _memrouter_kernel(of_ref, om_ref, x_ref, gf_ref, gm_ref, wmix_ref, gq_ref, wq_ref, k_ref, v_ref, wo_ref,
                      gffn_ref, wrh_ref, wrl_ref, br_ref, tri_ref, h2_ref, at_ref, meta_ref, metat_ref, cnt_ref, carry_ref,
                      *, ts):
    first = jnp.logical_and(pl.program_id(0) == 0, pl.program_id(1) == 0)

    @pl.when(first)
    def _():
        carry_ref[...] = jnp.zeros_like(carry_ref)

    nf = _rms(of_ref[0].astype(F32), gf_ref[...]).astype(BF16)
    nm = _rms(om_ref[0].astype(F32), gm_ref[...]).astype(BF16)
    h1 = x_ref[0] + _dot(nf, wmix_ref[0:FOX_WIDTH, :]) + _dot(nm, wmix_ref[FOX_WIDTH:FOX_WIDTH + MLA_WIDTH, :])

    q = (_dot(_rms(h1, gq_ref[...]).astype(BF16), wq_ref[...]) * MEM_QSCALE).astype(BF16)
    heads = []
    for h in range(MEM_HEADS):
        sl = slice(h * MEM_HEAD_DIM, (h + 1) * MEM_HEAD_DIM)
        sc = _dot_nt(q[:, sl], k_ref[0, :, sl])
        m = jnp.max(sc, axis=-1, keepdims=True)
        pm = jnp.exp2(sc - m)
        l = jnp.sum(pm, axis=-1, keepdims=True)
        heads.append((_dot(pm.astype(BF16), v_ref[0, :, sl]) / l).astype(BF16))
    h2 = h1 + _dot(jnp.concatenate(heads, axis=1), wo_ref[...])
    h2_ref[0] = h2

    a = _rms(h2, gffn_ref[...])
    _store_token_tiles(at_ref, a)
    a_hi = a.astype(BF16)
    a_lo = (a - a_hi.astype(F32)).astype(BF16)
    logits = _dot(a_hi, wrh_ref[...]) + _dot(a_lo, wrh_ref[...]) + _dot(a_hi, wrl_ref[...]) + br_ref[...]
    lane = lax.broadcasted_iota(I32, (ts, LANES), 1)
    lane_f = lane.astype(F32)
    work = jnp.where(lane < N_EXPERTS, logits, NEG)
    vals, idxs, sels = [], [], []
    for _ in range(TOP_K):
        mk = jnp.max(work, axis=-1, keepdims=True)
        ik = jnp.min(jnp.where(work == mk, lane_f, float(LANES)), axis=-1, keepdims=True)
        sel = lane_f == ik
        work = jnp.where(sel, NEG, work)
        vals.append(mk)
        idxs.append(ik)
        sels.append(sel)
    exps = [jnp.exp(vk - vals[0]) for vk in vals]
    denom = exps[0] + exps[1] + exps[2] + exps[3]
    chosen = jnp.logical_or(jnp.logical_or(sels[0], sels[1]), jnp.logical_or(sels[2], sels[3]))
    onehot = jnp.where(chosen, 1.0, 0.0)
    carry = carry_ref[...]
    rank = _dot(tri_ref[...], onehot.astype(BF16)) + carry
    carry_new = carry + jnp.sum(onehot, axis=0, keepdims=True)
    carry_ref[...] = carry_new
    cnt_ref[...] = carry_new
    meta = jnp.zeros((ts, LANES), F32)
    for kk in range(TOP_K):
        rk = jnp.sum(jnp.where(sels[kk], rank, 0.0), axis=-1, keepdims=True)
        meta = jnp.where(lane == kk, idxs[kk], meta)
        meta = jnp.where(lane == TOP_K + kk, rk, meta)
        meta = jnp.where(lane == 2 * TOP_K + kk, exps[kk] / denom, meta)
    meta_ref[0] = meta
    metat_ref[...] = jnp.transpose(meta)[0:META_ROWS, :]


def _memrouter(o_fox, o_mla, x, g_fox, g_mla, w_mix, g_mq, w_mq, kmem, vmem, w_mo, g_ffn, wr_hi, wr_lo, b_r):
    b, s, d = x.shape
    m = kmem.shape[1]
    ts = min(TS_MEM, s)

    def full(arr):
        return pl.BlockSpec(arr.shape, lambda bi, si: (0,) * arr.ndim)

    def tok(width):
        return pl.BlockSpec((1, ts, width), lambda bi, si: (bi, si, 0))

    kv = pl.BlockSpec((1, m, MEM_WIDTH), lambda bi, si: (bi, 0, 0))
    cnt = pl.BlockSpec((1, LANES), lambda bi, si: (0, 0))
    tri = jnp.asarray(np.tril(np.ones((ts, ts), np.float32), k=-1), BF16)
    return pl.pallas_call(
        functools.partial(_memrouter_kernel, ts=ts),
        grid=(b, s // ts),
        in_specs=[tok(FOX_WIDTH), tok(MLA_WIDTH), tok(d), full(g_fox), full(g_mla), full(w_mix), full(g_mq),
                  full(w_mq), kv, kv, full(w_mo), full(g_ffn), full(wr_hi), full(wr_lo), full(b_r), full(tri)],
        out_specs=(tok(d), pl.BlockSpec((ts * SUBLANES, LANES), lambda bi, si: (bi * (s // ts) + si, 0)),
                   tok(LANES), pl.BlockSpec((META_ROWS, ts), lambda bi, si: (0, bi * (s // ts) + si)), cnt),
        out_shape=(jax.ShapeDtypeStruct((b, s, d), F32), jax.ShapeDtypeStruct((b * s * SUBLANES, LANES), F32),
                   jax.ShapeDtypeStruct((b, s, LANES), F32), jax.ShapeDtypeStruct((META_ROWS, b * s), F32),
                   jax.ShapeDtypeStruct((1, LANES), F32)),
        scratch_shapes=[pltpu.VMEM((1, LANES), F32)],
        compiler_params=_cparams(2),
        name="memrouter",
    )(o_fox, o_mla, x, g_fox, g_mla, w_mix, g_mq, w_mq, kmem, vmem, w_mo, g_ffn, wr_hi, wr_lo, b_r, tri)


def _tile_copy(src_ref, src_row8, dst_ref, dst_row8, sem):
    return pltpu.make_async_copy(src_ref.at[pl.ds(pl.multiple_of(src_row8, SUBLANES), SUBLANES), :],
                                 dst_ref.at[pl.ds(pl.multiple_of(dst_row8, SUBLANES), SUBLANES), :], sem)


def _dispatch_kernel(last_ref, cnt_ref, nv_ref, pos_ref, at_ref, xs_ref, zbuf, zsem, sem, *, td, tm, min_used,
                     n_tiles):
    @pl.when(pl.program_id(0) == 0)
    def _():
        zbuf[...] = jnp.zeros_like(zbuf)

        def fill(row):
            return pltpu.make_async_copy(
                zbuf, xs_ref.at[pl.ds(pl.multiple_of(row * SUBLANES, SUBLANES), tm * SUBLANES), :], zsem)

        fills = [(cnt_ref[e] > 0, last_ref[e]) for e in range(N_EXPERTS)]
        fills += [(t >= nv_ref[0], t * tm) for t in range(min_used, n_tiles)]
        for pred, row in fills:
            pl.when(pred)(lambda row=row: fill(row).start())
        for pred, row in fills:
            pl.when(pred)(lambda row=row: fill(row).wait())

    def issue(r, c):
        for kk in range(TOP_K):
            _tile_copy(at_ref, r * SUBLANES, xs_ref, pos_ref[kk * td + r], sem).start(priority=kk % 2)
        return c

    lax.fori_loop(0, td, issue, 0, unroll=8)
    for _ in range(TOP_K):
        pltpu.make_async_copy(at_ref, xs_ref.at[pl.ds(0, td * SUBLANES), :], sem).wait()


def _dispatch(last_tile_row, cnt, nv, pos8, at, n_tiles, tm):
    n = at.shape[0] // SUBLANES
    td = min(TD_ROWS, n)
    grid_spec = pltpu.PrefetchScalarGridSpec(
        num_scalar_prefetch=3,
        grid=(n // td,),
        in_specs=[pl.BlockSpec((TOP_K * td,), lambda i, *_: (i,), memory_space=pltpu.SMEM),
                  pl.BlockSpec((td * SUBLANES, LANES), lambda i, *_: (i, 0))],
        out_specs=pl.BlockSpec(memory_space=pl.ANY),
        scratch_shapes=[pltpu.VMEM((tm * SUBLANES, LANES), F32), pltpu.SemaphoreType.DMA, pltpu.SemaphoreType.DMA],
    )
    min_used = n * TOP_K // tm
    return pl.pallas_call(
        functools.partial(_dispatch_kernel, td=td, tm=tm, min_used=min_used, n_tiles=n_tiles),
        grid_spec=grid_spec,
        out_shape=jax.ShapeDtypeStruct((n_tiles * tm * SUBLANES, LANES), F32),
        compiler_params=_cparams(1),
        name="dispatch",
    )(last_tile_row, cnt, nv, pos8, at)


def _moe_kernel(te_ref, rows_ref, slot_ref, nxt_ref, xs_ref, wgu_hbm, bgu_ref, wd_hbm, bd_ref, y_ref,
                wgu_f32, wd_f32, wgu_bf, wd_bf, sem, *, ch, tm):
    i = pl.program_id(0)
    d_exp = wd_hbm.shape[1]
    rows_used = rows_ref[i]
    valid = rows_used > 0
    new_expert = jnp.logical_or(i == 0, te_ref[i] != te_ref[jnp.maximum(i - 1, 0)])

    def weight_copies(e, sl):
        return (pltpu.make_async_copy(wgu_hbm.at[e], wgu_f32.at[sl], sem.at[sl, 0]),
                pltpu.make_async_copy(wd_hbm.at[e], wd_f32.at[sl], sem.at[sl, 1]))

    @pl.when(i == 0)
    def _():
        for c in weight_copies(te_ref[0], slot_ref[0]):
            c.start()

    @pl.when(jnp.logical_and(new_expert, valid))
    def _():
        sl = slot_ref[i]
        for c in weight_copies(te_ref[i], sl):
            c.wait()
        wgu_bf[...] = wgu_f32[sl].astype(BF16)
        wd_bf[...] = wd_f32[sl].astype(BF16)

        @pl.when(nxt_ref[i] >= 0)
        def _():
            for c in weight_copies(nxt_ref[i], 1 - sl):
                c.start()

    def ffn(rows):
        x = _load_token_tiles(xs_ref, rows).astype(BF16)
        acc = jnp.zeros((rows, wd_hbm.shape[2]), F32)
        for c in range(d_exp // ch):
            def gu(lo):
                return _dot(x, wgu_bf[:, lo:lo + ch]) + bgu_ref[0, :, lo:lo + ch]

            gate = jnp.minimum(gu(c * ch), SWIGLU_LIMIT)
            up = jnp.clip(gu(d_exp + c * ch), -SWIGLU_LIMIT, SWIGLU_LIMIT)
            act = gate * (1.0 / (1.0 + jnp.exp(-SWIGLU_ALPHA * gate))) * (up + 1.0)
            acc = acc + _dot(act.astype(BF16), wd_bf[c * ch:(c + 1) * ch, :])
        return acc + bd_ref[0]

    grp = tm // MOE_ROW_GROUPS
    for used in range(1, MOE_ROW_GROUPS + 1):
        rows = used * grp

        def partial_tile(rows=rows):
            _store_token_tiles(y_ref, ffn(rows))
            if rows < tm:
                y_ref[rows * SUBLANES:, :] = jnp.zeros(((tm - rows) * SUBLANES, LANES), F32)

        pl.when(jnp.logical_and(rows_used > rows - grp, rows_used <= rows))(partial_tile)

    @pl.when(jnp.logical_not(valid))
    def _():
        y_ref[...] = jnp.zeros_like(y_ref)


def _moe(te, rows, slot, nxt, xs, wgu, bgu, wd, bd, tm):
    n_tiles = xs.shape[0] // (tm * SUBLANES)
    _, d, d_exp2 = wgu.shape
    d_exp = wd.shape[1]
    assert d == SUBLANES * LANES and wd.shape[2] == d
    tile = pl.BlockSpec((tm * SUBLANES, LANES), lambda i, te, *_: (i, 0))
    grid_spec = pltpu.PrefetchScalarGridSpec(
        num_scalar_prefetch=4,
        grid=(n_tiles,),
        in_specs=[
            tile,
            pl.BlockSpec(memory_space=pl.ANY),
            pl.BlockSpec((1, 1, d_exp2), lambda i, te, *_: (te[i], 0, 0)),
            pl.BlockSpec(memory_space=pl.ANY),
            pl.BlockSpec((1, 1, d), lambda i, te, *_: (te[i], 0, 0)),
        ],
        out_specs=tile,
        scratch_shapes=[pltpu.VMEM((2, d, d_exp2), F32), pltpu.VMEM((2, d_exp, d), F32),
                        pltpu.VMEM((d, d_exp2), BF16), pltpu.VMEM((d_exp, d), BF16),
                        pltpu.SemaphoreType.DMA((2, 2))],
    )
    return pl.pallas_call(
        functools.partial(_moe_kernel, ch=min(MOE_CH, d_exp), tm=tm),
        grid_spec=grid_spec,
        out_shape=jax.ShapeDtypeStruct(xs.shape, F32),
        compiler_params=_cparams(1),
        name="moe",
    )(te, rows, slot, nxt, xs, wgu, bgu, wd, bd)


def _combine_kernel(pos_ref, posn_ref, h2_ref, meta_ref, y_ref, gf_ref, o_ref, ybuf, sem, *, td, final_norm):
    i = pl.program_id(0)
    slot = i % 2

    def issue_block(p_ref, sl):
        def issue(r, c):
            for kk in range(TOP_K):
                _tile_copy(y_ref, p_ref[kk * td + r], ybuf.at[sl, kk], r * SUBLANES,
                           sem.at[sl]).start(priority=kk % 2)
            return c

        lax.fori_loop(0, td, issue, 0, unroll=8)

    @pl.when(i == 0)
    def _():
        issue_block(pos_ref, 0)

    @pl.when(i + 1 < pl.num_programs(0))
    def _():
        issue_block(posn_ref, 1 - slot)

    for kk in range(TOP_K):
        pltpu.make_async_copy(y_ref.at[pl.ds(0, td * SUBLANES), :], ybuf.at[slot, kk], sem.at[slot]).wait()

    meta = meta_ref[...]
    acc = h2_ref[...]
    for kk in range(TOP_K):
        gate = meta[:, 2 * TOP_K + kk:2 * TOP_K + kk + 1]
        acc = acc + gate * _load_token_tiles(ybuf.at[slot, kk], td)
    o_ref[...] = _rms(acc, gf_ref[...]) if final_norm else acc


def _combine(pos8, h2, meta, y, g_final, final_norm):
    n, d = h2.shape
    td = min(TD_ROWS, n)
    n_blocks = n // td
    grid_spec = pltpu.PrefetchScalarGridSpec(
        num_scalar_prefetch=0,
        grid=(n_blocks,),
        in_specs=[pl.BlockSpec((TOP_K * td,), lambda i: (i,), memory_space=pltpu.SMEM),
                  pl.BlockSpec((TOP_K * td,), lambda i: (jnp.minimum(i + 1, n_blocks - 1),), memory_space=pltpu.SMEM),
                  pl.BlockSpec((td, d), lambda i: (i, 0)),
                  pl.BlockSpec((td, LANES), lambda i: (i, 0)),
                  pl.BlockSpec(memory_space=pl.ANY),
                  pl.BlockSpec(g_final.shape, lambda i: (0, 0))],
        out_specs=pl.BlockSpec((td, d), lambda i: (i, 0)),
        scratch_shapes=[pltpu.VMEM((2, TOP_K, td * SUBLANES, LANES), F32), pltpu.SemaphoreType.DMA((2,))],
    )
    return pl.pallas_call(
        functools.partial(_combine_kernel, td=td, final_norm=final_norm),
        grid_spec=grid_spec,
        out_shape=jax.ShapeDtypeStruct((n, d), F32),
        compiler_params=_cparams(1),
        name="combine",
    )(pos8, pos8, h2, meta, y, g_final)


def _rope_tables(seq_len):
    inv = 1.0 / (ROPE_BASE ** (jnp.arange(0, MLA_ROPE_DIM, 2, dtype=F32) / MLA_ROPE_DIM))
    ang = jnp.arange(seq_len, dtype=F32)[:, None] * inv[None, :]
    cos, sin = jnp.cos(ang), jnp.sin(ang)
    pad = jnp.zeros((seq_len, LANES - 2 * MLA_ROPE_DIM), F32)
    cos_t = jnp.concatenate([cos, cos, cos, cos, pad], axis=1)
    sin_t = jnp.concatenate([sin, sin, sin, sin, pad], axis=1)
    return cos_t, sin_t


def _rot_cols(w):
    half = MLA_ROPE_DIM // 2
    return jnp.concatenate([-w[:, half:], w[:, :half]], axis=1)


def _rope_pair_block(r0, r1):
    pad = jnp.zeros((r0.shape[0], LANES - 2 * MLA_ROPE_DIM), F32)
    return jnp.concatenate([r0, r1, pad], axis=1)


def _prep_inproj_weights(w_in, w_uq, w_ukv):
    d = w_in.shape[0]
    pts = np.cumsum((FOX_WIDTH, FOX_WIDTH, FOX_WIDTH, FOX_HEADS, MLA_Q_LORA, MLA_KV_LORA, MLA_ROPE_DIM))
    w_fl, w_ql = w_in[:, pts[2]:pts[3]], w_in[:, pts[3]:pts[4]]
    w_kvl, w_kr = w_in[:, pts[4]:pts[5]], w_in[:, pts[5]:pts[6]]
    misc = jnp.concatenate([w_fl, jnp.zeros((d, LANES - FOX_HEADS), F32)], axis=1)
    w_tail = jnp.concatenate([w_ql, w_kvl, misc, _rope_pair_block(w_kr, w_kr),
                              _rope_pair_block(_rot_cols(w_kr), _rot_cols(w_kr))], axis=1).astype(BF16)
    qd = MLA_NOPE_DIM + MLA_ROPE_DIM
    kvd = MLA_NOPE_DIM + MLA_V_DIM
    q_nope = [w_uq[:, h * qd:h * qd + MLA_NOPE_DIM] for h in range(MLA_HEADS)]
    q_rope = [w_uq[:, h * qd + MLA_NOPE_DIM:(h + 1) * qd] for h in range(MLA_HEADS)]
    wqa = [_rope_pair_block(q_rope[h], q_rope[h + 1]) for h in range(0, MLA_HEADS, 2)]
    wqb = [_rope_pair_block(_rot_cols(q_rope[h]), _rot_cols(q_rope[h + 1])) for h in range(0, MLA_HEADS, 2)]
    wk = [w_ukv[:, h * kvd:h * kvd + MLA_NOPE_DIM] for h in range(MLA_HEADS)]
    wv = [w_ukv[:, h * kvd + MLA_NOPE_DIM:(h + 1) * kvd] for h in range(MLA_HEADS)]
    cat = lambda xs: jnp.concatenate(xs, axis=1).astype(BF16)
    return w_in[:, :pts[2]].astype(BF16), w_tail, cat(q_nope), cat(wqa), cat(wqb), cat(wk), cat(wv)


def _routing_tables(counts, n_tiles, tm):
    tiles_e = (counts + tm - 1) // tm
    tile_end = jnp.cumsum(tiles_e)
    off = ((tile_end - tiles_e) * tm).astype(I32)
    total = tile_end[-1]
    ti = jnp.arange(n_tiles, dtype=I32)
    te = jnp.minimum(jnp.sum(ti[:, None] >= tile_end[None, :], axis=1), N_EXPERTS - 1).astype(I32)
    valid = ti < total
    last_e = jnp.sum(jnp.where(ti == total - 1, te, 0))
    te = jnp.where(valid, te, last_e).astype(I32)
    last_tile_row = (off + (tiles_e - 1) * tm).astype(I32)
    used = tiles_e > 0
    ids = jnp.arange(N_EXPERTS, dtype=I32)
    slot_e = (jnp.cumsum(used.astype(I32)) - 1) % 2
    later = jnp.where(used[None, :] & (ids[None, :] > ids[:, None]), ids[None, :], N_EXPERTS)
    nxt_e = jnp.min(later, axis=1)
    nxt_e = jnp.where(nxt_e == N_EXPERTS, -1, nxt_e).astype(I32)
    pick = te[:, None] == ids[None, :]
    slot = jnp.sum(jnp.where(pick, slot_e[None, :], 0), axis=1).astype(I32)
    nxt = jnp.sum(jnp.where(pick, nxt_e[None, :], 0), axis=1).astype(I32)
    end_row = jnp.sum(jnp.where(pick, (off + counts)[None, :], 0), axis=1)
    rows = jnp.where(valid, jnp.clip(end_row - ti * tm, 0, tm), 0).astype(I32)
    return off, last_tile_row, te, total.astype(I32).reshape(1), slot, nxt, rows


def kernel(x, mem, g_mix, w_in, b_f, g_q_lat, w_uq, g_kv_lat, w_ukv, g_fox_out, g_mla_out, w_o, g_mem_q, w_mem_q,
           g_mem_kv, w_mem_kv, w_mem_o, g_ffn, w_router, b_router, w_gate_up, b_gate_up, w_down, b_down, g_final):
    b, s, d = x.shape
    n = b * s
    depth = g_mix.shape[0]
    cos_t, sin_t = _rope_tables(s)
    row = lambda v: v.reshape(1, -1)
    h = x
    for l in range(depth):
        w_in_bf, w_tail, wqn, wqa, wqb, wk, wv = _prep_inproj_weights(w_in[l], w_uq[l], w_ukv[l])
        fq, fk, fv, flog, qm, km, vm = _inproj(h, row(g_mix[l]), w_in_bf, w_tail, row(g_q_lat[l]), wqn, wqa, wqb,
                                                row(g_kv_lat[l]), wk, wv, cos_t, sin_t)
        cq, ck = _decay(flog, b_f[l])
        o_fox = _attention_pipelined(True, fq, fk, fv, cq, ck)
        o_mla = _attention_pipelined(False, qm, km, vm)
        kmem, vmem = _memkv(mem, row(g_mem_kv[l]), w_mem_kv[l].astype(BF16))
        wr = jnp.zeros((d, LANES), F32).at[:, :N_EXPERTS].set(w_router[l])
        wr_hi = wr.astype(BF16)
        wr_lo = (wr - wr_hi.astype(F32)).astype(BF16)
        br = jnp.zeros((1, LANES), F32).at[0, :N_EXPERTS].set(b_router[l])
        h2, at, meta, meta_t, cnt = _memrouter(
            o_fox, o_mla, h, row(g_fox_out[l]), row(g_mla_out[l]), w_o[l].astype(BF16), row(g_mem_q[l]),
            w_mem_q[l].astype(BF16), kmem, vmem, w_mem_o[l].astype(BF16), row(g_ffn[l]), wr_hi, wr_lo, br)
        ek = meta_t[0:TOP_K].astype(I32)
        rk = meta_t[TOP_K:2 * TOP_K].astype(I32)
        n_tiles = n * TOP_K // TM_MOE + N_EXPERTS
        counts = cnt[0, :N_EXPERTS].astype(I32)
        off, last_tile_row, te, nv, slot, nxt, rows = _routing_tables(counts, n_tiles, TM_MOE)
        ids = jnp.arange(N_EXPERTS, dtype=I32)[:, None, None]
        off_of = jnp.sum(jnp.where(ek[None] == ids, off[:, None, None], 0), axis=0)
        pos8 = ((off_of + rk) * SUBLANES).astype(I32)
        td = min(TD_ROWS, n)
        pos8 = pos8.reshape(TOP_K, n // td, td).transpose(1, 0, 2).reshape(-1)
        xs = _dispatch(last_tile_row, counts, nv, pos8, at, n_tiles, TM_MOE)
        y = _moe(te, rows, slot, nxt, xs, w_gate_up[l], b_gate_up[l].reshape(N_EXPERTS, 1, -1), w_down[l],
                 b_down[l].reshape(N_EXPERTS, 1, -1), TM_MOE)
        h = _combine(pos8, h2.reshape(n, d), meta.reshape(n, LANES), y, row(g_final),
                     l == depth - 1).reshape(b, s, d)
    return h
```

```python
import functools

import numpy as np
import jax
import jax.numpy as jnp
from jax import lax
from jax.experimental import pallas as pl
from jax.experimental.pallas import tpu as pltpu

F32 = jnp.float32
BF16 = jnp.bfloat16
I32 = jnp.int32

LANES = 128
SUBLANES = 8
RMS_EPS = 1e-6
CHUNK = 64
FOX_HEADS = 8
FOX_HEAD_DIM = 64
FOX_WIDTH = FOX_HEADS * FOX_HEAD_DIM
MLA_HEADS = 8
MLA_Q_LORA = 384
MLA_KV_LORA = 256
MLA_NOPE_DIM = 64
MLA_ROPE_DIM = 32
MLA_V_DIM = 64
MLA_WIDTH = MLA_HEADS * MLA_V_DIM
ROPE_BASE = 10000.0
MEM_HEADS = 4
MEM_HEAD_DIM = 128
MEM_WIDTH = MEM_HEADS * MEM_HEAD_DIM
N_EXPERTS = 32
TOP_K = 4
SWIGLU_LIMIT = 7.0
SWIGLU_ALPHA = 1.702

LOG2E = 1.4426950408889634
NEG = -1e30
FOX_QSCALE = FOX_HEAD_DIM ** -0.5 * LOG2E
MLA_QSCALE = (MLA_NOPE_DIM + MLA_ROPE_DIM) ** -0.5 * LOG2E
MEM_QSCALE = MEM_HEAD_DIM ** -0.5 * LOG2E
FOX_DECAY_LANES = 16
META_ROWS = 16

VMEM_LIMIT = 56 * 1024 * 1024

TS_PROJ = 512
TS_MEM = 512
T_ATT = 512
DECAY_BLK = 256
TM_MOE = 512
MOE_ROW_GROUPS = 4
TD_ROWS = 512
MOE_CH = 512


def _cparams(n_axes):
    return pltpu.CompilerParams(dimension_semantics=("arbitrary",) * n_axes, vmem_limit_bytes=VMEM_LIMIT)


def _dot(a, b):
    return jnp.dot(a, b, preferred_element_type=F32)


def _dot_nt(a, b):
    return lax.dot_general(a, b, (((1,), (1,)), ((), ())), preferred_element_type=F32)


def _rms(x, g):
    return x * lax.rsqrt(jnp.mean(x * x, axis=-1, keepdims=True) + RMS_EPS) * g


def _split3(x):
    hi = x.astype(BF16)
    r1 = x - hi.astype(F32)
    mid = r1.astype(BF16)
    lo = (r1 - mid.astype(F32)).astype(BF16)
    return hi, mid, lo


def _interleave_blocks(a, b):
    parts = []
    for p in range(a.shape[1] // LANES):
        parts += [a[:, p * LANES:(p + 1) * LANES], b[:, p * LANES:(p + 1) * LANES]]
    return jnp.concatenate(parts, axis=1)


def _inproj_kernel(x_ref, g_ref, w1_ref, gq_ref, wqn_ref, wqa_ref, wqb_ref, gkv_ref, wk_ref, wv_ref,
                   cos_ref, sin_ref, fq_ref, fk_ref, fv_ref, fl_ref, qm_ref, km_ref, vm_ref):
    a = _rms(x_ref[0], g_ref[...]).astype(BF16)
    c0 = 3 * FOX_WIDTH
    c1 = c0 + MLA_Q_LORA
    c2 = c1 + MLA_KV_LORA

    def proj(lo, hi):
        return _dot(a, w1_ref[:, lo:hi])

    fq_ref[0] = (proj(0, FOX_WIDTH) * FOX_QSCALE).astype(BF16)
    fk_ref[0] = proj(FOX_WIDTH, 2 * FOX_WIDTH).astype(BF16)
    fv_ref[0] = proj(2 * FOX_WIDTH, c0).astype(BF16)
    qn = _rms(proj(c0, c1), gq_ref[...]).astype(BF16)
    kvn = _rms(proj(c1, c2), gkv_ref[...]).astype(BF16)
    fl_ref[0] = proj(c2, c2 + LANES)
    cos = cos_ref[...]
    sin = sin_ref[...]
    pairs = MLA_HEADS // 2
    kpe = (proj(c2 + LANES, c2 + 2 * LANES) * cos + proj(c2 + 2 * LANES, c2 + 3 * LANES) * sin).astype(BF16)
    cos4 = jnp.concatenate([cos] * pairs, axis=1)
    sin4 = jnp.concatenate([sin] * pairs, axis=1)
    q_nope = (_dot(qn, wqn_ref[...]) * MLA_QSCALE).astype(BF16)
    q_rope = ((_dot(qn, wqa_ref[...]) * cos4 + _dot(qn, wqb_ref[...]) * sin4) * MLA_QSCALE).astype(BF16)
    qm_ref[0] = _interleave_blocks(q_nope, q_rope)
    k_nope = _dot(kvn, wk_ref[...]).astype(BF16)
    km_ref[0] = _interleave_blocks(k_nope, jnp.concatenate([kpe] * pairs, axis=1))
    vm_ref[0] = _dot(kvn, wv_ref[...]).astype(BF16)


def _inproj(x, g_mix, w1, g_q, wqn, wqa, wqb, g_kv, wk, wv, cos_t, sin_t):
    b, s, d = x.shape
    ts = min(TS_PROJ, s)

    def full(arr):
        return pl.BlockSpec(arr.shape, lambda bi, si: (0,) * arr.ndim)

    def tok(width):
        return pl.BlockSpec((1, ts, width), lambda bi, si: (bi, si, 0))

    tab = pl.BlockSpec((ts, LANES), lambda bi, si: (si, 0))
    pair_w = 2 * LANES * (MLA_HEADS // 2)
    out_shapes = (
        jax.ShapeDtypeStruct((b, s, FOX_WIDTH), BF16),
        jax.ShapeDtypeStruct((b, s, FOX_WIDTH), BF16),
        jax.ShapeDtypeStruct((b, s, FOX_WIDTH), BF16),
        jax.ShapeDtypeStruct((b, s, LANES), F32),
        jax.ShapeDtypeStruct((b, s, pair_w), BF16),
        jax.ShapeDtypeStruct((b, s, pair_w), BF16),
        jax.ShapeDtypeStruct((b, s, MLA_WIDTH), BF16),
    )
    return pl.pallas_call(
        _inproj_kernel,
        grid=(b, s // ts),
        in_specs=[tok(d), full(g_mix), full(w1), full(g_q), full(wqn), full(wqa), full(wqb), full(g_kv), full(wk),
                  full(wv), tab, tab],
        out_specs=(tok(FOX_WIDTH), tok(FOX_WIDTH), tok(FOX_WIDTH), tok(LANES), tok(pair_w), tok(pair_w),
                   tok(MLA_WIDTH)),
        out_shape=out_shapes,
        compiler_params=_cparams(2),
        name="inproj",
    )(x, g_mix, w1, g_q, wqn, wqa, wqb, g_kv, wk, wv, cos_t, sin_t)


def _decay_kernel(fl_ref, bf_ref, place_ref, ones_ref, cq_ref, ck_ref, *, blk):
    s = fl_ref.shape[1]
    row = lax.broadcasted_iota(I32, (blk, blk), 0)
    col = lax.broadcasted_iota(I32, (blk, blk), 1)
    tri = jnp.where(row >= col, 1.0, 0.0).astype(BF16)
    carry = jnp.zeros((1, LANES), F32)
    for i in range(s // blk):
        sl = slice(i * blk, (i + 1) * blk)
        z = fl_ref[0, sl, :] + bf_ref[...]
        lf = (jnp.minimum(z, 0.0) - jnp.log1p(jnp.exp(-jnp.abs(z)))) * LOG2E
        part = _dot(tri, jnp.concatenate(_split3(lf), axis=1))
        cs = part[:, :LANES] + part[:, LANES:2 * LANES] + part[:, 2 * LANES:] + carry
        carry = cs[blk - 1:blk, :]
        placed = _dot(jnp.concatenate(_split3(cs), axis=1), place_ref[...]) + ones_ref[...]
        cq_ref[0, sl, :] = placed[:, :LANES].astype(BF16)
        ck_ref[0, sl, :] = placed[:, LANES:].astype(BF16)


def _decay_tables():
    place = np.zeros((3 * LANES, 2 * LANES), np.float32)
    ones = np.zeros((1, 2 * LANES), np.float32)
    for h in range(FOX_HEADS):
        for part in range(3):
            place[part * LANES + h, FOX_DECAY_LANES * h + part] = 1.0
            place[part * LANES + h, LANES + FOX_DECAY_LANES * h + 3 + part] = -1.0
            ones[0, FOX_DECAY_LANES * h + 3 + part] = 1.0
            ones[0, LANES + FOX_DECAY_LANES * h + part] = 1.0
    return jnp.asarray(place, BF16), jnp.asarray(ones)


def _decay(flog, b_f):
    b, s, _ = flog.shape
    blk = min(DECAY_BLK, s)
    place, ones = _decay_tables()
    bf = jnp.zeros((1, LANES), F32).at[0, :FOX_HEADS].set(b_f)

    def full(arr):
        return pl.BlockSpec(arr.shape, lambda bi: (0,) * arr.ndim)

    seq = pl.BlockSpec((1, s, LANES), lambda bi: (bi, 0, 0))
    return pl.pallas_call(
        functools.partial(_decay_kernel, blk=blk),
        grid=(b,),
        in_specs=[seq, full(bf), full(place), full(ones)],
        out_specs=(seq, seq),
        out_shape=(jax.ShapeDtypeStruct((b, s, LANES), BF16), jax.ShapeDtypeStruct((b, s, LANES), BF16)),
        compiler_params=_cparams(1),
        name="decay",
    )(flog, bf, place, ones)


def _attn_pipe_kernel(*refs, fox, t, nq, pairs):
    if fox:
        q_ref, cq_ref, k_ref, ck_ref, v_ref, o_ref, s_even, s_odd = refs
        group = FOX_DECAY_LANES
    else:
        q_ref, k_ref, v_ref, o_ref, s_even, s_odd = refs
        group = MLA_ROPE_DIM
    g = pl.program_id(0)
    i = g % nq
    pair = (jnp.minimum(g, pl.num_programs(0) - 2) // nq) % pairs
    base = 2 * pair if fox else 0
    lane = lax.broadcasted_iota(I32, (1, LANES), 1)

    @pl.when(g == 0)
    def _():
        s_odd[...] = jnp.zeros_like(s_odd)

    def body(iv):
        s_new, s_old = (s_even, s_odd) if iv % 2 == 0 else (s_odd, s_even)
        half = t // 2
        row = lax.broadcasted_iota(I32, (half, half), 0)
        col = lax.broadcasted_iota(I32, (half, half), 1)
        allowed = (col <= row) if fox else ((col // CHUNK) <= (row // CHUNK))
        past = iv * t
        if fox:
            q_main, q_extra = q_ref[0], cq_ref[0]
        else:
            q_main, q_extra = q_ref[0, :, :LANES], q_ref[0, :, LANES:]
        zero = jnp.zeros_like(q_main)

        def keys(lo, hi):
            if fox:
                return jnp.concatenate([k_ref[0, lo:hi, :], ck_ref[0, lo:hi, :]], axis=1)
            return k_ref[0, lo:hi, :]

        qa = [jnp.concatenate([jnp.where((lane // (LANES // 2)) == hh, q_main, zero),
                               jnp.where((lane // group) == base + hh, q_extra, zero)], axis=1) for hh in range(2)]
        for hh in range(2):
            if iv > 0:
                s_new[hh, :, 0:past] = _dot_nt(qa[hh], keys(0, past))
            upper = _dot_nt(qa[hh][:half], keys(past, past + half))
            s_new[hh, :half, past:past + half] = jnp.where(allowed, upper, NEG)
            lower = _dot_nt(qa[hh][half:], keys(past, past + t))
            s_new[hh, half:, past:past + half] = lower[:, :half]
            s_new[hh, half:, past + half:past + t] = jnp.where(allowed, lower[:, half:], NEG)

        seen = ((iv - 1) % nq + 1) * t
        outs = []
        for hh in range(2):
            sc = s_old[hh, :, 0:seen - half]
            sc_tail = s_old[hh, half:, seen - half:seen]
            m_main = jnp.max(sc, axis=-1, keepdims=True)
            m_low = jnp.maximum(m_main[half:], jnp.max(sc_tail, axis=-1, keepdims=True))
            p = jnp.exp2((sc - jnp.concatenate([m_main[:half], m_low], axis=0)).astype(BF16))
            p_tail = jnp.exp2((sc_tail - m_low).astype(BF16))
            own = (lane // (LANES // 2)) == hh
            one = jnp.ones((1, LANES), BF16)
            acc = _dot(p, jnp.where(own, v_ref[0, 0:seen - half, :], one))
            acc_tail = _dot(p_tail, jnp.where(own, v_ref[0, seen - half:seen, :], one))
            acc = jnp.concatenate([acc[:half], acc[half:] + acc_tail], axis=0)
            other = (1 - hh) * (LANES // 2)
            outs.append(acc * (1.0 / acc[:, other:other + 1]))
        o_ref[0] = jnp.where(lane < LANES // 2, outs[0], outs[1]).astype(o_ref.dtype)

    for iv in range(nq):
        pl.when(i == iv)(functools.partial(body, iv))


def _attention_pipelined(fox, q, k, v, cq=None, ck=None):
    b, s, _ = v.shape
    t = min(T_ATT, s)
    nq = s // t
    assert nq % 2 == 0
    pairs = v.shape[2] // LANES
    qw = q.shape[2] // pairs
    kw = k.shape[2] // pairs
    items = b * pairs * nq

    def item(g):
        return g // (pairs * nq), (g // nq) % pairs, g % nq

    def cur(g):
        return item(jnp.minimum(g, items - 1))

    def prev(g):
        return item(jnp.maximum(g - 1, 0))

    qspec = pl.BlockSpec((1, t, qw), lambda g: (cur(g)[0], cur(g)[2], cur(g)[1]))
    kspec = pl.BlockSpec((1, s, kw), lambda g: (cur(g)[0], 0, cur(g)[1]))
    vspec = pl.BlockSpec((1, s, LANES), lambda g: (prev(g)[0], 0, prev(g)[1]))
    ospec = pl.BlockSpec((1, t, LANES), lambda g: (prev(g)[0], prev(g)[2], prev(g)[1]))
    if fox:
        cqspec = pl.BlockSpec((1, t, LANES), lambda g: (cur(g)[0], cur(g)[2], 0))
        ckspec = pl.BlockSpec((1, s, LANES), lambda g: (cur(g)[0], 0, 0))
        in_specs = [qspec, cqspec, kspec, ckspec, vspec]
        args = (q, cq, k, ck, v)
    else:
        in_specs = [qspec, kspec, vspec]
        args = (q, k, v)
    return pl.pallas_call(
        functools.partial(_attn_pipe_kernel, fox=fox, t=t, nq=nq, pairs=pairs),
        grid=(items + 1,),
        in_specs=in_specs,
        out_specs=ospec,
        out_shape=jax.ShapeDtypeStruct((b, s, pairs * LANES), BF16),
        scratch_shapes=[pltpu.VMEM((2, t, s), F32), pltpu.VMEM((2, t, s), F32)],
        compiler_params=_cparams(1),
        name="fox_attn" if fox else "mla_attn",
    )(*args)


def _memkv_kernel(mem_ref, g_ref, w_ref, k_ref, v_ref):
    a = _rms(mem_ref[0], g_ref[...]).astype(BF16)
    k_ref[0] = _dot(a, w_ref[:, 0:MEM_WIDTH]).astype(BF16)
    v_ref[0] = _dot(a, w_ref[:, MEM_WIDTH:2 * MEM_WIDTH]).astype(BF16)


def _memkv(mem, g, w):
    b, m, d = mem.shape
    kv = pl.BlockSpec((1, m, MEM_WIDTH), lambda bi: (bi, 0, 0))
    return pl.pallas_call(
        _memkv_kernel,
        grid=(b,),
        in_specs=[pl.BlockSpec((1, m, d), lambda bi: (bi, 0, 0)), pl.BlockSpec(g.shape, lambda bi: (0, 0)),
                  pl.BlockSpec(w.shape, lambda bi: (0, 0))],
        out_specs=(kv, kv),
        out_shape=(jax.ShapeDtypeStruct((b, m, MEM_WIDTH), BF16), jax.ShapeDtypeStruct((b, m, MEM_WIDTH), BF16)),
        compiler_params=_cparams(1),
        name="memkv",
    )(mem, g, w)


def _store_token_tiles(ref, val):
    rows = val.shape[0]
    for j in range(SUBLANES):
        ref[pl.ds(j, rows, stride=SUBLANES), :] = val[:, j * LANES:(j + 1) * LANES]


def _load_token_tiles(ref, rows):
    return jnp.concatenate([ref[pl.ds(j, rows, stride=SUBLANES), :] for j in range(SUBLANES)], axis=1)


def _memrouter_kernel(of_ref, om_ref, x_ref, gf_ref, gm_ref, wmix_ref, gq_ref, wq_ref, k_ref, v_ref, wo_ref,
                      gffn_ref, wrh_ref, wrl_ref, br_ref, tri_ref, h2_ref, at_ref, meta_ref, metat_ref, cnt_ref, carry_ref,
                      *, ts):
    first = jnp.logical_and(pl.program_id(0) == 0, pl.program_id(1) == 0)

    @pl.when(first)
    def _():
        carry_ref[...] = jnp.zeros_like(carry_ref)

    nf = _rms(of_ref[0].astype(F32), gf_ref[...]).astype(BF16)
    nm = _rms(om_ref[0].astype(F32), gm_ref[...]).astype(BF16)
    h1 = x_ref[0] + _dot(nf, wmix_ref[0:FOX_WIDTH, :]) + _dot(nm, wmix_ref[FOX_WIDTH:FOX_WIDTH + MLA_WIDTH, :])

    q = (_dot(_rms(h1, gq_ref[...]).astype(BF16), wq_ref[...]) * MEM_QSCALE).astype(BF16)
    heads = []
    for h in range(MEM_HEADS):
        sl = slice(h * MEM_HEAD_DIM, (h + 1) * MEM_HEAD_DIM)
        sc = _dot_nt(q[:, sl], k_ref[0, :, sl])
        m = jnp.max(sc, axis=-1, keepdims=True)
        pm = jnp.exp2(sc - m)
        l = jnp.sum(pm, axis=-1, keepdims=True)
        heads.append((_dot(pm.astype(BF16), v_ref[0, :, sl]) / l).astype(BF16))
    h2 = h1 + _dot(jnp.concatenate(heads, axis=1), wo_ref[...])
    h2_ref[0] = h2

    a = _rms(h2, gffn_ref[...])
    _store_token_tiles(at_ref, a)
    a_hi = a.astype(BF16)
    a_lo = (a - a_hi.astype(F32)).astype(BF16)
    logits = _dot(a_hi, wrh_ref[...]) + _dot(a_lo, wrh_ref[...]) + _dot(a_hi, wrl_ref[...]) + br_ref[...]
    lane = lax.broadcasted_iota(I32, (ts, LANES), 1)
    lane_f = lane.astype(F32)
    work = jnp.where(lane < N_EXPERTS, logits, NEG)
    vals, idxs, sels = [], [], []
    for _ in range(TOP_K):
        mk = jnp.max(work, axis=-1, keepdims=True)
        ik = jnp.min(jnp.where(work == mk, lane_f, float(LANES)), axis=-1, keepdims=True)
        sel = lane_f == ik
        work = jnp.where(sel, NEG, work)
        vals.append(mk)
        idxs.append(ik)
        sels.append(sel)
    exps = [jnp.exp(vk - vals[0]) for vk in vals]
    denom = exps[0] + exps[1] + exps[2] + exps[3]
    chosen = jnp.logical_or(jnp.logical_or(sels[0], sels[1]), jnp.logical_or(sels[2], sels[3]))
    onehot = jnp.where(chosen, 1.0, 0.0)
    carry = carry_ref[...]
    rank = _dot(tri_ref[...], onehot.astype(BF16)) + carry
    carry_new = carry + jnp.sum(onehot, axis=0, keepdims=True)
    carry_ref[...] = carry_new
    cnt_ref[...] = carry_new
    meta = jnp.zeros((ts, LANES), F32)
    for kk in range(TOP_K):
        rk = jnp.sum(jnp.where(sels[kk], rank, 0.0), axis=-1, keepdims=True)
        meta = jnp.where(lane == kk, idxs[kk], meta)
        meta = jnp.where(lane == TOP_K + kk, rk, meta)
        meta = jnp.where(lane == 2 * TOP_K + kk, exps[kk] / denom, meta)
    meta_ref[0] = meta
    metat_ref[...] = jnp.transpose(meta)[0:META_ROWS, :]


def _memrouter(o_fox, o_mla, x, g_fox, g_mla, w_mix, g_mq, w_mq, kmem, vmem, w_mo, g_ffn, wr_hi, wr_lo, b_r):
    b, s, d = x.shape
    m = kmem.shape[1]
    ts = min(TS_MEM, s)

    def full(arr):
        return pl.BlockSpec(arr.shape, lambda bi, si: (0,) * arr.ndim)

    def tok(width):
        return pl.BlockSpec((1, ts, width), lambda bi, si: (bi, si, 0))

    kv = pl.BlockSpec((1, m, MEM_WIDTH), lambda bi, si: (bi, 0, 0))
    cnt = pl.BlockSpec((1, LANES), lambda bi, si: (0, 0))
    tri = jnp.asarray(np.tril(np.ones((ts, ts), np.float32), k=-1), BF16)
    return pl.pallas_call(
        functools.partial(_memrouter_kernel, ts=ts),
        grid=(b, s // ts),
        in_specs=[tok(FOX_WIDTH), tok(MLA_WIDTH), tok(d), full(g_fox), full(g_mla), full(w_mix), full(g_mq),
                  full(w_mq), kv, kv, full(w_mo), full(g_ffn), full(wr_hi), full(wr_lo), full(b_r), full(tri)],
        out_specs=(tok(d), pl.BlockSpec((ts * SUBLANES, LANES), lambda bi, si: (bi * (s // ts) + si, 0)),
                   tok(LANES), pl.BlockSpec((META_ROWS, ts), lambda bi, si: (0, bi * (s // ts) + si)), cnt),
        out_shape=(jax.ShapeDtypeStruct((b, s, d), F32), jax.ShapeDtypeStruct((b * s * SUBLANES, LANES), F32),
                   jax.ShapeDtypeStruct((b, s, LANES), F32), jax.ShapeDtypeStruct((META_ROWS, b * s), F32),
                   jax.ShapeDtypeStruct((1, LANES), F32)),
        scratch_shapes=[pltpu.VMEM((1, LANES), F32)],
        compiler_params=_cparams(2),
        name="memrouter",
    )(o_fox, o_mla, x, g_fox, g_mla, w_mix, g_mq, w_mq, kmem, vmem, w_mo, g_ffn, wr_hi, wr_lo, b_r, tri)


def _tile_copy(src_ref, src_row8, dst_ref, dst_row8, sem):
    return pltpu.make_async_copy(src_ref.at[pl.ds(pl.multiple_of(src_row8, SUBLANES), SUBLANES), :],
                                 dst_ref.at[pl.ds(pl.multiple_of(dst_row8, SUBLANES), SUBLANES), :], sem)


def _dispatch_kernel(last_ref, grp_ref, nv_ref, pos_ref, at_ref, xs_ref, zbuf, zsem, sem, *, td, tm, min_used,
                     n_tiles):
    @pl.when(pl.program_id(0) == 0)
    def _():
        zbuf[...] = jnp.zeros_like(zbuf)
        grp = tm // MOE_ROW_GROUPS

        def fill(row):
            return pltpu.make_async_copy(
                zbuf, xs_ref.at[pl.ds(pl.multiple_of(row * SUBLANES, SUBLANES), grp * SUBLANES), :], zsem)

        groups = range(MOE_ROW_GROUPS)
        fills = [(gi >= grp_ref[e], last_ref[e] + gi * grp) for e in range(N_EXPERTS) for gi in groups]
        fills += [(t >= nv_ref[0], t * tm + gi * grp) for t in range(min_used, n_tiles) for gi in groups]
        for pred, row in fills:
            pl.when(pred)(lambda row=row: fill(row).start())
        for pred, row in fills:
            pl.when(pred)(lambda row=row: fill(row).wait())

    def issue(r, c):
        for kk in range(TOP_K):
            _tile_copy(at_ref, r * SUBLANES, xs_ref, pos_ref[kk * td + r], sem).start(priority=kk % 2)
        return c

    lax.fori_loop(0, td, issue, 0, unroll=8)
    for _ in range(TOP_K):
        pltpu.make_async_copy(at_ref, xs_ref.at[pl.ds(0, td * SUBLANES), :], sem).wait()


def _dispatch(last_tile_row, last_group, nv, pos8, at, n_tiles, tm):
    n = at.shape[0] // SUBLANES
    td = min(TD_ROWS, n)
    grid_spec = pltpu.PrefetchScalarGridSpec(
        num_scalar_prefetch=3,
        grid=(n // td,),
        in_specs=[pl.BlockSpec((TOP_K * td,), lambda i, *_: (i,), memory_space=pltpu.SMEM),
                  pl.BlockSpec((td * SUBLANES, LANES), lambda i, *_: (i, 0))],
        out_specs=pl.BlockSpec(memory_space=pl.ANY),
        scratch_shapes=[pltpu.VMEM((tm // MOE_ROW_GROUPS * SUBLANES, LANES), F32), pltpu.SemaphoreType.DMA,
                        pltpu.SemaphoreType.DMA],
    )
    min_used = n * TOP_K // tm
    return pl.pallas_call(
        functools.partial(_dispatch_kernel, td=td, tm=tm, min_used=min_used, n_tiles=n_tiles),
        grid_spec=grid_spec,
        out_shape=jax.ShapeDtypeStruct((n_tiles * tm * SUBLANES, LANES), F32),
        compiler_params=_cparams(1),
        name="dispatch",
    )(last_tile_row, last_group, nv, pos8, at)


def _moe_kernel(te_ref, rows_ref, slot_ref, nxt_ref, xs_ref, wgu_hbm, bgu_ref, wd_hbm, bd_ref, y_ref,
                wgu_f32, wd_f32, wgu_bf, wd_bf, sem, *, ch, tm):
    i = pl.program_id(0)
    d_exp = wd_hbm.shape[1]
    rows_used = rows_ref[i]
    valid = rows_used > 0
    new_expert = jnp.logical_or(i == 0, te_ref[i] != te_ref[jnp.maximum(i - 1, 0)])

    def weight_copies(e, sl):
        return (pltpu.make_async_copy(wgu_hbm.at[e], wgu_f32.at[sl], sem.at[sl, 0]),
                pltpu.make_async_copy(wd_hbm.at[e], wd_f32.at[sl], sem.at[sl, 1]))

    @pl.when(i == 0)
    def _():
        for c in weight_copies(te_ref[0], slot_ref[0]):
            c.start()

    @pl.when(jnp.logical_and(new_expert, valid))
    def _():
        sl = slot_ref[i]
        for c in weight_copies(te_ref[i], sl):
            c.wait()
        wgu_bf[...] = wgu_f32[sl].astype(BF16)
        wd_bf[...] = wd_f32[sl].astype(BF16)

        @pl.when(nxt_ref[i] >= 0)
        def _():
            for c in weight_copies(nxt_ref[i], 1 - sl):
                c.start()

    def ffn(rows):
        x = _load_token_tiles(xs_ref, rows).astype(BF16)
        acc = jnp.zeros((rows, wd_hbm.shape[2]), F32)
        for c in range(d_exp // ch):
            def gu(lo):
                return _dot(x, wgu_bf[:, lo:lo + ch]) + bgu_ref[0, :, lo:lo + ch]

            gate = jnp.minimum(gu(c * ch), SWIGLU_LIMIT)
            up = jnp.clip(gu(d_exp + c * ch), -SWIGLU_LIMIT, SWIGLU_LIMIT)
            act = gate * (1.0 / (1.0 + jnp.exp(-SWIGLU_ALPHA * gate))) * (up + 1.0)
            acc = acc + _dot(act.astype(BF16), wd_bf[c * ch:(c + 1) * ch, :])
        return acc + bd_ref[0]

    grp = tm // MOE_ROW_GROUPS
    for used in range(1, MOE_ROW_GROUPS + 1):
        rows = used * grp

        def partial_tile(rows=rows):
            _store_token_tiles(y_ref, ffn(rows))
            if rows < tm:
                y_ref[rows * SUBLANES:, :] = jnp.zeros(((tm - rows) * SUBLANES, LANES), F32)

        pl.when(jnp.logical_and(rows_used > rows - grp, rows_used <= rows))(partial_tile)

    @pl.when(jnp.logical_not(valid))
    def _():
        y_ref[...] = jnp.zeros_like(y_ref)


def _moe(te, rows, slot, nxt, xs, wgu, bgu, wd, bd, tm):
    n_tiles = xs.shape[0] // (tm * SUBLANES)
    _, d, d_exp2 = wgu.shape
    d_exp = wd.shape[1]
    assert d == SUBLANES * LANES and wd.shape[2] == d
    tile = pl.BlockSpec((tm * SUBLANES, LANES), lambda i, te, *_: (i, 0))
    grid_spec = pltpu.PrefetchScalarGridSpec(
        num_scalar_prefetch=4,
        grid=(n_tiles,),
        in_specs=[
            tile,
            pl.BlockSpec(memory_space=pl.ANY),
            pl.BlockSpec((1, 1, d_exp2), lambda i, te, *_: (te[i], 0, 0)),
            pl.BlockSpec(memory_space=pl.ANY),
            pl.BlockSpec((1, 1, d), lambda i, te, *_: (te[i], 0, 0)),
        ],
        out_specs=tile,
        scratch_shapes=[pltpu.VMEM((2, d, d_exp2), F32), pltpu.VMEM((2, d_exp, d), F32),
                        pltpu.VMEM((d, d_exp2), BF16), pltpu.VMEM((d_exp, d), BF16),
                        pltpu.SemaphoreType.DMA((2, 2))],
    )
    return pl.pallas_call(
        functools.partial(_moe_kernel, ch=min(MOE_CH, d_exp), tm=tm),
        grid_spec=grid_spec,
        out_shape=jax.ShapeDtypeStruct(xs.shape, F32),
        compiler_params=_cparams(1),
        name="moe",
    )(te, rows, slot, nxt, xs, wgu, bgu, wd, bd)


def _combine_kernel(pos_ref, posn_ref, h2_ref, meta_ref, y_ref, gf_ref, o_ref, ybuf, sem, *, td, final_norm):
    i = pl.program_id(0)
    slot = i % 2

    def issue_block(p_ref, sl):
        def issue(r, c):
            for kk in range(TOP_K):
                _tile_copy(y_ref, p_ref[kk * td + r], ybuf.at[sl, kk], r * SUBLANES,
                           sem.at[sl]).start(priority=kk % 2)
            return c

        lax.fori_loop(0, td, issue, 0, unroll=8)

    @pl.when(i == 0)
    def _():
        issue_block(pos_ref, 0)

    @pl.when(i + 1 < pl.num_programs(0))
    def _():
        issue_block(posn_ref, 1 - slot)

    for kk in range(TOP_K):
        pltpu.make_async_copy(y_ref.at[pl.ds(0, td * SUBLANES), :], ybuf.at[slot, kk], sem.at[slot]).wait()

    meta = meta_ref[...]
    acc = h2_ref[...]
    for kk in range(TOP_K):
        gate = meta[:, 2 * TOP_K + kk:2 * TOP_K + kk + 1]
        acc = acc + gate * _load_token_tiles(ybuf.at[slot, kk], td)
    o_ref[...] = _rms(acc, gf_ref[...]) if final_norm else acc


def _combine(pos8, h2, meta, y, g_final, final_norm):
    n, d = h2.shape
    td = min(TD_ROWS, n)
    n_blocks = n // td
    grid_spec = pltpu.PrefetchScalarGridSpec(
        num_scalar_prefetch=0,
        grid=(n_blocks,),
        in_specs=[pl.BlockSpec((TOP_K * td,), lambda i: (i,), memory_space=pltpu.SMEM),
                  pl.BlockSpec((TOP_K * td,), lambda i: (jnp.minimum(i + 1, n_blocks - 1),), memory_space=pltpu.SMEM),
                  pl.BlockSpec((td, d), lambda i: (i, 0)),
                  pl.BlockSpec((td, LANES), lambda i: (i, 0)),
                  pl.BlockSpec(memory_space=pl.ANY),
                  pl.BlockSpec(g_final.shape, lambda i: (0, 0))],
        out_specs=pl.BlockSpec((td, d), lambda i: (i, 0)),
        scratch_shapes=[pltpu.VMEM((2, TOP_K, td * SUBLANES, LANES), F32), pltpu.SemaphoreType.DMA((2,))],
    )
    return pl.pallas_call(
        functools.partial(_combine_kernel, td=td, final_norm=final_norm),
        grid_spec=grid_spec,
        out_shape=jax.ShapeDtypeStruct((n, d), F32),
        compiler_params=_cparams(1),
        name="combine",
    )(pos8, pos8, h2, meta, y, g_final)


def _rope_tables(seq_len):
    inv = 1.0 / (ROPE_BASE ** (jnp.arange(0, MLA_ROPE_DIM, 2, dtype=F32) / MLA_ROPE_DIM))
    ang = jnp.arange(seq_len, dtype=F32)[:, None] * inv[None, :]
    cos, sin = jnp.cos(ang), jnp.sin(ang)
    pad = jnp.zeros((seq_len, LANES - 2 * MLA_ROPE_DIM), F32)
    cos_t = jnp.concatenate([cos, cos, cos, cos, pad], axis=1)
    sin_t = jnp.concatenate([sin, sin, sin, sin, pad], axis=1)
    return cos_t, sin_t


def _rot_cols(w):
    half = MLA_ROPE_DIM // 2
    return jnp.concatenate([-w[:, half:], w[:, :half]], axis=1)


def _rope_pair_block(r0, r1):
    pad = jnp.zeros((r0.shape[0], LANES - 2 * MLA_ROPE_DIM), F32)
    return jnp.concatenate([r0, r1, pad], axis=1)


def _prep_inproj_weights(w_in, w_uq, w_ukv):
    d = w_in.shape[0]
    pts = np.cumsum((FOX_WIDTH, FOX_WIDTH, FOX_WIDTH, FOX_HEADS, MLA_Q_LORA, MLA_KV_LORA, MLA_ROPE_DIM))
    w_fl, w_ql = w_in[:, pts[2]:pts[3]], w_in[:, pts[3]:pts[4]]
    w_kvl, w_kr = w_in[:, pts[4]:pts[5]], w_in[:, pts[5]:pts[6]]
    misc = jnp.concatenate([w_fl, jnp.zeros((d, LANES - FOX_HEADS), F32)], axis=1)
    w_tail = jnp.concatenate([w_ql, w_kvl, misc, _rope_pair_block(w_kr, w_kr),
                              _rope_pair_block(_rot_cols(w_kr), _rot_cols(w_kr))], axis=1).astype(BF16)
    qd = MLA_NOPE_DIM + MLA_ROPE_DIM
    kvd = MLA_NOPE_DIM + MLA_V_DIM
    q_nope = [w_uq[:, h * qd:h * qd + MLA_NOPE_DIM] for h in range(MLA_HEADS)]
    q_rope = [w_uq[:, h * qd + MLA_NOPE_DIM:(h + 1) * qd] for h in range(MLA_HEADS)]
    wqa = [_rope_pair_block(q_rope[h], q_rope[h + 1]) for h in range(0, MLA_HEADS, 2)]
    wqb = [_rope_pair_block(_rot_cols(q_rope[h]), _rot_cols(q_rope[h + 1])) for h in range(0, MLA_HEADS, 2)]
    wk = [w_ukv[:, h * kvd:h * kvd + MLA_NOPE_DIM] for h in range(MLA_HEADS)]
    wv = [w_ukv[:, h * kvd + MLA_NOPE_DIM:(h + 1) * kvd] for h in range(MLA_HEADS)]
    cat = lambda xs: jnp.concatenate(xs, axis=1).astype(BF16)
    w1 = jnp.concatenate([w_in[:, :pts[2]].astype(BF16), w_tail], axis=1)
    return w1, cat(q_nope), cat(wqa), cat(wqb), cat(wk), cat(wv)


def _routing_tables(counts, n_tiles, tm):
    tiles_e = (counts + tm - 1) // tm
    tile_end = jnp.cumsum(tiles_e)
    off = ((tile_end - tiles_e) * tm).astype(I32)
    total = tile_end[-1]
    ti = jnp.arange(n_tiles, dtype=I32)
    te = jnp.minimum(jnp.sum(ti[:, None] >= tile_end[None, :], axis=1), N_EXPERTS - 1).astype(I32)
    valid = ti < total
    last_e = jnp.sum(jnp.where(ti == total - 1, te, 0))
    te = jnp.where(valid, te, last_e).astype(I32)
    last_tile_row = (off + (tiles_e - 1) * tm).astype(I32)
    grp = tm // MOE_ROW_GROUPS
    last_group = jnp.where(counts > 0, ((counts - 1) % tm) // grp, MOE_ROW_GROUPS).astype(I32)
    used = tiles_e > 0
    ids = jnp.arange(N_EXPERTS, dtype=I32)
    slot_e = (jnp.cumsum(used.astype(I32)) - 1) % 2
    later = jnp.where(used[None, :] & (ids[None, :] > ids[:, None]), ids[None, :], N_EXPERTS)
    nxt_e = jnp.min(later, axis=1)
    nxt_e = jnp.where(nxt_e == N_EXPERTS, -1, nxt_e).astype(I32)
    pick = te[:, None] == ids[None, :]
    slot = jnp.sum(jnp.where(pick, slot_e[None, :], 0), axis=1).astype(I32)
    nxt = jnp.sum(jnp.where(pick, nxt_e[None, :], 0), axis=1).astype(I32)
    end_row = jnp.sum(jnp.where(pick, (off + counts)[None, :], 0), axis=1)
    rows = jnp.where(valid, jnp.clip(end_row - ti * tm, 0, tm), 0).astype(I32)
    return off, last_tile_row, last_group, te, total.astype(I32).reshape(1), slot, nxt, rows


def kernel(x, mem, g_mix, w_in, b_f, g_q_lat, w_uq, g_kv_lat, w_ukv, g_fox_out, g_mla_out, w_o, g_mem_q, w_mem_q,
           g_mem_kv, w_mem_kv, w_mem_o, g_ffn, w_router, b_router, w_gate_up, b_gate_up, w_down, b_down, g_final):
    b, s, d = x.shape
    n = b * s
    depth = g_mix.shape[0]
    cos_t, sin_t = _rope_tables(s)
    row = lambda v: v.reshape(1, -1)
    h = x
    for l in range(depth):
        w1, wqn, wqa, wqb, wk, wv = _prep_inproj_weights(w_in[l], w_uq[l], w_ukv[l])
        fq, fk, fv, flog, qm, km, vm = _inproj(h, row(g_mix[l]), w1, row(g_q_lat[l]), wqn, wqa, wqb,
                                                row(g_kv_lat[l]), wk, wv, cos_t, sin_t)
        cq, ck = _decay(flog, b_f[l])
        o_fox = _attention_pipelined(True, fq, fk, fv, cq, ck)
        o_mla = _attention_pipelined(False, qm, km, vm)
        kmem, vmem = _memkv(mem, row(g_mem_kv[l]), w_mem_kv[l].astype(BF16))
        wr = jnp.zeros((d, LANES), F32).at[:, :N_EXPERTS].set(w_router[l])
        wr_hi = wr.astype(BF16)
        wr_lo = (wr - wr_hi.astype(F32)).astype(BF16)
        br = jnp.zeros((1, LANES), F32).at[0, :N_EXPERTS].set(b_router[l])
        h2, at, meta, meta_t, cnt = _memrouter(
            o_fox, o_mla, h, row(g_fox_out[l]), row(g_mla_out[l]), w_o[l].astype(BF16), row(g_mem_q[l]),
            w_mem_q[l].astype(BF16), kmem, vmem, w_mem_o[l].astype(BF16), row(g_ffn[l]), wr_hi, wr_lo, br)
        ek = meta_t[0:TOP_K].astype(I32)
        rk = meta_t[TOP_K:2 * TOP_K].astype(I32)
        n_tiles = n * TOP_K // TM_MOE + N_EXPERTS
        counts = cnt[0, :N_EXPERTS].astype(I32)
        off, last_tile_row, last_group, te, nv, slot, nxt, rows = _routing_tables(counts, n_tiles, TM_MOE)
        ids = jnp.arange(N_EXPERTS, dtype=I32)[:, None, None]
        off_of = jnp.sum(jnp.where(ek[None] == ids, off[:, None, None], 0), axis=0)
        pos8 = ((off_of + rk) * SUBLANES).astype(I32)
        td = min(TD_ROWS, n)
        pos8 = pos8.reshape(TOP_K, n // td, td).transpose(1, 0, 2).reshape(-1)
        xs = _dispatch(last_tile_row, last_group, nv, pos8, at, n_tiles, TM_MOE)
        y = _moe(te, rows, slot, nxt, xs, w_gate_up[l], b_gate_up[l].reshape(N_EXPERTS, 1, -1), w_down[l],
                 b_down[l].reshape(N_EXPERTS, 1, -1), TM_MOE)
        h = _combine(pos8, h2.reshape(n, d), meta.reshape(n, LANES), y, row(g_final),
                     l == depth - 1).reshape(b, s, d)
    return h
```

```python
import functools

import numpy as np
import jax
import jax.numpy as jnp
from jax import lax
from jax.experimental import pallas as pl
from jax.experimental.pallas import tpu as pltpu

F32 = jnp.float32
BF16 = jnp.bfloat16
I32 = jnp.int32

LANES = 128
SUBLANES = 8
RMS_EPS = 1e-6
CHUNK = 64
FOX_HEADS = 8
FOX_HEAD_DIM = 64
FOX_WIDTH = FOX_HEADS * FOX_HEAD_DIM
MLA_HEADS = 8
MLA_Q_LORA = 384
MLA_KV_LORA = 256
MLA_NOPE_DIM = 64
MLA_ROPE_DIM = 32
MLA_V_DIM = 64
MLA_WIDTH = MLA_HEADS * MLA_V_DIM
ROPE_BASE = 10000.0
MEM_HEADS = 4
MEM_HEAD_DIM = 128
MEM_WIDTH = MEM_HEADS * MEM_HEAD_DIM
N_EXPERTS = 32
TOP_K = 4
SWIGLU_LIMIT = 7.0
SWIGLU_ALPHA = 1.702

LOG2E = 1.4426950408889634
NEG = -1e30
FOX_QSCALE = FOX_HEAD_DIM ** -0.5 * LOG2E
MLA_QSCALE = (MLA_NOPE_DIM + MLA_ROPE_DIM) ** -0.5 * LOG2E
MEM_QSCALE = MEM_HEAD_DIM ** -0.5 * LOG2E
FOX_DECAY_LANES = 16
META_ROWS = 16

VMEM_LIMIT = 56 * 1024 * 1024

TS_PROJ = 512
TS_MEM = 512
T_ATT = 512
DECAY_BLK = 256
TM_MOE = 512
MOE_ROW_GROUPS = 4
TD_DISPATCH = 1024
TD_COMBINE = 512
MOE_CH = 512


def _cparams(n_axes):
    return pltpu.CompilerParams(dimension_semantics=("arbitrary",) * n_axes, vmem_limit_bytes=VMEM_LIMIT)


def _dot(a, b):
    return jnp.dot(a, b, preferred_element_type=F32)


def _dot_nt(a, b):
    return lax.dot_general(a, b, (((1,), (1,)), ((), ())), preferred_element_type=F32)


def _rms(x, g):
    return x * lax.rsqrt(jnp.mean(x * x, axis=-1, keepdims=True) + RMS_EPS) * g


def _split3(x):
    hi = x.astype(BF16)
    r1 = x - hi.astype(F32)
    mid = r1.astype(BF16)
    lo = (r1 - mid.astype(F32)).astype(BF16)
    return hi, mid, lo


def _interleave_blocks(a, b):
    parts = []
    for p in range(a.shape[1] // LANES):
        parts += [a[:, p * LANES:(p + 1) * LANES], b[:, p * LANES:(p + 1) * LANES]]
    return jnp.concatenate(parts, axis=1)


def _inproj_kernel(x_ref, g_ref, w1_ref, gq_ref, wqn_ref, wqa_ref, wqb_ref, gkv_ref, wk_ref, wv_ref,
                   cos_ref, sin_ref, fq_ref, fk_ref, fv_ref, fl_ref, qm_ref, km_ref, vm_ref):
    a = _rms(x_ref[0], g_ref[...]).astype(BF16)
    c0 = 3 * FOX_WIDTH
    c1 = c0 + MLA_Q_LORA
    c2 = c1 + MLA_KV_LORA

    def proj(lo, hi):
        return _dot(a, w1_ref[:, lo:hi])

    fq_ref[0] = (proj(0, FOX_WIDTH) * FOX_QSCALE).astype(BF16)
    fk_ref[0] = proj(FOX_WIDTH, 2 * FOX_WIDTH).astype(BF16)
    fv_ref[0] = proj(2 * FOX_WIDTH, c0).astype(BF16)
    qn = _rms(proj(c0, c1), gq_ref[...]).astype(BF16)
    kvn = _rms(proj(c1, c2), gkv_ref[...]).astype(BF16)
    fl_ref[0] = proj(c2, c2 + LANES)
    cos = cos_ref[...]
    sin = sin_ref[...]
    pairs = MLA_HEADS // 2
    kpe = (proj(c2 + LANES, c2 + 2 * LANES) * cos + proj(c2 + 2 * LANES, c2 + 3 * LANES) * sin).astype(BF16)
    cos4 = jnp.concatenate([cos] * pairs, axis=1)
    sin4 = jnp.concatenate([sin] * pairs, axis=1)
    q_nope = (_dot(qn, wqn_ref[...]) * MLA_QSCALE).astype(BF16)
    q_rope = ((_dot(qn, wqa_ref[...]) * cos4 + _dot(qn, wqb_ref[...]) * sin4) * MLA_QSCALE).astype(BF16)
    qm_ref[0] = _interleave_blocks(q_nope, q_rope)
    k_nope = _dot(kvn, wk_ref[...]).astype(BF16)
    km_ref[0] = _interleave_blocks(k_nope, jnp.concatenate([kpe] * pairs, axis=1))
    vm_ref[0] = _dot(kvn, wv_ref[...]).astype(BF16)


def _inproj(x, g_mix, w1, g_q, wqn, wqa, wqb, g_kv, wk, wv, cos_t, sin_t):
    b, s, d = x.shape
    ts = min(TS_PROJ, s)

    def full(arr):
        return pl.BlockSpec(arr.shape, lambda bi, si: (0,) * arr.ndim)

    def tok(width):
        return pl.BlockSpec((1, ts, width), lambda bi, si: (bi, si, 0))

    tab = pl.BlockSpec((ts, LANES), lambda bi, si: (si, 0))
    pair_w = 2 * LANES * (MLA_HEADS // 2)
    out_shapes = (
        jax.ShapeDtypeStruct((b, s, FOX_WIDTH), BF16),
        jax.ShapeDtypeStruct((b, s, FOX_WIDTH), BF16),
        jax.ShapeDtypeStruct((b, s, FOX_WIDTH), BF16),
        jax.ShapeDtypeStruct((b, s, LANES), F32),
        jax.ShapeDtypeStruct((b, s, pair_w), BF16),
        jax.ShapeDtypeStruct((b, s, pair_w), BF16),
        jax.ShapeDtypeStruct((b, s, MLA_WIDTH), BF16),
    )
    return pl.pallas_call(
        _inproj_kernel,
        grid=(b, s // ts),
        in_specs=[tok(d), full(g_mix), full(w1), full(g_q), full(wqn), full(wqa), full(wqb), full(g_kv), full(wk),
                  full(wv), tab, tab],
        out_specs=(tok(FOX_WIDTH), tok(FOX_WIDTH), tok(FOX_WIDTH), tok(LANES), tok(pair_w), tok(pair_w),
                   tok(MLA_WIDTH)),
        out_shape=out_shapes,
        compiler_params=_cparams(2),
        name="inproj",
    )(x, g_mix, w1, g_q, wqn, wqa, wqb, g_kv, wk, wv, cos_t, sin_t)


def _decay_kernel(fl_ref, bf_ref, place_ref, ones_ref, cq_ref, ck_ref, *, blk):
    s = fl_ref.shape[1]
    row = lax.broadcasted_iota(I32, (blk, blk), 0)
    col = lax.broadcasted_iota(I32, (blk, blk), 1)
    tri = jnp.where(row >= col, 1.0, 0.0).astype(BF16)
    carry = jnp.zeros((1, LANES), F32)
    for i in range(s // blk):
        sl = slice(i * blk, (i + 1) * blk)
        z = fl_ref[0, sl, :] + bf_ref[...]
        lf = (jnp.minimum(z, 0.0) - jnp.log1p(jnp.exp(-jnp.abs(z)))) * LOG2E
        part = _dot(tri, jnp.concatenate(_split3(lf), axis=1))
        cs = part[:, :LANES] + part[:, LANES:2 * LANES] + part[:, 2 * LANES:] + carry
        carry = cs[blk - 1:blk, :]
        placed = _dot(jnp.concatenate(_split3(cs), axis=1), place_ref[...]) + ones_ref[...]
        cq_ref[0, sl, :] = placed[:, :LANES].astype(BF16)
        ck_ref[0, sl, :] = placed[:, LANES:].astype(BF16)


def _decay_tables():
    place = np.zeros((3 * LANES, 2 * LANES), np.float32)
    ones = np.zeros((1, 2 * LANES), np.float32)
    for h in range(FOX_HEADS):
        for part in range(3):
            place[part * LANES + h, FOX_DECAY_LANES * h + part] = 1.0
            place[part * LANES + h, LANES + FOX_DECAY_LANES * h + 3 + part] = -1.0
            ones[0, FOX_DECAY_LANES * h + 3 + part] = 1.0
            ones[0, LANES + FOX_DECAY_LANES * h + part] = 1.0
    return jnp.asarray(place, BF16), jnp.asarray(ones)


def _decay(flog, b_f):
    b, s, _ = flog.shape
    blk = min(DECAY_BLK, s)
    place, ones = _decay_tables()
    bf = jnp.zeros((1, LANES), F32).at[0, :FOX_HEADS].set(b_f)

    def full(arr):
        return pl.BlockSpec(arr.shape, lambda bi: (0,) * arr.ndim)

    seq = pl.BlockSpec((1, s, LANES), lambda bi: (bi, 0, 0))
    return pl.pallas_call(
        functools.partial(_decay_kernel, blk=blk),
        grid=(b,),
        in_specs=[seq, full(bf), full(place), full(ones)],
        out_specs=(seq, seq),
        out_shape=(jax.ShapeDtypeStruct((b, s, LANES), BF16), jax.ShapeDtypeStruct((b, s, LANES), BF16)),
        compiler_params=_cparams(1),
        name="decay",
    )(flog, bf, place, ones)


def _attn_pipe_kernel(*refs, fox, t, nq, pairs):
    if fox:
        q_ref, cq_ref, k_ref, ck_ref, v_ref, o_ref, s_even, s_odd = refs
        group = FOX_DECAY_LANES
    else:
        q_ref, k_ref, v_ref, o_ref, s_even, s_odd = refs
        group = MLA_ROPE_DIM
    g = pl.program_id(0)
    i = g % nq
    pair = (jnp.minimum(g, pl.num_programs(0) - 2) // nq) % pairs
    base = 2 * pair if fox else 0
    lane = lax.broadcasted_iota(I32, (1, LANES), 1)

    @pl.when(g == 0)
    def _():
        s_odd[...] = jnp.zeros_like(s_odd)

    def body(iv):
        s_new, s_old = (s_even, s_odd) if iv % 2 == 0 else (s_odd, s_even)
        half = t // 2
        row = lax.broadcasted_iota(I32, (half, half), 0)
        col = lax.broadcasted_iota(I32, (half, half), 1)
        allowed = (col <= row) if fox else ((col // CHUNK) <= (row // CHUNK))
        past = iv * t
        if fox:
            q_main, q_extra = q_ref[0], cq_ref[0]
        else:
            q_main, q_extra = q_ref[0, :, :LANES], q_ref[0, :, LANES:]
        zero = jnp.zeros_like(q_main)

        def keys(lo, hi):
            if fox:
                return jnp.concatenate([k_ref[0, lo:hi, :], ck_ref[0, lo:hi, :]], axis=1)
            return k_ref[0, lo:hi, :]

        qa = [jnp.concatenate([jnp.where((lane // (LANES // 2)) == hh, q_main, zero),
                               jnp.where((lane // group) == base + hh, q_extra, zero)], axis=1) for hh in range(2)]
        for hh in range(2):
            if iv > 0:
                s_new[hh, :, 0:past] = _dot_nt(qa[hh], keys(0, past))
            upper = _dot_nt(qa[hh][:half], keys(past, past + half))
            s_new[hh, :half, past:past + half] = jnp.where(allowed, upper, NEG)
            lower = _dot_nt(qa[hh][half:], keys(past, past + t))
            s_new[hh, half:, past:past + half] = lower[:, :half]
            s_new[hh, half:, past + half:past + t] = jnp.where(allowed, lower[:, half:], NEG)

        seen = ((iv - 1) % nq + 1) * t
        outs = []
        for hh in range(2):
            sc = s_old[hh, :, 0:seen - half]
            sc_tail = s_old[hh, half:, seen - half:seen]
            m_main = jnp.max(sc, axis=-1, keepdims=True)
            m_low = jnp.maximum(m_main[half:], jnp.max(sc_tail, axis=-1, keepdims=True))
            p = jnp.exp2((sc - jnp.concatenate([m_main[:half], m_low], axis=0)).astype(BF16))
            p_tail = jnp.exp2((sc_tail - m_low).astype(BF16))
            own = (lane // (LANES // 2)) == hh
            one = jnp.ones((1, LANES), BF16)
            acc = _dot(p, jnp.where(own, v_ref[0, 0:seen - half, :], one))
            acc_tail = _dot(p_tail, jnp.where(own, v_ref[0, seen - half:seen, :], one))
            acc = jnp.concatenate([acc[:half], acc[half:] + acc_tail], axis=0)
            other = (1 - hh) * (LANES // 2)
            outs.append(acc * (1.0 / acc[:, other:other + 1]))
        o_ref[0] = jnp.where(lane < LANES // 2, outs[0], outs[1]).astype(o_ref.dtype)

    for iv in range(nq):
        pl.when(i == iv)(functools.partial(body, iv))


def _attention_pipelined(fox, q, k, v, cq=None, ck=None):
    b, s, _ = v.shape
    t = min(T_ATT, s)
    nq = s // t
    assert nq % 2 == 0
    pairs = v.shape[2] // LANES
    qw = q.shape[2] // pairs
    kw = k.shape[2] // pairs
    items = b * pairs * nq

    def item(g):
        return g // (pairs * nq), (g // nq) % pairs, g % nq

    def cur(g):
        return item(jnp.minimum(g, items - 1))

    def prev(g):
        return item(jnp.maximum(g - 1, 0))

    qspec = pl.BlockSpec((1, t, qw), lambda g: (cur(g)[0], cur(g)[2], cur(g)[1]))
    kspec = pl.BlockSpec((1, s, kw), lambda g: (cur(g)[0], 0, cur(g)[1]))
    vspec = pl.BlockSpec((1, s, LANES), lambda g: (prev(g)[0], 0, prev(g)[1]))
    ospec = pl.BlockSpec((1, t, LANES), lambda g: (prev(g)[0], prev(g)[2], prev(g)[1]))
    if fox:
        cqspec = pl.BlockSpec((1, t, LANES), lambda g: (cur(g)[0], cur(g)[2], 0))
        ckspec = pl.BlockSpec((1, s, LANES), lambda g: (cur(g)[0], 0, 0))
        in_specs = [qspec, cqspec, kspec, ckspec, vspec]
        args = (q, cq, k, ck, v)
    else:
        in_specs = [qspec, kspec, vspec]
        args = (q, k, v)
    return pl.pallas_call(
        functools.partial(_attn_pipe_kernel, fox=fox, t=t, nq=nq, pairs=pairs),
        grid=(items + 1,),
        in_specs=in_specs,
        out_specs=ospec,
        out_shape=jax.ShapeDtypeStruct((b, s, pairs * LANES), BF16),
        scratch_shapes=[pltpu.VMEM((2, t, s), F32), pltpu.VMEM((2, t, s), F32)],
        compiler_params=_cparams(1),
        name="fox_attn" if fox else "mla_attn",
    )(*args)


def _memkv_kernel(mem_ref, g_ref, w_ref, k_ref, v_ref):
    a = _rms(mem_ref[0], g_ref[...]).astype(BF16)
    k_ref[0] = _dot(a, w_ref[:, 0:MEM_WIDTH]).astype(BF16)
    v_ref[0] = _dot(a, w_ref[:, MEM_WIDTH:2 * MEM_WIDTH]).astype(BF16)


def _memkv(mem, g, w):
    b, m, d = mem.shape
    kv = pl.BlockSpec((1, m, MEM_WIDTH), lambda bi: (bi, 0, 0))
    return pl.pallas_call(
        _memkv_kernel,
        grid=(b,),
        in_specs=[pl.BlockSpec((1, m, d), lambda bi: (bi, 0, 0)), pl.BlockSpec(g.shape, lambda bi: (0, 0)),
                  pl.BlockSpec(w.shape, lambda bi: (0, 0))],
        out_specs=(kv, kv),
        out_shape=(jax.ShapeDtypeStruct((b, m, MEM_WIDTH), BF16), jax.ShapeDtypeStruct((b, m, MEM_WIDTH), BF16)),
        compiler_params=_cparams(1),
        name="memkv",
    )(mem, g, w)


def _store_token_tiles(ref, val):
    rows = val.shape[0]
    for j in range(SUBLANES):
        ref[pl.ds(j, rows, stride=SUBLANES), :] = val[:, j * LANES:(j + 1) * LANES]


def _load_token_tiles(ref, rows):
    return jnp.concatenate([ref[pl.ds(j, rows, stride=SUBLANES), :] for j in range(SUBLANES)], axis=1)


def _memrouter_kernel(of_ref, om_ref, x_ref, gf_ref, gm_ref, wmix_ref, gq_ref, wq_ref, k_ref, v_ref, wo_ref,
                      gffn_ref, wrh_ref, wrl_ref, br_ref, tri_ref, h2_ref, at_ref, meta_ref, metat_ref, cnt_ref, carry_ref,
                      *, ts):
    first = jnp.logical_and(pl.program_id(0) == 0, pl.program_id(1) == 0)

    @pl.when(first)
    def _():
        carry_ref[...] = jnp.zeros_like(carry_ref)

    nf = _rms(of_ref[0].astype(F32), gf_ref[...]).astype(BF16)
    nm = _rms(om_ref[0].astype(F32), gm_ref[...]).astype(BF16)
    h1 = x_ref[0] + _dot(nf, wmix_ref[0:FOX_WIDTH, :]) + _dot(nm, wmix_ref[FOX_WIDTH:FOX_WIDTH + MLA_WIDTH, :])

    q = (_dot(_rms(h1, gq_ref[...]).astype(BF16), wq_ref[...]) * MEM_QSCALE).astype(BF16)
    heads = []
    for h in range(MEM_HEADS):
        sl = slice(h * MEM_HEAD_DIM, (h + 1) * MEM_HEAD_DIM)
        sc = _dot_nt(q[:, sl], k_ref[0, :, sl])
        m = jnp.max(sc, axis=-1, keepdims=True)
        pm = jnp.exp2(sc - m)
        l = jnp.sum(pm, axis=-1, keepdims=True)
        heads.append((_dot(pm.astype(BF16), v_ref[0, :, sl]) / l).astype(BF16))
    h2 = h1 + _dot(jnp.concatenate(heads, axis=1), wo_ref[...])
    h2_ref[0] = h2

    a = _rms(h2, gffn_ref[...])
    _store_token_tiles(at_ref, a)
    a_hi = a.astype(BF16)
    a_lo = (a - a_hi.astype(F32)).astype(BF16)
    logits = _dot(a_hi, wrh_ref[...]) + _dot(a_lo, wrh_ref[...]) + _dot(a_hi, wrl_ref[...]) + br_ref[...]
    lane = lax.broadcasted_iota(I32, (ts, LANES), 1)
    lane_f = lane.astype(F32)
    work = jnp.where(lane < N_EXPERTS, logits, NEG)
    vals, idxs, sels = [], [], []
    for _ in range(TOP_K):
        mk = jnp.max(work, axis=-1, keepdims=True)
        ik = jnp.min(jnp.where(work == mk, lane_f, float(LANES)), axis=-1, keepdims=True)
        sel = lane_f == ik
        work = jnp.where(sel, NEG, work)
        vals.append(mk)
        idxs.append(ik)
        sels.append(sel)
    exps = [jnp.exp(vk - vals[0]) for vk in vals]
    denom = exps[0] + exps[1] + exps[2] + exps[3]
    chosen = jnp.logical_or(jnp.logical_or(sels[0], sels[1]), jnp.logical_or(sels[2], sels[3]))
    onehot = jnp.where(chosen, 1.0, 0.0)
    carry = carry_ref[...]
    rank = _dot(tri_ref[...], onehot.astype(BF16)) + carry
    carry_new = carry + jnp.sum(onehot, axis=0, keepdims=True)
    carry_ref[...] = carry_new
    cnt_ref[...] = carry_new
    meta = jnp.zeros((ts, LANES), F32)
    for kk in range(TOP_K):
        rk = jnp.sum(jnp.where(sels[kk], rank, 0.0), axis=-1, keepdims=True)
        meta = jnp.where(lane == kk, idxs[kk], meta)
        meta = jnp.where(lane == TOP_K + kk, rk, meta)
        meta = jnp.where(lane == 2 * TOP_K + kk, exps[kk] / denom, meta)
    meta_ref[0] = meta
    metat_ref[...] = jnp.transpose(meta)[0:META_ROWS, :]


def _memrouter(o_fox, o_mla, x, g_fox, g_mla, w_mix, g_mq, w_mq, kmem, vmem, w_mo, g_ffn, wr_hi, wr_lo, b_r):
    b, s, d = x.shape
    m = kmem.shape[1]
    ts = min(TS_MEM, s)

    def full(arr):
        return pl.BlockSpec(arr.shape, lambda bi, si: (0,) * arr.ndim)

    def tok(width):
        return pl.BlockSpec((1, ts, width), lambda bi, si: (bi, si, 0))

    kv = pl.BlockSpec((1, m, MEM_WIDTH), lambda bi, si: (bi, 0, 0))
    cnt = pl.BlockSpec((1, LANES), lambda bi, si: (0, 0))
    tri = jnp.asarray(np.tril(np.ones((ts, ts), np.float32), k=-1), BF16)
    return pl.pallas_call(
        functools.partial(_memrouter_kernel, ts=ts),
        grid=(b, s // ts),
        in_specs=[tok(FOX_WIDTH), tok(MLA_WIDTH), tok(d), full(g_fox), full(g_mla), full(w_mix), full(g_mq),
                  full(w_mq), kv, kv, full(w_mo), full(g_ffn), full(wr_hi), full(wr_lo), full(b_r), full(tri)],
        out_specs=(tok(d), pl.BlockSpec((ts * SUBLANES, LANES), lambda bi, si: (bi * (s // ts) + si, 0)),
                   tok(LANES), pl.BlockSpec((META_ROWS, ts), lambda bi, si: (0, bi * (s // ts) + si)), cnt),
        out_shape=(jax.ShapeDtypeStruct((b, s, d), F32), jax.ShapeDtypeStruct((b * s * SUBLANES, LANES), F32),
                   jax.ShapeDtypeStruct((b, s, LANES), F32), jax.ShapeDtypeStruct((META_ROWS, b * s), F32),
                   jax.ShapeDtypeStruct((1, LANES), F32)),
        scratch_shapes=[pltpu.VMEM((1, LANES), F32)],
        compiler_params=_cparams(2),
        name="memrouter",
    )(o_fox, o_mla, x, g_fox, g_mla, w_mix, g_mq, w_mq, kmem, vmem, w_mo, g_ffn, wr_hi, wr_lo, b_r, tri)


def _tile_copy(src_ref, src_row8, dst_ref, dst_row8, sem):
    return pltpu.make_async_copy(src_ref.at[pl.ds(pl.multiple_of(src_row8, SUBLANES), SUBLANES), :],
                                 dst_ref.at[pl.ds(pl.multiple_of(dst_row8, SUBLANES), SUBLANES), :], sem)


def _dispatch_kernel(last_ref, grp_ref, nv_ref, pos_ref, at_ref, xs_ref, zbuf, zsem, sem, *, td, tm, min_used,
                     n_tiles):
    @pl.when(pl.program_id(0) == 0)
    def _():
        zbuf[...] = jnp.zeros_like(zbuf)
        grp = tm // MOE_ROW_GROUPS

        def fill(row):
            return pltpu.make_async_copy(
                zbuf, xs_ref.at[pl.ds(pl.multiple_of(row * SUBLANES, SUBLANES), grp * SUBLANES), :], zsem)

        groups = range(MOE_ROW_GROUPS)
        fills = [(gi >= grp_ref[e], last_ref[e] + gi * grp) for e in range(N_EXPERTS) for gi in groups]
        fills += [(t >= nv_ref[0], t * tm + gi * grp) for t in range(min_used, n_tiles) for gi in groups]
        for pred, row in fills:
            pl.when(pred)(lambda row=row: fill(row).start())
        for pred, row in fills:
            pl.when(pred)(lambda row=row: fill(row).wait())

    def issue(r, c):
        for kk in range(TOP_K):
            _tile_copy(at_ref, r * SUBLANES, xs_ref, pos_ref[kk * td + r], sem).start(priority=kk % 2)
        return c

    lax.fori_loop(0, td, issue, 0, unroll=8)
    for _ in range(TOP_K):
        pltpu.make_async_copy(at_ref, xs_ref.at[pl.ds(0, td * SUBLANES), :], sem).wait()


def _dispatch(last_tile_row, last_group, nv, pos8, at, n_tiles, tm):
    n = at.shape[0] // SUBLANES
    td = min(TD_DISPATCH, n)
    grid_spec = pltpu.PrefetchScalarGridSpec(
        num_scalar_prefetch=3,
        grid=(n // td,),
        in_specs=[pl.BlockSpec((TOP_K * td,), lambda i, *_: (i,), memory_space=pltpu.SMEM),
                  pl.BlockSpec((td * SUBLANES, LANES), lambda i, *_: (i, 0))],
        out_specs=pl.BlockSpec(memory_space=pl.ANY),
        scratch_shapes=[pltpu.VMEM((tm // MOE_ROW_GROUPS * SUBLANES, LANES), F32), pltpu.SemaphoreType.DMA,
                        pltpu.SemaphoreType.DMA],
    )
    min_used = n * TOP_K // tm
    return pl.pallas_call(
        functools.partial(_dispatch_kernel, td=td, tm=tm, min_used=min_used, n_tiles=n_tiles),
        grid_spec=grid_spec,
        out_shape=jax.ShapeDtypeStruct((n_tiles * tm * SUBLANES, LANES), F32),
        compiler_params=_cparams(1),
        name="dispatch",
    )(last_tile_row, last_group, nv, pos8, at)


def _moe_kernel(te_ref, rows_ref, slot_ref, nxt_ref, xs_ref, wgu_hbm, bgu_ref, wd_hbm, bd_ref, y_ref,
                wgu_f32, wd_f32, wgu_bf, wd_bf, sem, *, ch, tm):
    i = pl.program_id(0)
    d_exp = wd_hbm.shape[1]
    rows_used = rows_ref[i]
    valid = rows_used > 0
    new_expert = jnp.logical_or(i == 0, te_ref[i] != te_ref[jnp.maximum(i - 1, 0)])

    def weight_copies(e, sl):
        return (pltpu.make_async_copy(wgu_hbm.at[e], wgu_f32.at[sl], sem.at[sl, 0]),
                pltpu.make_async_copy(wd_hbm.at[e], wd_f32.at[sl], sem.at[sl, 1]))

    @pl.when(i == 0)
    def _():
        for c in weight_copies(te_ref[0], slot_ref[0]):
            c.start()

    @pl.when(jnp.logical_and(new_expert, valid))
    def _():
        sl = slot_ref[i]
        for c in weight_copies(te_ref[i], sl):
            c.wait()
        wgu_bf[...] = wgu_f32[sl].astype(BF16)
        wd_bf[...] = wd_f32[sl].astype(BF16)

        @pl.when(nxt_ref[i] >= 0)
        def _():
            for c in weight_copies(nxt_ref[i], 1 - sl):
                c.start()

    def ffn(rows):
        x = _load_token_tiles(xs_ref, rows).astype(BF16)
        acc = jnp.zeros((rows, wd_hbm.shape[2]), F32)
        for c in range(d_exp // ch):
            def gu(lo):
                return _dot(x, wgu_bf[:, lo:lo + ch]) + bgu_ref[0, :, lo:lo + ch]

            gate = jnp.minimum(gu(c * ch), SWIGLU_LIMIT)
            up = jnp.clip(gu(d_exp + c * ch), -SWIGLU_LIMIT, SWIGLU_LIMIT)
            act = gate * (1.0 / (1.0 + jnp.exp(-SWIGLU_ALPHA * gate))) * (up + 1.0)
            acc = acc + _dot(act.astype(BF16), wd_bf[c * ch:(c + 1) * ch, :])
        return acc + bd_ref[0]

    grp = tm // MOE_ROW_GROUPS
    for used in range(1, MOE_ROW_GROUPS + 1):
        rows = used * grp

        def partial_tile(rows=rows):
            _store_token_tiles(y_ref, ffn(rows))
            if rows < tm:
                y_ref[rows * SUBLANES:, :] = jnp.zeros(((tm - rows) * SUBLANES, LANES), F32)

        pl.when(jnp.logical_and(rows_used > rows - grp, rows_used <= rows))(partial_tile)

    @pl.when(jnp.logical_not(valid))
    def _():
        y_ref[...] = jnp.zeros_like(y_ref)


def _moe(te, rows, slot, nxt, xs, wgu, bgu, wd, bd, tm):
    n_tiles = xs.shape[0] // (tm * SUBLANES)
    _, d, d_exp2 = wgu.shape
    d_exp = wd.shape[1]
    assert d == SUBLANES * LANES and wd.shape[2] == d
    tile = pl.BlockSpec((tm * SUBLANES, LANES), lambda i, te, *_: (i, 0))
    grid_spec = pltpu.PrefetchScalarGridSpec(
        num_scalar_prefetch=4,
        grid=(n_tiles,),
        in_specs=[
            tile,
            pl.BlockSpec(memory_space=pl.ANY),
            pl.BlockSpec((1, 1, d_exp2), lambda i, te, *_: (te[i], 0, 0)),
            pl.BlockSpec(memory_space=pl.ANY),
            pl.BlockSpec((1, 1, d), lambda i, te, *_: (te[i], 0, 0)),
        ],
        out_specs=tile,
        scratch_shapes=[pltpu.VMEM((2, d, d_exp2), F32), pltpu.VMEM((2, d_exp, d), F32),
                        pltpu.VMEM((d, d_exp2), BF16), pltpu.VMEM((d_exp, d), BF16),
                        pltpu.SemaphoreType.DMA((2, 2))],
    )
    return pl.pallas_call(
        functools.partial(_moe_kernel, ch=min(MOE_CH, d_exp), tm=tm),
        grid_spec=grid_spec,
        out_shape=jax.ShapeDtypeStruct(xs.shape, F32),
        compiler_params=_cparams(1),
        name="moe",
    )(te, rows, slot, nxt, xs, wgu, bgu, wd, bd)


def _combine_kernel(pos_ref, posn_ref, h2_ref, meta_ref, y_ref, gf_ref, o_ref, ybuf, sem, *, td, final_norm):
    i = pl.program_id(0)
    slot = i % 2

    def issue_block(p_ref, sl):
        def issue(r, c):
            for kk in range(TOP_K):
                _tile_copy(y_ref, p_ref[kk * td + r], ybuf.at[sl, kk], r * SUBLANES,
                           sem.at[sl]).start(priority=kk % 2)
            return c

        lax.fori_loop(0, td, issue, 0, unroll=8)

    @pl.when(i == 0)
    def _():
        issue_block(pos_ref, 0)

    @pl.when(i + 1 < pl.num_programs(0))
    def _():
        issue_block(posn_ref, 1 - slot)

    for kk in range(TOP_K):
        pltpu.make_async_copy(y_ref.at[pl.ds(0, td * SUBLANES), :], ybuf.at[slot, kk], sem.at[slot]).wait()

    meta = meta_ref[...]
    acc = h2_ref[...]
    for kk in range(TOP_K):
        gate = meta[:, 2 * TOP_K + kk:2 * TOP_K + kk + 1]
        acc = acc + gate * _load_token_tiles(ybuf.at[slot, kk], td)
    o_ref[...] = _rms(acc, gf_ref[...]) if final_norm else acc


def _combine(pos8, h2, meta, y, g_final, final_norm):
    n, d = h2.shape
    td = min(TD_COMBINE, n)
    n_blocks = n // td
    grid_spec = pltpu.PrefetchScalarGridSpec(
        num_scalar_prefetch=0,
        grid=(n_blocks,),
        in_specs=[pl.BlockSpec((TOP_K * td,), lambda i: (i,), memory_space=pltpu.SMEM),
                  pl.BlockSpec((TOP_K * td,), lambda i: (jnp.minimum(i + 1, n_blocks - 1),), memory_space=pltpu.SMEM),
                  pl.BlockSpec((td, d), lambda i: (i, 0)),
                  pl.BlockSpec((td, LANES), lambda i: (i, 0)),
                  pl.BlockSpec(memory_space=pl.ANY),
                  pl.BlockSpec(g_final.shape, lambda i: (0, 0))],
        out_specs=pl.BlockSpec((td, d), lambda i: (i, 0)),
        scratch_shapes=[pltpu.VMEM((2, TOP_K, td * SUBLANES, LANES), F32), pltpu.SemaphoreType.DMA((2,))],
    )
    return pl.pallas_call(
        functools.partial(_combine_kernel, td=td, final_norm=final_norm),
        grid_spec=grid_spec,
        out_shape=jax.ShapeDtypeStruct((n, d), F32),
        compiler_params=_cparams(1),
        name="combine",
    )(pos8, pos8, h2, meta, y, g_final)


def _rope_tables(seq_len):
    inv = 1.0 / (ROPE_BASE ** (jnp.arange(0, MLA_ROPE_DIM, 2, dtype=F32) / MLA_ROPE_DIM))
    ang = jnp.arange(seq_len, dtype=F32)[:, None] * inv[None, :]
    cos, sin = jnp.cos(ang), jnp.sin(ang)
    pad = jnp.zeros((seq_len, LANES - 2 * MLA_ROPE_DIM), F32)
    cos_t = jnp.concatenate([cos, cos, cos, cos, pad], axis=1)
    sin_t = jnp.concatenate([sin, sin, sin, sin, pad], axis=1)
    return cos_t, sin_t


def _rot_cols(w):
    half = MLA_ROPE_DIM // 2
    return jnp.concatenate([-w[:, half:], w[:, :half]], axis=1)


def _rope_pair_block(r0, r1):
    pad = jnp.zeros((r0.shape[0], LANES - 2 * MLA_ROPE_DIM), F32)
    return jnp.concatenate([r0, r1, pad], axis=1)


def _prep_inproj_weights(w_in, w_uq, w_ukv):
    d = w_in.shape[0]
    pts = np.cumsum((FOX_WIDTH, FOX_WIDTH, FOX_WIDTH, FOX_HEADS, MLA_Q_LORA, MLA_KV_LORA, MLA_ROPE_DIM))
    w_fl, w_ql = w_in[:, pts[2]:pts[3]], w_in[:, pts[3]:pts[4]]
    w_kvl, w_kr = w_in[:, pts[4]:pts[5]], w_in[:, pts[5]:pts[6]]
    misc = jnp.concatenate([w_fl, jnp.zeros((d, LANES - FOX_HEADS), F32)], axis=1)
    w_tail = jnp.concatenate([w_ql, w_kvl, misc, _rope_pair_block(w_kr, w_kr),
                              _rope_pair_block(_rot_cols(w_kr), _rot_cols(w_kr))], axis=1).astype(BF16)
    qd = MLA_NOPE_DIM + MLA_ROPE_DIM
    kvd = MLA_NOPE_DIM + MLA_V_DIM
    q_nope = [w_uq[:, h * qd:h * qd + MLA_NOPE_DIM] for h in range(MLA_HEADS)]
    q_rope = [w_uq[:, h * qd + MLA_NOPE_DIM:(h + 1) * qd] for h in range(MLA_HEADS)]
    wqa = [_rope_pair_block(q_rope[h], q_rope[h + 1]) for h in range(0, MLA_HEADS, 2)]
    wqb = [_rope_pair_block(_rot_cols(q_rope[h]), _rot_cols(q_rope[h + 1])) for h in range(0, MLA_HEADS, 2)]
    wk = [w_ukv[:, h * kvd:h * kvd + MLA_NOPE_DIM] for h in range(MLA_HEADS)]
    wv = [w_ukv[:, h * kvd + MLA_NOPE_DIM:(h + 1) * kvd] for h in range(MLA_HEADS)]
    cat = lambda xs: jnp.concatenate(xs, axis=1).astype(BF16)
    w1 = jnp.concatenate([w_in[:, :pts[2]].astype(BF16), w_tail], axis=1)
    return w1, cat(q_nope), cat(wqa), cat(wqb), cat(wk), cat(wv)


def _routing_tables(counts, n_tiles, tm):
    tiles_e = (counts + tm - 1) // tm
    tile_end = jnp.cumsum(tiles_e)
    off = ((tile_end - tiles_e) * tm).astype(I32)
    total = tile_end[-1]
    ti = jnp.arange(n_tiles, dtype=I32)
    te = jnp.minimum(jnp.sum(ti[:, None] >= tile_end[None, :], axis=1), N_EXPERTS - 1).astype(I32)
    valid = ti < total
    last_e = jnp.sum(jnp.where(ti == total - 1, te, 0))
    te = jnp.where(valid, te, last_e).astype(I32)
    last_tile_row = (off + (tiles_e - 1) * tm).astype(I32)
    grp = tm // MOE_ROW_GROUPS
    last_group = jnp.where(counts > 0, ((counts - 1) % tm) // grp, MOE_ROW_GROUPS).astype(I32)
    used = tiles_e > 0
    ids = jnp.arange(N_EXPERTS, dtype=I32)
    slot_e = (jnp.cumsum(used.astype(I32)) - 1) % 2
    later = jnp.where(used[None, :] & (ids[None, :] > ids[:, None]), ids[None, :], N_EXPERTS)
    nxt_e = jnp.min(later, axis=1)
    nxt_e = jnp.where(nxt_e == N_EXPERTS, -1, nxt_e).astype(I32)
    pick = te[:, None] == ids[None, :]
    slot = jnp.sum(jnp.where(pick, slot_e[None, :], 0), axis=1).astype(I32)
    nxt = jnp.sum(jnp.where(pick, nxt_e[None, :], 0), axis=1).astype(I32)
    end_row = jnp.sum(jnp.where(pick, (off + counts)[None, :], 0), axis=1)
    rows = jnp.where(valid, jnp.clip(end_row - ti * tm, 0, tm), 0).astype(I32)
    return off, last_tile_row, last_group, te, total.astype(I32).reshape(1), slot, nxt, rows


def kernel(x, mem, g_mix, w_in, b_f, g_q_lat, w_uq, g_kv_lat, w_ukv, g_fox_out, g_mla_out, w_o, g_mem_q, w_mem_q,
           g_mem_kv, w_mem_kv, w_mem_o, g_ffn, w_router, b_router, w_gate_up, b_gate_up, w_down, b_down, g_final):
    b, s, d = x.shape
    n = b * s
    depth = g_mix.shape[0]
    cos_t, sin_t = _rope_tables(s)
    row = lambda v: v.reshape(1, -1)
    h = x
    for l in range(depth):
        w1, wqn, wqa, wqb, wk, wv = _prep_inproj_weights(w_in[l], w_uq[l], w_ukv[l])
        fq, fk, fv, flog, qm, km, vm = _inproj(h, row(g_mix[l]), w1, row(g_q_lat[l]), wqn, wqa, wqb,
                                                row(g_kv_lat[l]), wk, wv, cos_t, sin_t)
        cq, ck = _decay(flog, b_f[l])
        o_fox = _attention_pipelined(True, fq, fk, fv, cq, ck)
        o_mla = _attention_pipelined(False, qm, km, vm)
        kmem, vmem = _memkv(mem, row(g_mem_kv[l]), w_mem_kv[l].astype(BF16))
        wr = jnp.zeros((d, LANES), F32).at[:, :N_EXPERTS].set(w_router[l])
        wr_hi = wr.astype(BF16)
        wr_lo = (wr - wr_hi.astype(F32)).astype(BF16)
        br = jnp.zeros((1, LANES), F32).at[0, :N_EXPERTS].set(b_router[l])
        h2, at, meta, meta_t, cnt = _memrouter(
            o_fox, o_mla, h, row(g_fox_out[l]), row(g_mla_out[l]), w_o[l].astype(BF16), row(g_mem_q[l]),
            w_mem_q[l].astype(BF16), kmem, vmem, w_mem_o[l].astype(BF16), row(g_ffn[l]), wr_hi, wr_lo, br)
        ek = meta_t[0:TOP_K].astype(I32)
        rk = meta_t[TOP_K:2 * TOP_K].astype(I32)
        n_tiles = n * TOP_K // TM_MOE + N_EXPERTS
        counts = cnt[0, :N_EXPERTS].astype(I32)
        off, last_tile_row, last_group, te, nv, slot, nxt, rows = _routing_tables(counts, n_tiles, TM_MOE)
        ids = jnp.arange(N_EXPERTS, dtype=I32)[:, None, None]
        off_of = jnp.sum(jnp.where(ek[None] == ids, off[:, None, None], 0), axis=0)
        pos8 = ((off_of + rk) * SUBLANES).astype(I32)

        def by_block(td):
            return pos8.reshape(TOP_K, n // td, td).transpose(1, 0, 2).reshape(-1)

        xs = _dispatch(last_tile_row, last_group, nv, by_block(min(TD_DISPATCH, n)), at, n_tiles, TM_MOE)
        y = _moe(te, rows, slot, nxt, xs, w_gate_up[l], b_gate_up[l].reshape(N_EXPERTS, 1, -1), w_down[l],
                 b_down[l].reshape(N_EXPERTS, 1, -1), TM_MOE)
        h = _combine(by_block(min(TD_COMBINE, n)), h2.reshape(n, d), meta.reshape(n, LANES), y, row(g_final),
                     l == depth - 1).reshape(b, s, d)
    return h
```

```python
import functools

import numpy as np
import jax
import jax.numpy as jnp
from jax import lax
from jax.experimental import pallas as pl
from jax.experimental.pallas import tpu as pltpu

F32 = jnp.float32
BF16 = jnp.bfloat16
I32 = jnp.int32

LANES = 128
SUBLANES = 8
RMS_EPS = 1e-6
CHUNK = 64
FOX_HEADS = 8
FOX_HEAD_DIM = 64
FOX_WIDTH = FOX_HEADS * FOX_HEAD_DIM
MLA_HEADS = 8
MLA_Q_LORA = 384
MLA_KV_LORA = 256
MLA_NOPE_DIM = 64
MLA_ROPE_DIM = 32
MLA_V_DIM = 64
MLA_WIDTH = MLA_HEADS * MLA_V_DIM
ROPE_BASE = 10000.0
MEM_HEADS = 4
MEM_HEAD_DIM = 128
MEM_WIDTH = MEM_HEADS * MEM_HEAD_DIM
N_EXPERTS = 32
TOP_K = 4
SWIGLU_LIMIT = 7.0
SWIGLU_ALPHA = 1.702

LOG2E = 1.4426950408889634
NEG = -1e30
FOX_QSCALE = FOX_HEAD_DIM ** -0.5 * LOG2E
MLA_QSCALE = (MLA_NOPE_DIM + MLA_ROPE_DIM) ** -0.5 * LOG2E
MEM_QSCALE = MEM_HEAD_DIM ** -0.5 * LOG2E
FOX_DECAY_LANES = 16
META_ROWS = 16

VMEM_LIMIT = 56 * 1024 * 1024

TS_PROJ = 512
TS_MEM = 512
T_ATT = 512
DECAY_BLK = 256
TM_MOE = 512
MOE_ROW_GROUPS = 4
TD_DISPATCH = 1024
TD_COMBINE = 512
MOE_CH = 512


def _cparams(n_axes):
    return pltpu.CompilerParams(dimension_semantics=("arbitrary",) * n_axes, vmem_limit_bytes=VMEM_LIMIT)


def _dot(a, b):
    return jnp.dot(a, b, preferred_element_type=F32)


def _dot_nt(a, b):
    return lax.dot_general(a, b, (((1,), (1,)), ((), ())), preferred_element_type=F32)


def _rms(x, g):
    return x * lax.rsqrt(jnp.mean(x * x, axis=-1, keepdims=True) + RMS_EPS) * g


def _split3(x):
    hi = x.astype(BF16)
    r1 = x - hi.astype(F32)
    mid = r1.astype(BF16)
    lo = (r1 - mid.astype(F32)).astype(BF16)
    return hi, mid, lo


def _interleave_blocks(a, b):
    parts = []
    for p in range(a.shape[1] // LANES):
        parts += [a[:, p * LANES:(p + 1) * LANES], b[:, p * LANES:(p + 1) * LANES]]
    return jnp.concatenate(parts, axis=1)


def _inproj_kernel(x_ref, g_ref, w1_ref, gq_ref, wqn_ref, wqa_ref, wqb_ref, gkv_ref, wk_ref, wv_ref,
                   cos_ref, sin_ref, fq_ref, fk_ref, fv_ref, fl_ref, qm_ref, km_ref, vm_ref):
    a = _rms(x_ref[0], g_ref[...]).astype(BF16)
    c0 = 3 * FOX_WIDTH
    c1 = c0 + MLA_Q_LORA
    c2 = c1 + MLA_KV_LORA

    def proj(lo, hi):
        return _dot(a, w1_ref[:, lo:hi])

    fq_ref[0] = (proj(0, FOX_WIDTH) * FOX_QSCALE).astype(BF16)
    fk_ref[0] = proj(FOX_WIDTH, 2 * FOX_WIDTH).astype(BF16)
    fv_ref[0] = proj(2 * FOX_WIDTH, c0).astype(BF16)
    qn = _rms(proj(c0, c1), gq_ref[...]).astype(BF16)
    kvn = _rms(proj(c1, c2), gkv_ref[...]).astype(BF16)
    fl_ref[0] = proj(c2, c2 + LANES)
    cos = cos_ref[...]
    sin = sin_ref[...]
    pairs = MLA_HEADS // 2
    kpe = (proj(c2 + LANES, c2 + 2 * LANES) * cos + proj(c2 + 2 * LANES, c2 + 3 * LANES) * sin).astype(BF16)
    cos4 = jnp.concatenate([cos] * pairs, axis=1)
    sin4 = jnp.concatenate([sin] * pairs, axis=1)
    q_nope = (_dot(qn, wqn_ref[...]) * MLA_QSCALE).astype(BF16)
    q_rope = ((_dot(qn, wqa_ref[...]) * cos4 + _dot(qn, wqb_ref[...]) * sin4) * MLA_QSCALE).astype(BF16)
    qm_ref[0] = _interleave_blocks(q_nope, q_rope)
    k_nope = _dot(kvn, wk_ref[...]).astype(BF16)
    km_ref[0] = _interleave_blocks(k_nope, jnp.concatenate([kpe] * pairs, axis=1))
    vm_ref[0] = _dot(kvn, wv_ref[...]).astype(BF16)


def _inproj(x, g_mix, w1, g_q, wqn, wqa, wqb, g_kv, wk, wv, cos_t, sin_t):
    b, s, d = x.shape
    ts = min(TS_PROJ, s)

    def full(arr):
        return pl.BlockSpec(arr.shape, lambda bi, si: (0,) * arr.ndim)

    def tok(width):
        return pl.BlockSpec((1, ts, width), lambda bi, si: (bi, si, 0))

    tab = pl.BlockSpec((ts, LANES), lambda bi, si: (si, 0))
    pair_w = 2 * LANES * (MLA_HEADS // 2)
    out_shapes = (
        jax.ShapeDtypeStruct((b, s, FOX_WIDTH), BF16),
        jax.ShapeDtypeStruct((b, s, FOX_WIDTH), BF16),
        jax.ShapeDtypeStruct((b, s, FOX_WIDTH), BF16),
        jax.ShapeDtypeStruct((b, s, LANES), F32),
        jax.ShapeDtypeStruct((b, s, pair_w), BF16),
        jax.ShapeDtypeStruct((b, s, pair_w), BF16),
        jax.ShapeDtypeStruct((b, s, MLA_WIDTH), BF16),
    )
    return pl.pallas_call(
        _inproj_kernel,
        grid=(b, s // ts),
        in_specs=[tok(d), full(g_mix), full(w1), full(g_q), full(wqn), full(wqa), full(wqb), full(g_kv), full(wk),
                  full(wv), tab, tab],
        out_specs=(tok(FOX_WIDTH), tok(FOX_WIDTH), tok(FOX_WIDTH), tok(LANES), tok(pair_w), tok(pair_w),
                   tok(MLA_WIDTH)),
        out_shape=out_shapes,
        compiler_params=_cparams(2),
        name="inproj",
    )(x, g_mix, w1, g_q, wqn, wqa, wqb, g_kv, wk, wv, cos_t, sin_t)


def _decay_kernel(fl_ref, bf_ref, place_ref, ones_ref, cq_ref, ck_ref, *, blk):
    s = fl_ref.shape[1]
    row = lax.broadcasted_iota(I32, (blk, blk), 0)
    col = lax.broadcasted_iota(I32, (blk, blk), 1)
    tri = jnp.where(row >= col, 1.0, 0.0).astype(BF16)
    carry = jnp.zeros((1, LANES), F32)
    for i in range(s // blk):
        sl = slice(i * blk, (i + 1) * blk)
        z = fl_ref[0, sl, :] + bf_ref[...]
        lf = (jnp.minimum(z, 0.0) - jnp.log1p(jnp.exp(-jnp.abs(z)))) * LOG2E
        part = _dot(tri, jnp.concatenate(_split3(lf), axis=1))
        cs = part[:, :LANES] + part[:, LANES:2 * LANES] + part[:, 2 * LANES:] + carry
        carry = cs[blk - 1:blk, :]
        placed = _dot(jnp.concatenate(_split3(cs), axis=1), place_ref[...]) + ones_ref[...]
        cq_ref[0, sl, :] = placed[:, :LANES].astype(BF16)
        ck_ref[0, sl, :] = placed[:, LANES:].astype(BF16)


def _decay_tables():
    place = np.zeros((3 * LANES, 2 * LANES), np.float32)
    ones = np.zeros((1, 2 * LANES), np.float32)
    for h in range(FOX_HEADS):
        for part in range(3):
            place[part * LANES + h, FOX_DECAY_LANES * h + part] = 1.0
            place[part * LANES + h, LANES + FOX_DECAY_LANES * h + 3 + part] = -1.0
            ones[0, FOX_DECAY_LANES * h + 3 + part] = 1.0
            ones[0, LANES + FOX_DECAY_LANES * h + part] = 1.0
    return jnp.asarray(place, BF16), jnp.asarray(ones)


def _decay(flog, b_f):
    b, s, _ = flog.shape
    blk = min(DECAY_BLK, s)
    place, ones = _decay_tables()
    bf = jnp.zeros((1, LANES), F32).at[0, :FOX_HEADS].set(b_f)

    def full(arr):
        return pl.BlockSpec(arr.shape, lambda bi: (0,) * arr.ndim)

    seq = pl.BlockSpec((1, s, LANES), lambda bi: (bi, 0, 0))
    return pl.pallas_call(
        functools.partial(_decay_kernel, blk=blk),
        grid=(b,),
        in_specs=[seq, full(bf), full(place), full(ones)],
        out_specs=(seq, seq),
        out_shape=(jax.ShapeDtypeStruct((b, s, LANES), BF16), jax.ShapeDtypeStruct((b, s, LANES), BF16)),
        compiler_params=_cparams(1),
        name="decay",
    )(flog, bf, place, ones)


def _attn_pipe_kernel(*refs, fox, t, nq, pairs):
    if fox:
        q_ref, cq_ref, k_ref, ck_ref, v_ref, o_ref, s_even, s_odd = refs
        group = FOX_DECAY_LANES
    else:
        q_ref, k_ref, v_ref, o_ref, s_even, s_odd = refs
        group = MLA_ROPE_DIM
    g = pl.program_id(0)
    i = g % nq
    pair = (jnp.minimum(g, pl.num_programs(0) - 2) // nq) % pairs
    base = 2 * pair if fox else 0
    lane = lax.broadcasted_iota(I32, (1, LANES), 1)

    @pl.when(g == 0)
    def _():
        s_odd[...] = jnp.zeros_like(s_odd)

    def body(iv, cur_q, prev_q):
        s_new, s_old = (s_even, s_odd) if iv % 2 == 0 else (s_odd, s_even)
        half = t // 2
        row = lax.broadcasted_iota(I32, (half, half), 0)
        col = lax.broadcasted_iota(I32, (half, half), 1)
        allowed = (col <= row) if fox else ((col // CHUNK) <= (row // CHUNK))
        past = cur_q * t
        if fox:
            q_main, q_extra = q_ref[0], cq_ref[0]
        else:
            q_main, q_extra = q_ref[0, :, :LANES], q_ref[0, :, LANES:]
        zero = jnp.zeros_like(q_main)

        def keys(lo, hi):
            if fox:
                return jnp.concatenate([k_ref[0, lo:hi, :], ck_ref[0, lo:hi, :]], axis=1)
            return k_ref[0, lo:hi, :]

        qa = [jnp.concatenate([jnp.where((lane // (LANES // 2)) == hh, q_main, zero),
                               jnp.where((lane // group) == base + hh, q_extra, zero)], axis=1) for hh in range(2)]
        for hh in range(2):
            if past > 0:
                s_new[hh, :, 0:past] = _dot_nt(qa[hh], keys(0, past))
            upper = _dot_nt(qa[hh][:half], keys(past, past + half))
            s_new[hh, :half, past:past + half] = jnp.where(allowed, upper, NEG)
            lower = _dot_nt(qa[hh][half:], keys(past, past + t))
            s_new[hh, half:, past:past + half] = lower[:, :half]
            s_new[hh, half:, past + half:past + t] = jnp.where(allowed, lower[:, half:], NEG)

        seen = (prev_q + 1) * t
        outs = []
        for hh in range(2):
            sc = s_old[hh, :, 0:seen - half]
            sc_tail = s_old[hh, half:, seen - half:seen]
            m_main = jnp.max(sc, axis=-1, keepdims=True)
            m_low = jnp.maximum(m_main[half:], jnp.max(sc_tail, axis=-1, keepdims=True))
            p = jnp.exp2((sc - jnp.concatenate([m_main[:half], m_low], axis=0)).astype(BF16))
            p_tail = jnp.exp2((sc_tail - m_low).astype(BF16))
            own = (lane // (LANES // 2)) == hh
            one = jnp.ones((1, LANES), BF16)
            acc = _dot(p, jnp.where(own, v_ref[0, 0:seen - half, :], one))
            acc_tail = _dot(p_tail, jnp.where(own, v_ref[0, seen - half:seen, :], one))
            acc = jnp.concatenate([acc[:half], acc[half:] + acc_tail], axis=0)
            other = (1 - hh) * (LANES // 2)
            outs.append(acc * (1.0 / acc[:, other:other + 1]))
        o_ref[0] = jnp.where(lane < LANES // 2, outs[0], outs[1]).astype(o_ref.dtype)

    downwards = ((g // nq) % pairs) % 2
    for down in (0, 1):
        for iv in range(nq):
            cur_q = nq - 1 - iv if down else iv
            if iv == 0:
                prev_q = nq - 1 if down else 0
            else:
                prev_q = cur_q + 1 if down else cur_q - 1
            pl.when(jnp.logical_and(i == iv, downwards == down))(functools.partial(body, iv, cur_q, prev_q))


def _attention_pipelined(fox, q, k, v, cq=None, ck=None):
    b, s, _ = v.shape
    t = min(T_ATT, s)
    nq = s // t
    assert nq % 2 == 0
    pairs = v.shape[2] // LANES
    qw = q.shape[2] // pairs
    kw = k.shape[2] // pairs
    items = b * pairs * nq

    assert pairs % 2 == 0

    def item(g):
        pair, i = (g // nq) % pairs, g % nq
        return g // (pairs * nq), pair, jnp.where(pair % 2 == 1, nq - 1 - i, i)

    def cur(g):
        return item(jnp.minimum(g, items - 1))

    def prev(g):
        return item(jnp.maximum(g - 1, 0))

    qspec = pl.BlockSpec((1, t, qw), lambda g: (cur(g)[0], cur(g)[2], cur(g)[1]))
    kspec = pl.BlockSpec((1, s, kw), lambda g: (cur(g)[0], 0, cur(g)[1]))
    vspec = pl.BlockSpec((1, s, LANES), lambda g: (prev(g)[0], 0, prev(g)[1]))
    ospec = pl.BlockSpec((1, t, LANES), lambda g: (prev(g)[0], prev(g)[2], prev(g)[1]))
    if fox:
        cqspec = pl.BlockSpec((1, t, LANES), lambda g: (cur(g)[0], cur(g)[2], 0))
        ckspec = pl.BlockSpec((1, s, LANES), lambda g: (cur(g)[0], 0, 0))
        in_specs = [qspec, cqspec, kspec, ckspec, vspec]
        args = (q, cq, k, ck, v)
    else:
        in_specs = [qspec, kspec, vspec]
        args = (q, k, v)
    return pl.pallas_call(
        functools.partial(_attn_pipe_kernel, fox=fox, t=t, nq=nq, pairs=pairs),
        grid=(items + 1,),
        in_specs=in_specs,
        out_specs=ospec,
        out_shape=jax.ShapeDtypeStruct((b, s, pairs * LANES), BF16),
        scratch_shapes=[pltpu.VMEM((2, t, s), F32), pltpu.VMEM((2, t, s), F32)],
        compiler_params=_cparams(1),
        name="fox_attn" if fox else "mla_attn",
    )(*args)


def _memkv_kernel(mem_ref, g_ref, w_ref, k_ref, v_ref):
    a = _rms(mem_ref[0], g_ref[...]).astype(BF16)
    k_ref[0] = _dot(a, w_ref[:, 0:MEM_WIDTH]).astype(BF16)
    v_ref[0] = _dot(a, w_ref[:, MEM_WIDTH:2 * MEM_WIDTH]).astype(BF16)


def _memkv(mem, g, w):
    b, m, d = mem.shape
    kv = pl.BlockSpec((1, m, MEM_WIDTH), lambda bi: (bi, 0, 0))
    return pl.pallas_call(
        _memkv_kernel,
        grid=(b,),
        in_specs=[pl.BlockSpec((1, m, d), lambda bi: (bi, 0, 0)), pl.BlockSpec(g.shape, lambda bi: (0, 0)),
                  pl.BlockSpec(w.shape, lambda bi: (0, 0))],
        out_specs=(kv, kv),
        out_shape=(jax.ShapeDtypeStruct((b, m, MEM_WIDTH), BF16), jax.ShapeDtypeStruct((b, m, MEM_WIDTH), BF16)),
        compiler_params=_cparams(1),
        name="memkv",
    )(mem, g, w)


def _store_token_tiles(ref, val):
    rows = val.shape[0]
    for j in range(SUBLANES):
        ref[pl.ds(j, rows, stride=SUBLANES), :] = val[:, j * LANES:(j + 1) * LANES]


def _load_token_tiles(ref, rows):
    return jnp.concatenate([ref[pl.ds(j, rows, stride=SUBLANES), :] for j in range(SUBLANES)], axis=1)


def _memrouter_kernel(of_ref, om_ref, x_ref, gf_ref, gm_ref, wmix_ref, gq_ref, wq_ref, k_ref, v_ref, wo_ref,
                      gffn_ref, wrh_ref, wrl_ref, br_ref, tri_ref, h2_ref, at_ref, meta_ref, metat_ref, cnt_ref, carry_ref,
                      *, ts):
    first = jnp.logical_and(pl.program_id(0) == 0, pl.program_id(1) == 0)

    @pl.when(first)
    def _():
        carry_ref[...] = jnp.zeros_like(carry_ref)

    nf = _rms(of_ref[0].astype(F32), gf_ref[...]).astype(BF16)
    nm = _rms(om_ref[0].astype(F32), gm_ref[...]).astype(BF16)
    h1 = x_ref[0] + _dot(nf, wmix_ref[0:FOX_WIDTH, :]) + _dot(nm, wmix_ref[FOX_WIDTH:FOX_WIDTH + MLA_WIDTH, :])

    q = (_dot(_rms(h1, gq_ref[...]).astype(BF16), wq_ref[...]) * MEM_QSCALE).astype(BF16)
    heads = []
    for h in range(MEM_HEADS):
        sl = slice(h * MEM_HEAD_DIM, (h + 1) * MEM_HEAD_DIM)
        sc = _dot_nt(q[:, sl], k_ref[0, :, sl])
        m = jnp.max(sc, axis=-1, keepdims=True)
        pm = jnp.exp2(sc - m)
        l = jnp.sum(pm, axis=-1, keepdims=True)
        heads.append((_dot(pm.astype(BF16), v_ref[0, :, sl]) / l).astype(BF16))
    h2 = h1 + _dot(jnp.concatenate(heads, axis=1), wo_ref[...])
    h2_ref[0] = h2

    a = _rms(h2, gffn_ref[...])
    _store_token_tiles(at_ref, a)
    a_hi = a.astype(BF16)
    a_lo = (a - a_hi.astype(F32)).astype(BF16)
    logits = _dot(a_hi, wrh_ref[...]) + _dot(a_lo, wrh_ref[...]) + _dot(a_hi, wrl_ref[...]) + br_ref[...]
    lane = lax.broadcasted_iota(I32, (ts, LANES), 1)
    lane_f = lane.astype(F32)
    work = jnp.where(lane < N_EXPERTS, logits, NEG)
    vals, idxs, sels = [], [], []
    for _ in range(TOP_K):
        mk = jnp.max(work, axis=-1, keepdims=True)
        ik = jnp.min(jnp.where(work == mk, lane_f, float(LANES)), axis=-1, keepdims=True)
        sel = lane_f == ik
        work = jnp.where(sel, NEG, work)
        vals.append(mk)
        idxs.append(ik)
        sels.append(sel)
    exps = [jnp.exp(vk - vals[0]) for vk in vals]
    denom = exps[0] + exps[1] + exps[2] + exps[3]
    chosen = jnp.logical_or(jnp.logical_or(sels[0], sels[1]), jnp.logical_or(sels[2], sels[3]))
    onehot = jnp.where(chosen, 1.0, 0.0)
    carry = carry_ref[...]
    rank = _dot(tri_ref[...], onehot.astype(BF16)) + carry
    carry_new = carry + jnp.sum(onehot, axis=0, keepdims=True)
    carry_ref[...] = carry_new
    cnt_ref[...] = carry_new
    meta = jnp.zeros((ts, LANES), F32)
    for kk in range(TOP_K):
        rk = jnp.sum(jnp.where(sels[kk], rank, 0.0), axis=-1, keepdims=True)
        meta = jnp.where(lane == kk, idxs[kk], meta)
        meta = jnp.where(lane == TOP_K + kk, rk, meta)
        meta = jnp.where(lane == 2 * TOP_K + kk, exps[kk] / denom, meta)
    meta_ref[0] = meta
    metat_ref[...] = jnp.transpose(meta)[0:META_ROWS, :]


def _memrouter(o_fox, o_mla, x, g_fox, g_mla, w_mix, g_mq, w_mq, kmem, vmem, w_mo, g_ffn, wr_hi, wr_lo, b_r):
    b, s, d = x.shape
    m = kmem.shape[1]
    ts = min(TS_MEM, s)

    def full(arr):
        return pl.BlockSpec(arr.shape, lambda bi, si: (0,) * arr.ndim)

    def tok(width):
        return pl.BlockSpec((1, ts, width), lambda bi, si: (bi, si, 0))

    kv = pl.BlockSpec((1, m, MEM_WIDTH), lambda bi, si: (bi, 0, 0))
    cnt = pl.BlockSpec((1, LANES), lambda bi, si: (0, 0))
    tri = jnp.asarray(np.tril(np.ones((ts, ts), np.float32), k=-1), BF16)
    return pl.pallas_call(
        functools.partial(_memrouter_kernel, ts=ts),
        grid=(b, s // ts),
        in_specs=[tok(FOX_WIDTH), tok(MLA_WIDTH), tok(d), full(g_fox), full(g_mla), full(w_mix), full(g_mq),
                  full(w_mq), kv, kv, full(w_mo), full(g_ffn), full(wr_hi), full(wr_lo), full(b_r), full(tri)],
        out_specs=(tok(d), pl.BlockSpec((ts * SUBLANES, LANES), lambda bi, si: (bi * (s // ts) + si, 0)),
                   tok(LANES), pl.BlockSpec((META_ROWS, ts), lambda bi, si: (0, bi * (s // ts) + si)), cnt),
        out_shape=(jax.ShapeDtypeStruct((b, s, d), F32), jax.ShapeDtypeStruct((b * s * SUBLANES, LANES), F32),
                   jax.ShapeDtypeStruct((b, s, LANES), F32), jax.ShapeDtypeStruct((META_ROWS, b * s), F32),
                   jax.ShapeDtypeStruct((1, LANES), F32)),
        scratch_shapes=[pltpu.VMEM((1, LANES), F32)],
        compiler_params=_cparams(2),
        name="memrouter",
    )(o_fox, o_mla, x, g_fox, g_mla, w_mix, g_mq, w_mq, kmem, vmem, w_mo, g_ffn, wr_hi, wr_lo, b_r, tri)


def _tile_copy(src_ref, src_row8, dst_ref, dst_row8, sem):
    return pltpu.make_async_copy(src_ref.at[pl.ds(pl.multiple_of(src_row8, SUBLANES), SUBLANES), :],
                                 dst_ref.at[pl.ds(pl.multiple_of(dst_row8, SUBLANES), SUBLANES), :], sem)


def _dispatch_kernel(last_ref, grp_ref, nv_ref, pos_ref, at_ref, xs_ref, zbuf, zsem, sem, *, td, tm, min_used,
                     n_tiles):
    @pl.when(pl.program_id(0) == 0)
    def _():
        zbuf[...] = jnp.zeros_like(zbuf)
        grp = tm // MOE_ROW_GROUPS

        def fill(row):
            return pltpu.make_async_copy(
                zbuf, xs_ref.at[pl.ds(pl.multiple_of(row * SUBLANES, SUBLANES), grp * SUBLANES), :], zsem)

        groups = range(MOE_ROW_GROUPS)
        fills = [(gi >= grp_ref[e], last_ref[e] + gi * grp) for e in range(N_EXPERTS) for gi in groups]
        fills += [(t >= nv_ref[0], t * tm + gi * grp) for t in range(min_used, n_tiles) for gi in groups]
        for pred, row in fills:
            pl.when(pred)(lambda row=row: fill(row).start())
        for pred, row in fills:
            pl.when(pred)(lambda row=row: fill(row).wait())

    def issue(r, c):
        for kk in range(TOP_K):
            _tile_copy(at_ref, r * SUBLANES, xs_ref, pos_ref[kk * td + r], sem).start(priority=kk % 2)
        return c

    lax.fori_loop(0, td, issue, 0, unroll=8)
    for _ in range(TOP_K):
        pltpu.make_async_copy(at_ref, xs_ref.at[pl.ds(0, td * SUBLANES), :], sem).wait()


def _dispatch(last_tile_row, last_group, nv, pos8, at, n_tiles, tm):
    n = at.shape[0] // SUBLANES
    td = min(TD_DISPATCH, n)
    grid_spec = pltpu.PrefetchScalarGridSpec(
        num_scalar_prefetch=3,
        grid=(n // td,),
        in_specs=[pl.BlockSpec((TOP_K * td,), lambda i, *_: (i,), memory_space=pltpu.SMEM),
                  pl.BlockSpec((td * SUBLANES, LANES), lambda i, *_: (i, 0))],
        out_specs=pl.BlockSpec(memory_space=pl.ANY),
        scratch_shapes=[pltpu.VMEM((tm // MOE_ROW_GROUPS * SUBLANES, LANES), F32), pltpu.SemaphoreType.DMA,
                        pltpu.SemaphoreType.DMA],
    )
    min_used = n * TOP_K // tm
    return pl.pallas_call(
        functools.partial(_dispatch_kernel, td=td, tm=tm, min_used=min_used, n_tiles=n_tiles),
        grid_spec=grid_spec,
        out_shape=jax.ShapeDtypeStruct((n_tiles * tm * SUBLANES, LANES), F32),
        compiler_params=_cparams(1),
        name="dispatch",
    )(last_tile_row, last_group, nv, pos8, at)


def _moe_kernel(te_ref, rows_ref, slot_ref, nxt_ref, xs_ref, wgu_hbm, bgu_ref, wd_hbm, bd_ref, y_ref,
                wgu_f32, wd_f32, wgu_bf, wd_bf, sem, *, ch, tm):
    i = pl.program_id(0)
    d_exp = wd_hbm.shape[1]
    rows_used = rows_ref[i]
    valid = rows_used > 0
    new_expert = jnp.logical_or(i == 0, te_ref[i] != te_ref[jnp.maximum(i - 1, 0)])

    def weight_copies(e, sl):
        return (pltpu.make_async_copy(wgu_hbm.at[e], wgu_f32.at[sl], sem.at[sl, 0]),
                pltpu.make_async_copy(wd_hbm.at[e], wd_f32.at[sl], sem.at[sl, 1]))

    @pl.when(i == 0)
    def _():
        for c in weight_copies(te_ref[0], slot_ref[0]):
            c.start()

    @pl.when(jnp.logical_and(new_expert, valid))
    def _():
        sl = slot_ref[i]
        for c in weight_copies(te_ref[i], sl):
            c.wait()
        wgu_bf[...] = wgu_f32[sl].astype(BF16)
        wd_bf[...] = wd_f32[sl].astype(BF16)

        @pl.when(nxt_ref[i] >= 0)
        def _():
            for c in weight_copies(nxt_ref[i], 1 - sl):
                c.start()

    def ffn(rows):
        x = _load_token_tiles(xs_ref, rows).astype(BF16)
        acc = jnp.zeros((rows, wd_hbm.shape[2]), F32)
        for c in range(d_exp // ch):
            def gu(lo):
                return _dot(x, wgu_bf[:, lo:lo + ch]) + bgu_ref[0, :, lo:lo + ch]

            gate = jnp.minimum(gu(c * ch), SWIGLU_LIMIT)
            up = jnp.clip(gu(d_exp + c * ch), -SWIGLU_LIMIT, SWIGLU_LIMIT)
            act = gate * (1.0 / (1.0 + jnp.exp(-SWIGLU_ALPHA * gate))) * (up + 1.0)
            acc = acc + _dot(act.astype(BF16), wd_bf[c * ch:(c + 1) * ch, :])
        return acc + bd_ref[0]

    grp = tm // MOE_ROW_GROUPS
    for used in range(1, MOE_ROW_GROUPS + 1):
        rows = used * grp

        def partial_tile(rows=rows):
            _store_token_tiles(y_ref, ffn(rows))
            if rows < tm:
                y_ref[rows * SUBLANES:, :] = jnp.zeros(((tm - rows) * SUBLANES, LANES), F32)

        pl.when(jnp.logical_and(rows_used > rows - grp, rows_used <= rows))(partial_tile)

    @pl.when(jnp.logical_not(valid))
    def _():
        y_ref[...] = jnp.zeros_like(y_ref)


def _moe(te, rows, slot, nxt, xs, wgu, bgu, wd, bd, tm):
    n_tiles = xs.shape[0] // (tm * SUBLANES)
    _, d, d_exp2 = wgu.shape
    d_exp = wd.shape[1]
    assert d == SUBLANES * LANES and wd.shape[2] == d
    tile = pl.BlockSpec((tm * SUBLANES, LANES), lambda i, te, *_: (i, 0))
    grid_spec = pltpu.PrefetchScalarGridSpec(
        num_scalar_prefetch=4,
        grid=(n_tiles,),
        in_specs=[
            tile,
            pl.BlockSpec(memory_space=pl.ANY),
            pl.BlockSpec((1, 1, d_exp2), lambda i, te, *_: (te[i], 0, 0)),
            pl.BlockSpec(memory_space=pl.ANY),
            pl.BlockSpec((1, 1, d), lambda i, te, *_: (te[i], 0, 0)),
        ],
        out_specs=tile,
        scratch_shapes=[pltpu.VMEM((2, d, d_exp2), F32), pltpu.VMEM((2, d_exp, d), F32),
                        pltpu.VMEM((d, d_exp2), BF16), pltpu.VMEM((d_exp, d), BF16),
                        pltpu.SemaphoreType.DMA((2, 2))],
    )
    return pl.pallas_call(
        functools.partial(_moe_kernel, ch=min(MOE_CH, d_exp), tm=tm),
        grid_spec=grid_spec,
        out_shape=jax.ShapeDtypeStruct(xs.shape, F32),
        compiler_params=_cparams(1),
        name="moe",
    )(te, rows, slot, nxt, xs, wgu, bgu, wd, bd)


def _combine_kernel(pos_ref, posn_ref, h2_ref, meta_ref, y_ref, gf_ref, o_ref, ybuf, sem, *, td, final_norm):
    i = pl.program_id(0)
    slot = i % 2

    def issue_block(p_ref, sl):
        def issue(r, c):
            for kk in range(TOP_K):
                _tile_copy(y_ref, p_ref[kk * td + r], ybuf.at[sl, kk], r * SUBLANES,
                           sem.at[sl]).start(priority=kk % 2)
            return c

        lax.fori_loop(0, td, issue, 0, unroll=8)

    @pl.when(i == 0)
    def _():
        issue_block(pos_ref, 0)

    @pl.when(i + 1 < pl.num_programs(0))
    def _():
        issue_block(posn_ref, 1 - slot)

    for kk in range(TOP_K):
        pltpu.make_async_copy(y_ref.at[pl.ds(0, td * SUBLANES), :], ybuf.at[slot, kk], sem.at[slot]).wait()

    meta = meta_ref[...]
    acc = h2_ref[...]
    for kk in range(TOP_K):
        gate = meta[:, 2 * TOP_K + kk:2 * TOP_K + kk + 1]
        acc = acc + gate * _load_token_tiles(ybuf.at[slot, kk], td)
    o_ref[...] = _rms(acc, gf_ref[...]) if final_norm else acc


def _combine(pos8, h2, meta, y, g_final, final_norm):
    n, d = h2.shape
    td = min(TD_COMBINE, n)
    n_blocks = n // td
    grid_spec = pltpu.PrefetchScalarGridSpec(
        num_scalar_prefetch=0,
        grid=(n_blocks,),
        in_specs=[pl.BlockSpec((TOP_K * td,), lambda i: (i,), memory_space=pltpu.SMEM),
                  pl.BlockSpec((TOP_K * td,), lambda i: (jnp.minimum(i + 1, n_blocks - 1),), memory_space=pltpu.SMEM),
                  pl.BlockSpec((td, d), lambda i: (i, 0)),
                  pl.BlockSpec((td, LANES), lambda i: (i, 0)),
                  pl.BlockSpec(memory_space=pl.ANY),
                  pl.BlockSpec(g_final.shape, lambda i: (0, 0))],
        out_specs=pl.BlockSpec((td, d), lambda i: (i, 0)),
        scratch_shapes=[pltpu.VMEM((2, TOP_K, td * SUBLANES, LANES), F32), pltpu.SemaphoreType.DMA((2,))],
    )
    return pl.pallas_call(
        functools.partial(_combine_kernel, td=td, final_norm=final_norm),
        grid_spec=grid_spec,
        out_shape=jax.ShapeDtypeStruct((n, d), F32),
        compiler_params=_cparams(1),
        name="combine",
    )(pos8, pos8, h2, meta, y, g_final)


def _rope_tables(seq_len):
    inv = 1.0 / (ROPE_BASE ** (jnp.arange(0, MLA_ROPE_DIM, 2, dtype=F32) / MLA_ROPE_DIM))
    ang = jnp.arange(seq_len, dtype=F32)[:, None] * inv[None, :]
    cos, sin = jnp.cos(ang), jnp.sin(ang)
    pad = jnp.zeros((seq_len, LANES - 2 * MLA_ROPE_DIM), F32)
    cos_t = jnp.concatenate([cos, cos, cos, cos, pad], axis=1)
    sin_t = jnp.concatenate([sin, sin, sin, sin, pad], axis=1)
    return cos_t, sin_t


def _rot_cols(w):
    half = MLA_ROPE_DIM // 2
    return jnp.concatenate([-w[:, half:], w[:, :half]], axis=1)


def _rope_pair_block(r0, r1):
    pad = jnp.zeros((r0.shape[0], LANES - 2 * MLA_ROPE_DIM), F32)
    return jnp.concatenate([r0, r1, pad], axis=1)


def _prep_inproj_weights(w_in, w_uq, w_ukv):
    d = w_in.shape[0]
    pts = np.cumsum((FOX_WIDTH, FOX_WIDTH, FOX_WIDTH, FOX_HEADS, MLA_Q_LORA, MLA_KV_LORA, MLA_ROPE_DIM))
    w_fl, w_ql = w_in[:, pts[2]:pts[3]], w_in[:, pts[3]:pts[4]]
    w_kvl, w_kr = w_in[:, pts[4]:pts[5]], w_in[:, pts[5]:pts[6]]
    misc = jnp.concatenate([w_fl, jnp.zeros((d, LANES - FOX_HEADS), F32)], axis=1)
    w_tail = jnp.concatenate([w_ql, w_kvl, misc, _rope_pair_block(w_kr, w_kr),
                              _rope_pair_block(_rot_cols(w_kr), _rot_cols(w_kr))], axis=1).astype(BF16)
    qd = MLA_NOPE_DIM + MLA_ROPE_DIM
    kvd = MLA_NOPE_DIM + MLA_V_DIM
    q_nope = [w_uq[:, h * qd:h * qd + MLA_NOPE_DIM] for h in range(MLA_HEADS)]
    q_rope = [w_uq[:, h * qd + MLA_NOPE_DIM:(h + 1) * qd] for h in range(MLA_HEADS)]
    wqa = [_rope_pair_block(q_rope[h], q_rope[h + 1]) for h in range(0, MLA_HEADS, 2)]
    wqb = [_rope_pair_block(_rot_cols(q_rope[h]), _rot_cols(q_rope[h + 1])) for h in range(0, MLA_HEADS, 2)]
    wk = [w_ukv[:, h * kvd:h * kvd + MLA_NOPE_DIM] for h in range(MLA_HEADS)]
    wv = [w_ukv[:, h * kvd + MLA_NOPE_DIM:(h + 1) * kvd] for h in range(MLA_HEADS)]
    cat = lambda xs: jnp.concatenate(xs, axis=1).astype(BF16)
    w1 = jnp.concatenate([w_in[:, :pts[2]].astype(BF16), w_tail], axis=1)
    return w1, cat(q_nope), cat(wqa), cat(wqb), cat(wk), cat(wv)


def _routing_tables(counts, n_tiles, tm):
    tiles_e = (counts + tm - 1) // tm
    tile_end = jnp.cumsum(tiles_e)
    off = ((tile_end - tiles_e) * tm).astype(I32)
    total = tile_end[-1]
    ti = jnp.arange(n_tiles, dtype=I32)
    te = jnp.minimum(jnp.sum(ti[:, None] >= tile_end[None, :], axis=1), N_EXPERTS - 1).astype(I32)
    valid = ti < total
    last_e = jnp.sum(jnp.where(ti == total - 1, te, 0))
    te = jnp.where(valid, te, last_e).astype(I32)
    last_tile_row = (off + (tiles_e - 1) * tm).astype(I32)
    grp = tm // MOE_ROW_GROUPS
    last_group = jnp.where(counts > 0, ((counts - 1) % tm) // grp, MOE_ROW_GROUPS).astype(I32)
    used = tiles_e > 0
    ids = jnp.arange(N_EXPERTS, dtype=I32)
    slot_e = (jnp.cumsum(used.astype(I32)) - 1) % 2
    later = jnp.where(used[None, :] & (ids[None, :] > ids[:, None]), ids[None, :], N_EXPERTS)
    nxt_e = jnp.min(later, axis=1)
    nxt_e = jnp.where(nxt_e == N_EXPERTS, -1, nxt_e).astype(I32)
    pick = te[:, None] == ids[None, :]
    slot = jnp.sum(jnp.where(pick, slot_e[None, :], 0), axis=1).astype(I32)
    nxt = jnp.sum(jnp.where(pick, nxt_e[None, :], 0), axis=1).astype(I32)
    end_row = jnp.sum(jnp.where(pick, (off + counts)[None, :], 0), axis=1)
    rows = jnp.where(valid, jnp.clip(end_row - ti * tm, 0, tm), 0).astype(I32)
    return off, last_tile_row, last_group, te, total.astype(I32).reshape(1), slot, nxt, rows


def kernel(x, mem, g_mix, w_in, b_f, g_q_lat, w_uq, g_kv_lat, w_ukv, g_fox_out, g_mla_out, w_o, g_mem_q, w_mem_q,
           g_mem_kv, w_mem_kv, w_mem_o, g_ffn, w_router, b_router, w_gate_up, b_gate_up, w_down, b_down, g_final):
    b, s, d = x.shape
    n = b * s
    depth = g_mix.shape[0]
    cos_t, sin_t = _rope_tables(s)
    row = lambda v: v.reshape(1, -1)
    h = x
    for l in range(depth):
        w1, wqn, wqa, wqb, wk, wv = _prep_inproj_weights(w_in[l], w_uq[l], w_ukv[l])
        fq, fk, fv, flog, qm, km, vm = _inproj(h, row(g_mix[l]), w1, row(g_q_lat[l]), wqn, wqa, wqb,
                                                row(g_kv_lat[l]), wk, wv, cos_t, sin_t)
        cq, ck = _decay(flog, b_f[l])
        o_fox = _attention_pipelined(True, fq, fk, fv, cq, ck)
        o_mla = _attention_pipelined(False, qm, km, vm)
        kmem, vmem = _memkv(mem, row(g_mem_kv[l]), w_mem_kv[l].astype(BF16))
        wr = jnp.zeros((d, LANES), F32).at[:, :N_EXPERTS].set(w_router[l])
        wr_hi = wr.astype(BF16)
        wr_lo = (wr - wr_hi.astype(F32)).astype(BF16)
        br = jnp.zeros((1, LANES), F32).at[0, :N_EXPERTS].set(b_router[l])
        h2, at, meta, meta_t, cnt = _memrouter(
            o_fox, o_mla, h, row(g_fox_out[l]), row(g_mla_out[l]), w_o[l].astype(BF16), row(g_mem_q[l]),
            w_mem_q[l].astype(BF16), kmem, vmem, w_mem_o[l].astype(BF16), row(g_ffn[l]), wr_hi, wr_lo, br)
        ek = meta_t[0:TOP_K].astype(I32)
        rk = meta_t[TOP_K:2 * TOP_K].astype(I32)
        n_tiles = n * TOP_K // TM_MOE + N_EXPERTS
        counts = cnt[0, :N_EXPERTS].astype(I32)
        off, last_tile_row, last_group, te, nv, slot, nxt, rows = _routing_tables(counts, n_tiles, TM_MOE)
        ids = jnp.arange(N_EXPERTS, dtype=I32)[:, None, None]
        off_of = jnp.sum(jnp.where(ek[None] == ids, off[:, None, None], 0), axis=0)
        pos8 = ((off_of + rk) * SUBLANES).astype(I32)

        def by_block(td):
            return pos8.reshape(TOP_K, n // td, td).transpose(1, 0, 2).reshape(-1)

        xs = _dispatch(last_tile_row, last_group, nv, by_block(min(TD_DISPATCH, n)), at, n_tiles, TM_MOE)
        y = _moe(te, rows, slot, nxt, xs, w_gate_up[l], b_gate_up[l].reshape(N_EXPERTS, 1, -1), w_down[l],
                 b_down[l].reshape(N_EXPERTS, 1, -1), TM_MOE)
        h = _combine(by_block(min(TD_COMBINE, n)), h2.reshape(n, d), meta.reshape(n, LANES), y, row(g_final),
                     l == depth - 1).reshape(b, s, d)
    return h
```

```python
import functools

import numpy as np
import jax
import jax.numpy as jnp
from jax import lax
from jax.experimental import pallas as pl
from jax.experimental.pallas import tpu as pltpu

F32 = jnp.float32
BF16 = jnp.bfloat16
I32 = jnp.int32

LANES = 128
SUBLANES = 8
RMS_EPS = 1e-6
CHUNK = 64
FOX_HEADS = 8
FOX_HEAD_DIM = 64
FOX_WIDTH = FOX_HEADS * FOX_HEAD_DIM
MLA_HEADS = 8
MLA_Q_LORA = 384
MLA_KV_LORA = 256
MLA_NOPE_DIM = 64
MLA_ROPE_DIM = 32
MLA_V_DIM = 64
MLA_WIDTH = MLA_HEADS * MLA_V_DIM
ROPE_BASE = 10000.0
MEM_HEADS = 4
MEM_HEAD_DIM = 128
MEM_WIDTH = MEM_HEADS * MEM_HEAD_DIM
N_EXPERTS = 32
TOP_K = 4
SWIGLU_LIMIT = 7.0
SWIGLU_ALPHA = 1.702

LOG2E = 1.4426950408889634
NEG = -1e30
FOX_QSCALE = FOX_HEAD_DIM ** -0.5 * LOG2E
MLA_QSCALE = (MLA_NOPE_DIM + MLA_ROPE_DIM) ** -0.5 * LOG2E
MEM_QSCALE = MEM_HEAD_DIM ** -0.5 * LOG2E
FOX_DECAY_LANES = 16
META_ROWS = 16

VMEM_LIMIT = 56 * 1024 * 1024

TS_PROJ = 1024
TS_MEM = 1024
T_ATT = 512
DECAY_BLK = 256
TM_MOE = 512
MOE_ROW_GROUPS = 4
TD_DISPATCH = 1024
TD_COMBINE = 512
MOE_CH = 512


def _cparams(n_axes):
    return pltpu.CompilerParams(dimension_semantics=("arbitrary",) * n_axes, vmem_limit_bytes=VMEM_LIMIT)


def _dot(a, b):
    return jnp.dot(a, b, preferred_element_type=F32)


def _dot_nt(a, b):
    return lax.dot_general(a, b, (((1,), (1,)), ((), ())), preferred_element_type=F32)


def _rms(x, g):
    return x * lax.rsqrt(jnp.mean(x * x, axis=-1, keepdims=True) + RMS_EPS) * g


def _split3(x):
    hi = x.astype(BF16)
    r1 = x - hi.astype(F32)
    mid = r1.astype(BF16)
    lo = (r1 - mid.astype(F32)).astype(BF16)
    return hi, mid, lo


def _interleave_blocks(a, b):
    parts = []
    for p in range(a.shape[1] // LANES):
        parts += [a[:, p * LANES:(p + 1) * LANES], b[:, p * LANES:(p + 1) * LANES]]
    return jnp.concatenate(parts, axis=1)


def _inproj_kernel(x_ref, g_ref, w1_ref, gq_ref, wqn_ref, wqa_ref, wqb_ref, gkv_ref, wk_ref, wv_ref,
                   cos_ref, sin_ref, fq_ref, fk_ref, fv_ref, fl_ref, qm_ref, km_ref, vm_ref):
    a = _rms(x_ref[0], g_ref[...]).astype(BF16)
    c0 = 3 * FOX_WIDTH
    c1 = c0 + MLA_Q_LORA
    c2 = c1 + MLA_KV_LORA

    def proj(lo, hi):
        return _dot(a, w1_ref[:, lo:hi])

    fq_ref[0] = (proj(0, FOX_WIDTH) * FOX_QSCALE).astype(BF16)
    fk_ref[0] = proj(FOX_WIDTH, 2 * FOX_WIDTH).astype(BF16)
    fv_ref[0] = proj(2 * FOX_WIDTH, c0).astype(BF16)
    qn = _rms(proj(c0, c1), gq_ref[...]).astype(BF16)
    kvn = _rms(proj(c1, c2), gkv_ref[...]).astype(BF16)
    fl_ref[0] = proj(c2, c2 + LANES)
    cos = cos_ref[...]
    sin = sin_ref[...]
    pairs = MLA_HEADS // 2
    kpe = (proj(c2 + LANES, c2 + 2 * LANES) * cos + proj(c2 + 2 * LANES, c2 + 3 * LANES) * sin).astype(BF16)
    cos4 = jnp.concatenate([cos] * pairs, axis=1)
    sin4 = jnp.concatenate([sin] * pairs, axis=1)
    q_nope = (_dot(qn, wqn_ref[...]) * MLA_QSCALE).astype(BF16)
    q_rope = ((_dot(qn, wqa_ref[...]) * cos4 + _dot(qn, wqb_ref[...]) * sin4) * MLA_QSCALE).astype(BF16)
    qm_ref[0] = _interleave_blocks(q_nope, q_rope)
    k_nope = _dot(kvn, wk_ref[...]).astype(BF16)
    km_ref[0] = _interleave_blocks(k_nope, jnp.concatenate([kpe] * pairs, axis=1))
    vm_ref[0] = _dot(kvn, wv_ref[...]).astype(BF16)


def _inproj(x, g_mix, w1, g_q, wqn, wqa, wqb, g_kv, wk, wv, cos_t, sin_t):
    b, s, d = x.shape
    ts = min(TS_PROJ, s)

    def full(arr):
        return pl.BlockSpec(arr.shape, lambda bi, si: (0,) * arr.ndim)

    def tok(width):
        return pl.BlockSpec((1, ts, width), lambda bi, si: (bi, si, 0))

    tab = pl.BlockSpec((ts, LANES), lambda bi, si: (si, 0))
    pair_w = 2 * LANES * (MLA_HEADS // 2)
    out_shapes = (
        jax.ShapeDtypeStruct((b, s, FOX_WIDTH), BF16),
        jax.ShapeDtypeStruct((b, s, FOX_WIDTH), BF16),
        jax.ShapeDtypeStruct((b, s, FOX_WIDTH), BF16),
        jax.ShapeDtypeStruct((b, s, LANES), F32),
        jax.ShapeDtypeStruct((b, s, pair_w), BF16),
        jax.ShapeDtypeStruct((b, s, pair_w), BF16),
        jax.ShapeDtypeStruct((b, s, MLA_WIDTH), BF16),
    )
    return pl.pallas_call(
        _inproj_kernel,
        grid=(b, s // ts),
        in_specs=[tok(d), full(g_mix), full(w1), full(g_q), full(wqn), full(wqa), full(wqb), full(g_kv), full(wk),
                  full(wv), tab, tab],
        out_specs=(tok(FOX_WIDTH), tok(FOX_WIDTH), tok(FOX_WIDTH), tok(LANES), tok(pair_w), tok(pair_w),
                   tok(MLA_WIDTH)),
        out_shape=out_shapes,
        compiler_params=_cparams(2),
        name="inproj",
    )(x, g_mix, w1, g_q, wqn, wqa, wqb, g_kv, wk, wv, cos_t, sin_t)


def _decay_kernel(fl_ref, bf_ref, place_ref, ones_ref, cq_ref, ck_ref, *, blk):
    s = fl_ref.shape[1]
    row = lax.broadcasted_iota(I32, (blk, blk), 0)
    col = lax.broadcasted_iota(I32, (blk, blk), 1)
    tri = jnp.where(row >= col, 1.0, 0.0).astype(BF16)
    carry = jnp.zeros((1, LANES), F32)
    for i in range(s // blk):
        sl = slice(i * blk, (i + 1) * blk)
        z = fl_ref[0, sl, :] + bf_ref[...]
        lf = (jnp.minimum(z, 0.0) - jnp.log1p(jnp.exp(-jnp.abs(z)))) * LOG2E
        part = _dot(tri, jnp.concatenate(_split3(lf), axis=1))
        cs = part[:, :LANES] + part[:, LANES:2 * LANES] + part[:, 2 * LANES:] + carry
        carry = cs[blk - 1:blk, :]
        placed = _dot(jnp.concatenate(_split3(cs), axis=1), place_ref[...]) + ones_ref[...]
        cq_ref[0, sl, :] = placed[:, :LANES].astype(BF16)
        ck_ref[0, sl, :] = placed[:, LANES:].astype(BF16)


def _decay_tables():
    place = np.zeros((3 * LANES, 2 * LANES), np.float32)
    ones = np.zeros((1, 2 * LANES), np.float32)
    for h in range(FOX_HEADS):
        for part in range(3):
            place[part * LANES + h, FOX_DECAY_LANES * h + part] = 1.0
            place[part * LANES + h, LANES + FOX_DECAY_LANES * h + 3 + part] = -1.0
            ones[0, FOX_DECAY_LANES * h + 3 + part] = 1.0
            ones[0, LANES + FOX_DECAY_LANES * h + part] = 1.0
    return jnp.asarray(place, BF16), jnp.asarray(ones)


def _decay(flog, b_f):
    b, s, _ = flog.shape
    blk = min(DECAY_BLK, s)
    place, ones = _decay_tables()
    bf = jnp.zeros((1, LANES), F32).at[0, :FOX_HEADS].set(b_f)

    def full(arr):
        return pl.BlockSpec(arr.shape, lambda bi: (0,) * arr.ndim)

    seq = pl.BlockSpec((1, s, LANES), lambda bi: (bi, 0, 0))
    return pl.pallas_call(
        functools.partial(_decay_kernel, blk=blk),
        grid=(b,),
        in_specs=[seq, full(bf), full(place), full(ones)],
        out_specs=(seq, seq),
        out_shape=(jax.ShapeDtypeStruct((b, s, LANES), BF16), jax.ShapeDtypeStruct((b, s, LANES), BF16)),
        compiler_params=_cparams(1),
        name="decay",
    )(flog, bf, place, ones)


def _attn_pipe_kernel(*refs, fox, t, nq, pairs):
    if fox:
        q_ref, cq_ref, k_ref, ck_ref, v_ref, o_ref, s_even, s_odd = refs
        group = FOX_DECAY_LANES
    else:
        q_ref, k_ref, v_ref, o_ref, s_even, s_odd = refs
        group = MLA_ROPE_DIM
    g = pl.program_id(0)
    i = g % nq
    pair = (jnp.minimum(g, pl.num_programs(0) - 2) // nq) % pairs
    base = 2 * pair if fox else 0
    lane = lax.broadcasted_iota(I32, (1, LANES), 1)

    @pl.when(g == 0)
    def _():
        s_odd[...] = jnp.zeros_like(s_odd)

    def body(iv):
        s_new, s_old = (s_even, s_odd) if iv % 2 == 0 else (s_odd, s_even)
        half = t // 2
        row = lax.broadcasted_iota(I32, (half, half), 0)
        col = lax.broadcasted_iota(I32, (half, half), 1)
        allowed = (col <= row) if fox else ((col // CHUNK) <= (row // CHUNK))
        past = iv * t
        if fox:
            q_main, q_extra = q_ref[0], cq_ref[0]
        else:
            q_main, q_extra = q_ref[0, :, :LANES], q_ref[0, :, LANES:]
        zero = jnp.zeros_like(q_main)

        def keys(lo, hi):
            if fox:
                return jnp.concatenate([k_ref[0, lo:hi, :], ck_ref[0, lo:hi, :]], axis=1)
            return k_ref[0, lo:hi, :]

        qa = [jnp.concatenate([jnp.where((lane // (LANES // 2)) == hh, q_main, zero),
                               jnp.where((lane // group) == base + hh, q_extra, zero)], axis=1) for hh in range(2)]
        for hh in range(2):
            if iv > 0:
                s_new[hh, :, 0:past] = _dot_nt(qa[hh], keys(0, past))
            upper = _dot_nt(qa[hh][:half], keys(past, past + half))
            s_new[hh, :half, past:past + half] = jnp.where(allowed, upper, NEG)
            lower = _dot_nt(qa[hh][half:], keys(past, past + t))
            s_new[hh, half:, past:past + half] = lower[:, :half]
            s_new[hh, half:, past + half:past + t] = jnp.where(allowed, lower[:, half:], NEG)

        seen = ((iv - 1) % nq + 1) * t
        outs = []
        for hh in range(2):
            sc = s_old[hh, :, 0:seen - half]
            sc_tail = s_old[hh, half:, seen - half:seen]
            m_main = jnp.max(sc, axis=-1, keepdims=True)
            m_low = jnp.maximum(m_main[half:], jnp.max(sc_tail, axis=-1, keepdims=True))
            p = jnp.exp2((sc - jnp.concatenate([m_main[:half], m_low], axis=0)).astype(BF16))
            p_tail = jnp.exp2((sc_tail - m_low).astype(BF16))
            own = (lane // (LANES // 2)) == hh
            one = jnp.ones((1, LANES), BF16)
            acc = _dot(p, jnp.where(own, v_ref[0, 0:seen - half, :], one))
            acc_tail = _dot(p_tail, jnp.where(own, v_ref[0, seen - half:seen, :], one))
            acc = jnp.concatenate([acc[:half], acc[half:] + acc_tail], axis=0)
            other = (1 - hh) * (LANES // 2)
            outs.append(acc * (1.0 / acc[:, other:other + 1]))
        o_ref[0] = jnp.where(lane < LANES // 2, outs[0], outs[1]).astype(o_ref.dtype)

    for iv in range(nq):
        pl.when(i == iv)(functools.partial(body, iv))


def _attention_pipelined(fox, q, k, v, cq=None, ck=None):
    b, s, _ = v.shape
    t = min(T_ATT, s)
    nq = s // t
    assert nq % 2 == 0
    pairs = v.shape[2] // LANES
    qw = q.shape[2] // pairs
    kw = k.shape[2] // pairs
    items = b * pairs * nq

    def item(g):
        return g // (pairs * nq), (g // nq) % pairs, g % nq

    def cur(g):
        return item(jnp.minimum(g, items - 1))

    def prev(g):
        return item(jnp.maximum(g - 1, 0))

    qspec = pl.BlockSpec((1, t, qw), lambda g: (cur(g)[0], cur(g)[2], cur(g)[1]))
    kspec = pl.BlockSpec((1, s, kw), lambda g: (cur(g)[0], 0, cur(g)[1]))
    vspec = pl.BlockSpec((1, s, LANES), lambda g: (prev(g)[0], 0, prev(g)[1]))
    ospec = pl.BlockSpec((1, t, LANES), lambda g: (prev(g)[0], prev(g)[2], prev(g)[1]))
    if fox:
        cqspec = pl.BlockSpec((1, t, LANES), lambda g: (cur(g)[0], cur(g)[2], 0))
        ckspec = pl.BlockSpec((1, s, LANES), lambda g: (cur(g)[0], 0, 0))
        in_specs = [qspec, cqspec, kspec, ckspec, vspec]
        args = (q, cq, k, ck, v)
    else:
        in_specs = [qspec, kspec, vspec]
        args = (q, k, v)
    return pl.pallas_call(
        functools.partial(_attn_pipe_kernel, fox=fox, t=t, nq=nq, pairs=pairs),
        grid=(items + 1,),
        in_specs=in_specs,
        out_specs=ospec,
        out_shape=jax.ShapeDtypeStruct((b, s, pairs * LANES), BF16),
        scratch_shapes=[pltpu.VMEM((2, t, s), F32), pltpu.VMEM((2, t, s), F32)],
        compiler_params=_cparams(1),
        name="fox_attn" if fox else "mla_attn",
    )(*args)


def _memkv_kernel(mem_ref, g_ref, w_ref, k_ref, v_ref):
    a = _rms(mem_ref[0], g_ref[...]).astype(BF16)
    k_ref[0] = _dot(a, w_ref[:, 0:MEM_WIDTH]).astype(BF16)
    v_ref[0] = _dot(a, w_ref[:, MEM_WIDTH:2 * MEM_WIDTH]).astype(BF16)


def _memkv(mem, g, w):
    b, m, d = mem.shape
    kv = pl.BlockSpec((1, m, MEM_WIDTH), lambda bi: (bi, 0, 0))
    return pl.pallas_call(
        _memkv_kernel,
        grid=(b,),
        in_specs=[pl.BlockSpec((1, m, d), lambda bi: (bi, 0, 0)), pl.BlockSpec(g.shape, lambda bi: (0, 0)),
                  pl.BlockSpec(w.shape, lambda bi: (0, 0))],
        out_specs=(kv, kv),
        out_shape=(jax.ShapeDtypeStruct((b, m, MEM_WIDTH), BF16), jax.ShapeDtypeStruct((b, m, MEM_WIDTH), BF16)),
        compiler_params=_cparams(1),
        name="memkv",
    )(mem, g, w)


def _store_token_tiles(ref, val):
    rows = val.shape[0]
    for j in range(SUBLANES):
        ref[pl.ds(j, rows, stride=SUBLANES), :] = val[:, j * LANES:(j + 1) * LANES]


def _load_token_tiles(ref, rows):
    return jnp.concatenate([ref[pl.ds(j, rows, stride=SUBLANES), :] for j in range(SUBLANES)], axis=1)


def _memrouter_kernel(of_ref, om_ref, x_ref, gf_ref, gm_ref, wmix_ref, gq_ref, wq_ref, k_ref, v_ref, wo_ref,
                      gffn_ref, wrh_ref, wrl_ref, br_ref, tri_ref, h2_ref, at_ref, meta_ref, metat_ref, cnt_ref, carry_ref,
                      *, ts):
    first = jnp.logical_and(pl.program_id(0) == 0, pl.program_id(1) == 0)

    @pl.when(first)
    def _():
        carry_ref[...] = jnp.zeros_like(carry_ref)

    nf = _rms(of_ref[0].astype(F32), gf_ref[...]).astype(BF16)
    nm = _rms(om_ref[0].astype(F32), gm_ref[...]).astype(BF16)
    h1 = x_ref[0] + _dot(nf, wmix_ref[0:FOX_WIDTH, :]) + _dot(nm, wmix_ref[FOX_WIDTH:FOX_WIDTH + MLA_WIDTH, :])

    q = (_dot(_rms(h1, gq_ref[...]).astype(BF16), wq_ref[...]) * MEM_QSCALE).astype(BF16)
    heads = []
    for h in range(MEM_HEADS):
        sl = slice(h * MEM_HEAD_DIM, (h + 1) * MEM_HEAD_DIM)
        sc = _dot_nt(q[:, sl], k_ref[0, :, sl])
        m = jnp.max(sc, axis=-1, keepdims=True)
        pm = jnp.exp2(sc - m)
        l = jnp.sum(pm, axis=-1, keepdims=True)
        heads.append((_dot(pm.astype(BF16), v_ref[0, :, sl]) / l).astype(BF16))
    h2 = h1 + _dot(jnp.concatenate(heads, axis=1), wo_ref[...])
    h2_ref[0] = h2

    a = _rms(h2, gffn_ref[...])
    _store_token_tiles(at_ref, a)
    a_hi = a.astype(BF16)
    a_lo = (a - a_hi.astype(F32)).astype(BF16)
    logits = _dot(a_hi, wrh_ref[...]) + _dot(a_lo, wrh_ref[...]) + _dot(a_hi, wrl_ref[...]) + br_ref[...]
    lane = lax.broadcasted_iota(I32, (ts, LANES), 1)
    lane_f = lane.astype(F32)
    work = jnp.where(lane < N_EXPERTS, logits, NEG)
    vals, idxs, sels = [], [], []
    for _ in range(TOP_K):
        mk = jnp.max(work, axis=-1, keepdims=True)
        ik = jnp.min(jnp.where(work == mk, lane_f, float(LANES)), axis=-1, keepdims=True)
        sel = lane_f == ik
        work = jnp.where(sel, NEG, work)
        vals.append(mk)
        idxs.append(ik)
        sels.append(sel)
    exps = [jnp.exp(vk - vals[0]) for vk in vals]
    denom = exps[0] + exps[1] + exps[2] + exps[3]
    chosen = jnp.logical_or(jnp.logical_or(sels[0], sels[1]), jnp.logical_or(sels[2], sels[3]))
    onehot = jnp.where(chosen, 1.0, 0.0)
    carry = carry_ref[...]
    rank = _dot(tri_ref[...], onehot.astype(BF16)) + carry
    carry_new = carry + jnp.sum(onehot, axis=0, keepdims=True)
    carry_ref[...] = carry_new
    cnt_ref[...] = carry_new
    meta = jnp.zeros((ts, LANES), F32)
    for kk in range(TOP_K):
        rk = jnp.sum(jnp.where(sels[kk], rank, 0.0), axis=-1, keepdims=True)
        meta = jnp.where(lane == kk, idxs[kk], meta)
        meta = jnp.where(lane == TOP_K + kk, rk, meta)
        meta = jnp.where(lane == 2 * TOP_K + kk, exps[kk] / denom, meta)
    meta_ref[0] = meta
    metat_ref[...] = jnp.transpose(meta)[0:META_ROWS, :]


def _memrouter(o_fox, o_mla, x, g_fox, g_mla, w_mix, g_mq, w_mq, kmem, vmem, w_mo, g_ffn, wr_hi, wr_lo, b_r):
    b, s, d = x.shape
    m = kmem.shape[1]
    ts = min(TS_MEM, s)

    def full(arr):
        return pl.BlockSpec(arr.shape, lambda bi, si: (0,) * arr.ndim)

    def tok(width):
        return pl.BlockSpec((1, ts, width), lambda bi, si: (bi, si, 0))

    kv = pl.BlockSpec((1, m, MEM_WIDTH), lambda bi, si: (bi, 0, 0))
    cnt = pl.BlockSpec((1, LANES), lambda bi, si: (0, 0))
    tri = jnp.asarray(np.tril(np.ones((ts, ts), np.float32), k=-1), BF16)
    return pl.pallas_call(
        functools.partial(_memrouter_kernel, ts=ts),
        grid=(b, s // ts),
        in_specs=[tok(FOX_WIDTH), tok(MLA_WIDTH), tok(d), full(g_fox), full(g_mla), full(w_mix), full(g_mq),
                  full(w_mq), kv, kv, full(w_mo), full(g_ffn), full(wr_hi), full(wr_lo), full(b_r), full(tri)],
        out_specs=(tok(d), pl.BlockSpec((ts * SUBLANES, LANES), lambda bi, si: (bi * (s // ts) + si, 0)),
                   tok(LANES), pl.BlockSpec((META_ROWS, ts), lambda bi, si: (0, bi * (s // ts) + si)), cnt),
        out_shape=(jax.ShapeDtypeStruct((b, s, d), F32), jax.ShapeDtypeStruct((b * s * SUBLANES, LANES), F32),
                   jax.ShapeDtypeStruct((b, s, LANES), F32), jax.ShapeDtypeStruct((META_ROWS, b * s), F32),
                   jax.ShapeDtypeStruct((1, LANES), F32)),
        scratch_shapes=[pltpu.VMEM((1, LANES), F32)],
        compiler_params=_cparams(2),
        name="memrouter",
    )(o_fox, o_mla, x, g_fox, g_mla, w_mix, g_mq, w_mq, kmem, vmem, w_mo, g_ffn, wr_hi, wr_lo, b_r, tri)


def _tile_copy(src_ref, src_row8, dst_ref, dst_row8, sem):
    return pltpu.make_async_copy(src_ref.at[pl.ds(pl.multiple_of(src_row8, SUBLANES), SUBLANES), :],
                                 dst_ref.at[pl.ds(pl.multiple_of(dst_row8, SUBLANES), SUBLANES), :], sem)


def _dispatch_kernel(last_ref, grp_ref, nv_ref, pos_ref, at_ref, xs_ref, zbuf, zsem, sem, *, td, tm, min_used,
                     n_tiles):
    @pl.when(pl.program_id(0) == 0)
    def _():
        zbuf[...] = jnp.zeros_like(zbuf)
        grp = tm // MOE_ROW_GROUPS

        def fill(row):
            return pltpu.make_async_copy(
                zbuf, xs_ref.at[pl.ds(pl.multiple_of(row * SUBLANES, SUBLANES), grp * SUBLANES), :], zsem)

        groups = range(MOE_ROW_GROUPS)
        fills = [(gi >= grp_ref[e], last_ref[e] + gi * grp) for e in range(N_EXPERTS) for gi in groups]
        fills += [(t >= nv_ref[0], t * tm + gi * grp) for t in range(min_used, n_tiles) for gi in groups]
        for pred, row in fills:
            pl.when(pred)(lambda row=row: fill(row).start())
        for pred, row in fills:
            pl.when(pred)(lambda row=row: fill(row).wait())

    def issue(r, c):
        for kk in range(TOP_K):
            _tile_copy(at_ref, r * SUBLANES, xs_ref, pos_ref[kk * td + r], sem).start(priority=kk % 2)
        return c

    lax.fori_loop(0, td, issue, 0, unroll=8)
    for _ in range(TOP_K):
        pltpu.make_async_copy(at_ref, xs_ref.at[pl.ds(0, td * SUBLANES), :], sem).wait()


def _dispatch(last_tile_row, last_group, nv, pos8, at, n_tiles, tm):
    n = at.shape[0] // SUBLANES
    td = min(TD_DISPATCH, n)
    grid_spec = pltpu.PrefetchScalarGridSpec(
        num_scalar_prefetch=3,
        grid=(n // td,),
        in_specs=[pl.BlockSpec((TOP_K * td,), lambda i, *_: (i,), memory_space=pltpu.SMEM),
                  pl.BlockSpec((td * SUBLANES, LANES), lambda i, *_: (i, 0))],
        out_specs=pl.BlockSpec(memory_space=pl.ANY),
        scratch_shapes=[pltpu.VMEM((tm // MOE_ROW_GROUPS * SUBLANES, LANES), F32), pltpu.SemaphoreType.DMA,
                        pltpu.SemaphoreType.DMA],
    )
    min_used = n * TOP_K // tm
    return pl.pallas_call(
        functools.partial(_dispatch_kernel, td=td, tm=tm, min_used=min_used, n_tiles=n_tiles),
        grid_spec=grid_spec,
        out_shape=jax.ShapeDtypeStruct((n_tiles * tm * SUBLANES, LANES), F32),
        compiler_params=_cparams(1),
        name="dispatch",
    )(last_tile_row, last_group, nv, pos8, at)


def _moe_kernel(te_ref, rows_ref, slot_ref, nxt_ref, xs_ref, wgu_hbm, bgu_ref, wd_hbm, bd_ref, y_ref,
                wgu_f32, wd_f32, wgu_bf, wd_bf, sem, *, ch, tm):
    i = pl.program_id(0)
    d_exp = wd_hbm.shape[1]
    rows_used = rows_ref[i]
    valid = rows_used > 0
    new_expert = jnp.logical_or(i == 0, te_ref[i] != te_ref[jnp.maximum(i - 1, 0)])

    def weight_copies(e, sl):
        return (pltpu.make_async_copy(wgu_hbm.at[e], wgu_f32.at[sl], sem.at[sl, 0]),
                pltpu.make_async_copy(wd_hbm.at[e], wd_f32.at[sl], sem.at[sl, 1]))

    @pl.when(i == 0)
    def _():
        for c in weight_copies(te_ref[0], slot_ref[0]):
            c.start()

    @pl.when(jnp.logical_and(new_expert, valid))
    def _():
        sl = slot_ref[i]
        for c in weight_copies(te_ref[i], sl):
            c.wait()
        wgu_bf[...] = wgu_f32[sl].astype(BF16)
        wd_bf[...] = wd_f32[sl].astype(BF16)

        @pl.when(nxt_ref[i] >= 0)
        def _():
            for c in weight_copies(nxt_ref[i], 1 - sl):
                c.start()

    def ffn(rows):
        x = _load_token_tiles(xs_ref, rows).astype(BF16)
        acc = jnp.zeros((rows, wd_hbm.shape[2]), F32)
        for c in range(d_exp // ch):
            def gu(lo):
                return _dot(x, wgu_bf[:, lo:lo + ch]) + bgu_ref[0, :, lo:lo + ch]

            gate = jnp.minimum(gu(c * ch), SWIGLU_LIMIT)
            up = jnp.clip(gu(d_exp + c * ch), -SWIGLU_LIMIT, SWIGLU_LIMIT)
            act = gate * (1.0 / (1.0 + jnp.exp(-SWIGLU_ALPHA * gate))) * (up + 1.0)
            acc = acc + _dot(act.astype(BF16), wd_bf[c * ch:(c + 1) * ch, :])
        return acc + bd_ref[0]

    grp = tm // MOE_ROW_GROUPS
    for used in range(1, MOE_ROW_GROUPS + 1):
        rows = used * grp

        def partial_tile(rows=rows):
            _store_token_tiles(y_ref, ffn(rows))
            if rows < tm:
                y_ref[rows * SUBLANES:, :] = jnp.zeros(((tm - rows) * SUBLANES, LANES), F32)

        pl.when(jnp.logical_and(rows_used > rows - grp, rows_used <= rows))(partial_tile)

    @pl.when(jnp.logical_not(valid))
    def _():
        y_ref[...] = jnp.zeros_like(y_ref)


def _moe(te, rows, slot, nxt, xs, wgu, bgu, wd, bd, tm):
    n_tiles = xs.shape[0] // (tm * SUBLANES)
    _, d, d_exp2 = wgu.shape
    d_exp = wd.shape[1]
    assert d == SUBLANES * LANES and wd.shape[2] == d
    tile = pl.BlockSpec((tm * SUBLANES, LANES), lambda i, te, *_: (i, 0))
    grid_spec = pltpu.PrefetchScalarGridSpec(
        num_scalar_prefetch=4,
        grid=(n_tiles,),
        in_specs=[
            tile,
            pl.BlockSpec(memory_space=pl.ANY),
            pl.BlockSpec((1, 1, d_exp2), lambda i, te, *_: (te[i], 0, 0)),
            pl.BlockSpec(memory_space=pl.ANY),
            pl.BlockSpec((1, 1, d), lambda i, te, *_: (te[i], 0, 0)),
        ],
        out_specs=tile,
        scratch_shapes=[pltpu.VMEM((2, d, d_exp2), F32), pltpu.VMEM((2, d_exp, d), F32),
                        pltpu.VMEM((d, d_exp2), BF16), pltpu.VMEM((d_exp, d), BF16),
                        pltpu.SemaphoreType.DMA((2, 2))],
    )
    return pl.pallas_call(
        functools.partial(_moe_kernel, ch=min(MOE_CH, d_exp), tm=tm),
        grid_spec=grid_spec,
        out_shape=jax.ShapeDtypeStruct(xs.shape, F32),
        compiler_params=_cparams(1),
        name="moe",
    )(te, rows, slot, nxt, xs, wgu, bgu, wd, bd)


def _combine_kernel(pos_ref, posn_ref, h2_ref, meta_ref, y_ref, gf_ref, o_ref, ybuf, sem, *, td, final_norm):
    i = pl.program_id(0)
    slot = i % 2

    def issue_block(p_ref, sl):
        def issue(r, c):
            for kk in range(TOP_K):
                _tile_copy(y_ref, p_ref[kk * td + r], ybuf.at[sl, kk], r * SUBLANES,
                           sem.at[sl]).start(priority=kk % 2)
            return c

        lax.fori_loop(0, td, issue, 0, unroll=8)

    @pl.when(i == 0)
    def _():
        issue_block(pos_ref, 0)

    @pl.when(i + 1 < pl.num_programs(0))
    def _():
        issue_block(posn_ref, 1 - slot)

    for kk in range(TOP_K):
        pltpu.make_async_copy(y_ref.at[pl.ds(0, td * SUBLANES), :], ybuf.at[slot, kk], sem.at[slot]).wait()

    meta = meta_ref[...]
    acc = h2_ref[...]
    for kk in range(TOP_K):
        gate = meta[:, 2 * TOP_K + kk:2 * TOP_K + kk + 1]
        acc = acc + gate * _load_token_tiles(ybuf.at[slot, kk], td)
    o_ref[...] = _rms(acc, gf_ref[...]) if final_norm else acc


def _combine(pos8, h2, meta, y, g_final, final_norm):
    n, d = h2.shape
    td = min(TD_COMBINE, n)
    n_blocks = n // td
    grid_spec = pltpu.PrefetchScalarGridSpec(
        num_scalar_prefetch=0,
        grid=(n_blocks,),
        in_specs=[pl.BlockSpec((TOP_K * td,), lambda i: (i,), memory_space=pltpu.SMEM),
                  pl.BlockSpec((TOP_K * td,), lambda i: (jnp.minimum(i + 1, n_blocks - 1),), memory_space=pltpu.SMEM),
                  pl.BlockSpec((td, d), lambda i: (i, 0)),
                  pl.BlockSpec((td, LANES), lambda i: (i, 0)),
                  pl.BlockSpec(memory_space=pl.ANY),
                  pl.BlockSpec(g_final.shape, lambda i: (0, 0))],
        out_specs=pl.BlockSpec((td, d), lambda i: (i, 0)),
        scratch_shapes=[pltpu.VMEM((2, TOP_K, td * SUBLANES, LANES), F32), pltpu.SemaphoreType.DMA((2,))],
    )
    return pl.pallas_call(
        functools.partial(_combine_kernel, td=td, final_norm=final_norm),
        grid_spec=grid_spec,
        out_shape=jax.ShapeDtypeStruct((n, d), F32),
        compiler_params=_cparams(1),
        name="combine",
    )(pos8, pos8, h2, meta, y, g_final)


def _rope_tables(seq_len):
    inv = 1.0 / (ROPE_BASE ** (jnp.arange(0, MLA_ROPE_DIM, 2, dtype=F32) / MLA_ROPE_DIM))
    ang = jnp.arange(seq_len, dtype=F32)[:, None] * inv[None, :]
    cos, sin = jnp.cos(ang), jnp.sin(ang)
    pad = jnp.zeros((seq_len, LANES - 2 * MLA_ROPE_DIM), F32)
    cos_t = jnp.concatenate([cos, cos, cos, cos, pad], axis=1)
    sin_t = jnp.concatenate([sin, sin, sin, sin, pad], axis=1)
    return cos_t, sin_t


def _rot_cols(w):
    half = MLA_ROPE_DIM // 2
    return jnp.concatenate([-w[:, half:], w[:, :half]], axis=1)


def _rope_pair_block(r0, r1):
    pad = jnp.zeros((r0.shape[0], LANES - 2 * MLA_ROPE_DIM), F32)
    return jnp.concatenate([r0, r1, pad], axis=1)


def _prep_inproj_weights(w_in, w_uq, w_ukv):
    d = w_in.shape[0]
    pts = np.cumsum((FOX_WIDTH, FOX_WIDTH, FOX_WIDTH, FOX_HEADS, MLA_Q_LORA, MLA_KV_LORA, MLA_ROPE_DIM))
    w_fl, w_ql = w_in[:, pts[2]:pts[3]], w_in[:, pts[3]:pts[4]]
    w_kvl, w_kr = w_in[:, pts[4]:pts[5]], w_in[:, pts[5]:pts[6]]
    misc = jnp.concatenate([w_fl, jnp.zeros((d, LANES - FOX_HEADS), F32)], axis=1)
    w_tail = jnp.concatenate([w_ql, w_kvl, misc, _rope_pair_block(w_kr, w_kr),
                              _rope_pair_block(_rot_cols(w_kr), _rot_cols(w_kr))], axis=1).astype(BF16)
    qd = MLA_NOPE_DIM + MLA_ROPE_DIM
    kvd = MLA_NOPE_DIM + MLA_V_DIM
    q_nope = [w_uq[:, h * qd:h * qd + MLA_NOPE_DIM] for h in range(MLA_HEADS)]
    q_rope = [w_uq[:, h * qd + MLA_NOPE_DIM:(h + 1) * qd] for h in range(MLA_HEADS)]
    wqa = [_rope_pair_block(q_rope[h], q_rope[h + 1]) for h in range(0, MLA_HEADS, 2)]
    wqb = [_rope_pair_block(_rot_cols(q_rope[h]), _rot_cols(q_rope[h + 1])) for h in range(0, MLA_HEADS, 2)]
    wk = [w_ukv[:, h * kvd:h * kvd + MLA_NOPE_DIM] for h in range(MLA_HEADS)]
    wv = [w_ukv[:, h * kvd + MLA_NOPE_DIM:(h + 1) * kvd] for h in range(MLA_HEADS)]
    cat = lambda xs: jnp.concatenate(xs, axis=1).astype(BF16)
    w1 = jnp.concatenate([w_in[:, :pts[2]].astype(BF16), w_tail], axis=1)
    return w1, cat(q_nope), cat(wqa), cat(wqb), cat(wk), cat(wv)


def _routing_tables(counts, n_tiles, tm):
    tiles_e = (counts + tm - 1) // tm
    tile_end = jnp.cumsum(tiles_e)
    off = ((tile_end - tiles_e) * tm).astype(I32)
    total = tile_end[-1]
    ti = jnp.arange(n_tiles, dtype=I32)
    te = jnp.minimum(jnp.sum(ti[:, None] >= tile_end[None, :], axis=1), N_EXPERTS - 1).astype(I32)
    valid = ti < total
    last_e = jnp.sum(jnp.where(ti == total - 1, te, 0))
    te = jnp.where(valid, te, last_e).astype(I32)
    last_tile_row = (off + (tiles_e - 1) * tm).astype(I32)
    grp = tm // MOE_ROW_GROUPS
    last_group = jnp.where(counts > 0, ((counts - 1) % tm) // grp, MOE_ROW_GROUPS).astype(I32)
    used = tiles_e > 0
    ids = jnp.arange(N_EXPERTS, dtype=I32)
    slot_e = (jnp.cumsum(used.astype(I32)) - 1) % 2
    later = jnp.where(used[None, :] & (ids[None, :] > ids[:, None]), ids[None, :], N_EXPERTS)
    nxt_e = jnp.min(later, axis=1)
    nxt_e = jnp.where(nxt_e == N_EXPERTS, -1, nxt_e).astype(I32)
    pick = te[:, None] == ids[None, :]
    slot = jnp.sum(jnp.where(pick, slot_e[None, :], 0), axis=1).astype(I32)
    nxt = jnp.sum(jnp.where(pick, nxt_e[None, :], 0), axis=1).astype(I32)
    end_row = jnp.sum(jnp.where(pick, (off + counts)[None, :], 0), axis=1)
    rows = jnp.where(valid, jnp.clip(end_row - ti * tm, 0, tm), 0).astype(I32)
    return off, last_tile_row, last_group, te, total.astype(I32).reshape(1), slot, nxt, rows


def kernel(x, mem, g_mix, w_in, b_f, g_q_lat, w_uq, g_kv_lat, w_ukv, g_fox_out, g_mla_out, w_o, g_mem_q, w_mem_q,
           g_mem_kv, w_mem_kv, w_mem_o, g_ffn, w_router, b_router, w_gate_up, b_gate_up, w_down, b_down, g_final):
    b, s, d = x.shape
    n = b * s
    depth = g_mix.shape[0]
    cos_t, sin_t = _rope_tables(s)
    row = lambda v: v.reshape(1, -1)
    h = x
    for l in range(depth):
        w1, wqn, wqa, wqb, wk, wv = _prep_inproj_weights(w_in[l], w_uq[l], w_ukv[l])
        fq, fk, fv, flog, qm, km, vm = _inproj(h, row(g_mix[l]), w1, row(g_q_lat[l]), wqn, wqa, wqb,
                                                row(g_kv_lat[l]), wk, wv, cos_t, sin_t)
        cq, ck = _decay(flog, b_f[l])
        o_fox = _attention_pipelined(True, fq, fk, fv, cq, ck)
        o_mla = _attention_pipelined(False, qm, km, vm)
        kmem, vmem = _memkv(mem, row(g_mem_kv[l]), w_mem_kv[l].astype(BF16))
        wr = jnp.zeros((d, LANES), F32).at[:, :N_EXPERTS].set(w_router[l])
        wr_hi = wr.astype(BF16)
        wr_lo = (wr - wr_hi.astype(F32)).astype(BF16)
        br = jnp.zeros((1, LANES), F32).at[0, :N_EXPERTS].set(b_router[l])
        h2, at, meta, meta_t, cnt = _memrouter(
            o_fox, o_mla, h, row(g_fox_out[l]), row(g_mla_out[l]), w_o[l].astype(BF16), row(g_mem_q[l]),
            w_mem_q[l].astype(BF16), kmem, vmem, w_mem_o[l].astype(BF16), row(g_ffn[l]), wr_hi, wr_lo, br)
        ek = meta_t[0:TOP_K].astype(I32)
        rk = meta_t[TOP_K:2 * TOP_K].astype(I32)
        n_tiles = n * TOP_K // TM_MOE + N_EXPERTS
        counts = cnt[0, :N_EXPERTS].astype(I32)
        off, last_tile_row, last_group, te, nv, slot, nxt, rows = _routing_tables(counts, n_tiles, TM_MOE)
        ids = jnp.arange(N_EXPERTS, dtype=I32)[:, None, None]
        off_of = jnp.sum(jnp.where(ek[None] == ids, off[:, None, None], 0), axis=0)
        pos8 = ((off_of + rk) * SUBLANES).astype(I32)

        def by_block(td):
            return pos8.reshape(TOP_K, n // td, td).transpose(1, 0, 2).reshape(-1)

        xs = _dispatch(last_tile_row, last_group, nv, by_block(min(TD_DISPATCH, n)), at, n_tiles, TM_MOE)
        y = _moe(te, rows, slot, nxt, xs, w_gate_up[l], b_gate_up[l].reshape(N_EXPERTS, 1, -1), w_down[l],
                 b_down[l].reshape(N_EXPERTS, 1, -1), TM_MOE)
        h = _combine(by_block(min(TD_COMBINE, n)), h2.reshape(n, d), meta.reshape(n, LANES), y, row(g_final),
                     l == depth - 1).reshape(b, s, d)
    return h
```
